```python
import jax
import jax.numpy as jnp
from jax import lax
import numpy as np

D_MODEL = 1024
BATCH = 32
SEQ = 256
DEPTH = 2
DEC_BATCH = 2
DEC_SEQ = 4096
PAST_LEN = 256

GRID_W = 64
EPS = 1e-6
CHUNK = 64
Q_BLOCK = 128

MLA_HEADS = 4
Q_LORA = 256
KV_LORA = 128
NOPE_DIM = 64
ROPE_DIM = 32
MLA_V_DIM = 64
ROPE_BASE = 10000.0
MLA_SCALE = (NOPE_DIM + ROPE_DIM) ** -0.5

GLA_HEADS = 4
GLA_DK = 64
GLA_DV = 64
GLA_GATE_RANK = 16
GLA_TAU = 16.0

GDN_HEADS = 8
GDN_DK = 64
GDN_DV = 64
CONV_W = 3

N_EXPERTS = 16
EXPERT_FF = 1024
EC_CAPACITY = 2

MLA_W = MLA_HEADS * MLA_V_DIM
GLA_QK = GLA_HEADS * GLA_DK
GLA_W = GLA_HEADS * GLA_DV
GDN_QK = GDN_HEADS * GDN_DK
GDN_W = GDN_HEADS * GDN_DV
MIX_W = MLA_W + GLA_W + GDN_W
IN_SIZES = (Q_LORA, KV_LORA, ROPE_DIM,
            GLA_QK, GLA_QK, GLA_W, 2 * GLA_GATE_RANK, GLA_W,
            GDN_QK, GDN_QK, GDN_W, GDN_W, 2 * GDN_HEADS, 2 * GDN_HEADS)
IN_DIM = sum(IN_SIZES)

kernel_name = 'hybrid_mla_gla_gdn_ec_flow_step'


def split_cols(t, sizes):
    outs, start = [], 0
    for s in sizes:
        outs.append(t[..., start:start + s])
        start += s
    return outs


def rms_norm(x, g):
    xf = x.astype(jnp.float32)
    y = xf * lax.rsqrt(jnp.mean(xf * xf, axis=-1, keepdims=True) + EPS)
    return (y * g.astype(jnp.float32)).astype(x.dtype)


def l2_norm(x):
    xf = x.astype(jnp.float32)
    return (xf * lax.rsqrt(jnp.sum(xf * xf, axis=-1, keepdims=True) + EPS)).astype(x.dtype)


def flip(t):
    return jnp.flip(t, axis=1)


def axial_rope(x):
    n = x.shape[1]
    rows = n // GRID_W
    row = jnp.repeat(jnp.arange(rows), GRID_W).astype(jnp.float32)
    col = jnp.tile(jnp.arange(GRID_W), rows).astype(jnp.float32)
    half = ROPE_DIM // 2
    freqs = ROPE_BASE ** (-jnp.arange(0, half, 2, dtype=jnp.float32) / half)
    ang = jnp.concatenate([row[:, None] * freqs, col[:, None] * freqs], axis=-1)
    ang = ang.reshape((n,) + (1,) * (x.ndim - 3) + (half,))
    cos, sin = jnp.cos(ang), jnp.sin(ang)
    xf = x.astype(jnp.float32)
    x1, x2 = xf[..., 0::2], xf[..., 1::2]
    out = jnp.stack([x1 * cos - x2 * sin, x1 * sin + x2 * cos], axis=-1)
    return out.reshape(x.shape).astype(x.dtype)


def block_attention(q, k, v, scale):
    b, nq, h, dk = q.shape
    nb = nq // Q_BLOCK
    qb = jnp.moveaxis(q.reshape(b, nb, Q_BLOCK, h, dk), 1, 0)

    def one_block(qi):
        s = jnp.einsum('bqhd,bkhd->bhqk', qi, k).astype(jnp.float32) * scale
        p = jax.nn.softmax(s, axis=-1).astype(v.dtype)
        return jnp.einsum('bhqk,bkhd->bqhd', p, v)

    o = lax.map(one_block, qb)
    return jnp.moveaxis(o, 0, 1).reshape(b, nq, h, v.shape[-1])


def expand_kv(ckv, w_kv_b):
    b, n, _ = ckv.shape
    kv = (ckv @ w_kv_b).reshape(b, n, MLA_HEADS, NOPE_DIM + MLA_V_DIM)
    return kv[..., :NOPE_DIM], kv[..., NOPE_DIM:]


def heads_bcast(t):
    return jnp.broadcast_to(t[:, :, None, :], t.shape[:2] + (MLA_HEADS, t.shape[-1]))


def short_conv(x, w):
    c = x.shape[-1]
    y = lax.conv_general_dilated(x, w[:, None, :].astype(x.dtype), window_strides=(1,),
                                 padding=[(CONV_W // 2, CONV_W // 2)],
                                 dimension_numbers=('NWC', 'WIO', 'NWC'), feature_group_count=c)
    return jax.nn.silu(y)


def gla_chunked(q, k, v, log_a, s0):
    out_dtype = v.dtype
    b, n, h, _ = q.shape
    nc = n // CHUNK

    def chunks(t):
        return t.astype(jnp.float32).reshape(b, nc, CHUNK, h, t.shape[-1])

    q, k, v, log_a = chunks(q), chunks(k), chunks(v), chunks(log_a)
    g = jnp.cumsum(log_a, axis=2)
    g_last = g[:, :, -1:]
    q_dec = q * jnp.exp(g)
    k_inv = k * jnp.exp(-g)
    k_end = k * jnp.exp(g_last - g)
    causal = jnp.tril(jnp.ones((CHUNK, CHUNK), dtype=bool))
    a = jnp.where(causal, jnp.einsum('bnihk,bnjhk->bnhij', q_dec, k_inv), 0.0)
    o_intra = jnp.einsum('bnhij,bnjhv->bnihv', a, v)

    def step(s, inp):
        qc, kc, vc, dc = inp
        o = jnp.einsum('bihk,bhkv->bihv', qc, s)
        s = s * dc[..., None] + jnp.einsum('bihk,bihv->bhkv', kc, vc)
        return s, o

    xs = tuple(jnp.moveaxis(t, 1, 0) for t in (q_dec, k_end, v, jnp.exp(g_last[:, :, 0])))
    s_fin, o_inter = lax.scan(step, s0.astype(jnp.float32), xs)
    o = o_intra + jnp.moveaxis(o_inter, 0, 1)
    return o.reshape(b, n, h, -1).astype(out_dtype), s_fin.astype(out_dtype)


def gdn_chunked(q, k, v, beta, log_a, s0):
    out_dtype = v.dtype
    b, n, h, _ = q.shape
    nc = n // CHUNK

    def chunks(t):
        return t.astype(jnp.float32).reshape((b, nc, CHUNK) + t.shape[2:])

    q, k, v, beta, log_a = (chunks(t) for t in (q, k, v, beta, log_a))
    g = jnp.cumsum(log_a, axis=2)
    gh = jnp.swapaxes(g, 2, 3)
    diff = gh[..., :, None] - gh[..., None, :]
    incl = jnp.tril(jnp.ones((CHUNK, CHUNK), dtype=bool))
    strict = jnp.tril(jnp.ones((CHUNK, CHUNK), dtype=bool), k=-1)
    k_beta = k * beta[..., None]
    m = jnp.einsum('bnihk,bnjhk->bnhij', k_beta, k) * jnp.exp(jnp.where(strict, diff, -jnp.inf))
    eye = jnp.broadcast_to(jnp.eye(CHUNK, dtype=jnp.float32), m.shape)
    t_inv = lax.linalg.triangular_solve(m, eye, left_side=True, lower=True, unit_diagonal=True)
    u = jnp.einsum('bnhij,bnjhv->bnihv', t_inv, v * beta[..., None])
    w = jnp.einsum('bnhij,bnjhk->bnihk', t_inv, k_beta * jnp.exp(g)[..., None])
    qk = jnp.einsum('bnihk,bnjhk->bnhij', q, k) * jnp.exp(jnp.where(incl, diff, -jnp.inf))
    q_dec = q * jnp.exp(g)[..., None]
    g_last = g[:, :, -1]
    k_end = k * jnp.exp(g_last[:, :, None] - g)[..., None]

    def step(s, inp):
        qc, kc, uc, wc, ac, dl = inp
        v_new = uc - jnp.einsum('bihk,bhkv->bihv', wc, s)
        o = jnp.einsum('bihk,bhkv->bihv', qc, s) + jnp.einsum('bhij,bjhv->bihv', ac, v_new)
        s = s * jnp.exp(dl)[..., None, None] + jnp.einsum('bihk,bihv->bhkv', kc, v_new)
        return s, o

    xs = tuple(jnp.moveaxis(t, 1, 0) for t in (q_dec, k_end, u, w, qk, g_last))
    s_fin, o = lax.scan(step, s0.astype(jnp.float32), xs)
    o = jnp.moveaxis(o, 0, 1)
    return o.reshape(b, n, h, -1).astype(out_dtype), s_fin.astype(out_dtype)


def bidirectional(scan_fn, shared, gates_f, gates_b, s0):
    o_f, s_f = scan_fn(*shared, *gates_f, s0[:, 0])
    o_b, s_b = scan_fn(*[flip(t) for t in shared], *[flip(t) for t in gates_b], s0[:, 1])
    return o_f + flip(o_b), jnp.stack([s_f, s_b], axis=1)


def token_mixers(xm, l, W, ctx_cache):
    b, n, _ = xm.shape
    dtype = xm.dtype
    f32 = jnp.float32
    proj = xm @ W['w_in'][l]
    (q_lat, kv_lat, k_rope, gla_q, gla_k, gla_v, gla_glr, gla_z,
     gdn_q, gdn_k, gdn_v, gdn_z, gdn_b, gdn_a) = split_cols(proj, IN_SIZES)

    q = (rms_norm(q_lat, W['mla_q_norm'][l]) @ W['mla_wq_b'][l]).reshape(b, n, MLA_HEADS, NOPE_DIM + ROPE_DIM)
    q_nope, q_rope = q[..., :NOPE_DIM], q[..., NOPE_DIM:]
    ckv = rms_norm(kv_lat, W['mla_kv_norm'][l])
    k_nope, v = expand_kv(ckv, W['mla_wkv_b'][l])
    if ctx_cache is None:
        qa = jnp.concatenate([q_nope, q_rope], axis=-1)
        ka = jnp.concatenate([k_nope, heads_bcast(k_rope)], axis=-1)
        va = v
        s0_gla = jnp.zeros((b, 2, GLA_HEADS, GLA_DK, GLA_DV), dtype)
        s0_gdn = jnp.zeros((b, 2, GDN_HEADS, GDN_DK, GDN_DV), dtype)
    else:
        ckv_c, kr_c, s0_gla, s0_gdn = ctx_cache
        lc = ckv_c.shape[1]
        k_nope_c, v_c = expand_kv(ckv_c, W['mla_wkv_b'][l])
        zr_lat = jnp.zeros((b, n, MLA_HEADS, ROPE_DIM), dtype)
        zr_ctx = jnp.zeros((b, lc, MLA_HEADS, ROPE_DIM), dtype)
        qa = jnp.concatenate([q_nope, axial_rope(q_rope), q_rope], axis=-1)
        k_lat = jnp.concatenate([k_nope, heads_bcast(axial_rope(k_rope)), zr_lat], axis=-1)
        k_ctx = jnp.concatenate([k_nope_c, zr_ctx, heads_bcast(kr_c)], axis=-1)
        ka = jnp.concatenate([k_ctx, k_lat], axis=1)
        va = jnp.concatenate([v_c, v], axis=1)
    o_mla = block_attention(qa, ka, va, MLA_SCALE).reshape(b, n, MLA_W)

    gq = gla_q.reshape(b, n, GLA_HEADS, GLA_DK) * (GLA_DK ** -0.5)
    gk = gla_k.reshape(b, n, GLA_HEADS, GLA_DK)
    gv = gla_v.reshape(b, n, GLA_HEADS, GLA_DV)
    gla_la = []
    for d in range(2):
        lr = gla_glr[..., d * GLA_GATE_RANK:(d + 1) * GLA_GATE_RANK]
        logit = (lr @ W['gla_wg'][l, d] + W['gla_bg'][l, d]).astype(f32)
        gla_la.append((jax.nn.log_sigmoid(logit) / GLA_TAU).reshape(b, n, GLA_HEADS, GLA_DK))
    o_gla, st_gla = bidirectional(gla_chunked, (gq, gk, gv), (gla_la[0],), (gla_la[1],), s0_gla)
    o_gla = rms_norm(o_gla, W['gla_out_norm'][l]) * jax.nn.silu(gla_z.reshape(b, n, GLA_HEADS, GLA_DV))

    qkv = short_conv(jnp.concatenate([gdn_q, gdn_k, gdn_v], axis=-1), W['gdn_conv'][l])
    dq, dk, dv = split_cols(qkv, (GDN_QK, GDN_QK, GDN_W))
    dq = l2_norm(dq.reshape(b, n, GDN_HEADS, GDN_DK)) * (GDN_DK ** -0.5)
    dk = l2_norm(dk.reshape(b, n, GDN_HEADS, GDN_DK))
    dv = dv.reshape(b, n, GDN_HEADS, GDN_DV)
    gdn_gates = []
    for d in range(2):
        sl = slice(d * GDN_HEADS, (d + 1) * GDN_HEADS)
        beta = jax.nn.sigmoid(gdn_b[..., sl].astype(f32))
        la = -jnp.exp(W['gdn_a_log'][l, d].astype(f32)) * jax.nn.softplus(
            gdn_a[..., sl].astype(f32) + W['gdn_dt_bias'][l, d].astype(f32))
        gdn_gates.append((beta, la))
    o_gdn, st_gdn = bidirectional(gdn_chunked, (dq, dk, dv), gdn_gates[0], gdn_gates[1], s0_gdn)
    o_gdn = rms_norm(o_gdn, W['gdn_out_norm'][l]) * jax.nn.silu(gdn_z.reshape(b, n, GDN_HEADS, GDN_DV))

    o = jnp.concatenate([o_mla, o_gla.reshape(b, n, GLA_W), o_gdn.reshape(b, n, GDN_W)], axis=-1)
    out = o @ W['w_out'][l]
    new = (ckv, k_rope, st_gla, st_gdn) if ctx_cache is None else None
    return out, new


def expert_choice_ffn(xm, l, W):
    b, n, d = xm.shape
    t = b * n
    xf = xm.reshape(t, d)
    aff = jax.nn.softmax((xf @ W['w_router'][l]).astype(jnp.float32), axis=-1)
    cap = EC_CAPACITY * t // N_EXPERTS
    gate, idx = lax.top_k(aff.T, cap)
    xe = xf[idx]
    h = jax.nn.silu(jnp.einsum('ecd,edf->ecf', xe, W['w_e1'][l])) * jnp.einsum('ecd,edf->ecf', xe, W['w_e3'][l])
    ye = jnp.einsum('ecf,efd->ecd', h, W['w_e2'][l]) * gate[..., None].astype(xm.dtype)
    out = jnp.zeros_like(xf).at[idx.reshape(-1)].add(ye.reshape(-1, d))
    return out.reshape(b, n, d)


def trunk_layer(x, l, W, cond, ctx_cache):
    mod = jax.nn.silu(cond) @ W['w_mod'][l] + W['b_mod'][l]
    sh1, sc1, g1, sh2, sc2, g2 = jnp.split(mod[:, None, :], 6, axis=-1)
    nrm = W['sandwich_norms'][l]
    xm = rms_norm(x, nrm[0]) * (1 + sc1) + sh1
    y, new = token_mixers(xm, l, W, ctx_cache)
    x = x + g1 * rms_norm(y, nrm[1])
    xm = rms_norm(x, nrm[2]) * (1 + sc2) + sh2
    x = x + g2 * rms_norm(expert_choice_ffn(xm, l, W), nrm[3])
    return x, new


def setup_inputs(seed: int = 0) -> dict:
    key = jax.random.key(seed)
    ks = iter(jax.random.split(key, 40))
    f32 = jnp.float32

    def nrm(shape, scale):
        return jax.random.normal(next(ks), shape, f32) * scale

    def gain(shape):
        return 1.0 + nrm(shape, 0.02)

    dt = jnp.exp(jax.random.uniform(next(ks), (DEPTH, 2, GDN_HEADS), f32,
                                    minval=float(np.log(1e-3)), maxval=float(np.log(1e-1))))
    return {
        'x_prompt': nrm((BATCH, SEQ, D_MODEL), 1.0),
        'x_sample': nrm((DEC_BATCH, DEC_SEQ, D_MODEL), 1.0),
        'cache_ckv': nrm((DEC_BATCH, DEPTH, PAST_LEN, KV_LORA), 1.0),
        'cache_krope': nrm((DEC_BATCH, DEPTH, PAST_LEN, ROPE_DIM), 1.0),
        'state_gla': nrm((DEC_BATCH, DEPTH, 2, GLA_HEADS, GLA_DK, GLA_DV), 0.2),
        'state_gdn': nrm((DEC_BATCH, DEPTH, 2, GDN_HEADS, GDN_DK, GDN_DV), 0.2),
        'c': nrm((DEC_BATCH, D_MODEL), 1.0),
        'c_ctx': nrm((D_MODEL,), 1.0),
        'w_in': nrm((DEPTH, D_MODEL, IN_DIM), D_MODEL ** -0.5),
        'mla_q_norm': gain((DEPTH, Q_LORA)),
        'mla_wq_b': nrm((DEPTH, Q_LORA, MLA_HEADS * (NOPE_DIM + ROPE_DIM)), Q_LORA ** -0.5),
        'mla_kv_norm': gain((DEPTH, KV_LORA)),
        'mla_wkv_b': nrm((DEPTH, KV_LORA, MLA_HEADS * (NOPE_DIM + MLA_V_DIM)), KV_LORA ** -0.5),
        'gla_wg': nrm((DEPTH, 2, GLA_GATE_RANK, GLA_QK), GLA_GATE_RANK ** -0.5),
        'gla_bg': nrm((DEPTH, 2, GLA_QK), 0.1),
        'gla_out_norm': gain((DEPTH, GLA_DV)),
        'gdn_conv': nrm((DEPTH, CONV_W, 2 * GDN_QK + GDN_W), CONV_W ** -0.5),
        'gdn_a_log': jnp.log(jax.random.uniform(next(ks), (DEPTH, 2, GDN_HEADS), f32, minval=1.0, maxval=16.0)),
        'gdn_dt_bias': dt + jnp.log(-jnp.expm1(-dt)),
        'gdn_out_norm': gain((DEPTH, GDN_DV)),
        'w_out': nrm((DEPTH, MIX_W, D_MODEL), MIX_W ** -0.5),
        'w_mod': nrm((DEPTH, D_MODEL, 6 * D_MODEL), 0.5 * D_MODEL ** -0.5),
        'b_mod': nrm((DEPTH, 6 * D_MODEL), 0.02),
        'sandwich_norms': gain((DEPTH, 4, D_MODEL)),
        'w_router': nrm((DEPTH, D_MODEL, N_EXPERTS), D_MODEL ** -0.5),
        'w_e1': nrm((DEPTH, N_EXPERTS, D_MODEL, EXPERT_FF), D_MODEL ** -0.5),
        'w_e3': nrm((DEPTH, N_EXPERTS, D_MODEL, EXPERT_FF), D_MODEL ** -0.5),
        'w_e2': nrm((DEPTH, N_EXPERTS, EXPERT_FF, D_MODEL), EXPERT_FF ** -0.5),
    }


def reference(x_prompt, x_sample, cache_ckv, cache_krope, state_gla, state_gdn, c, c_ctx,
              w_in, mla_q_norm, mla_wq_b, mla_kv_norm, mla_wkv_b, gla_wg, gla_bg, gla_out_norm,
              gdn_conv, gdn_a_log, gdn_dt_bias, gdn_out_norm, w_out, w_mod, b_mod, sandwich_norms,
              w_router, w_e1, w_e3, w_e2):
    W = dict(w_in=w_in, mla_q_norm=mla_q_norm, mla_wq_b=mla_wq_b, mla_kv_norm=mla_kv_norm,
             mla_wkv_b=mla_wkv_b, gla_wg=gla_wg, gla_bg=gla_bg, gla_out_norm=gla_out_norm,
             gdn_conv=gdn_conv, gdn_a_log=gdn_a_log, gdn_dt_bias=gdn_dt_bias, gdn_out_norm=gdn_out_norm,
             w_out=w_out, w_mod=w_mod, b_mod=b_mod, sandwich_norms=sandwich_norms,
             w_router=w_router, w_e1=w_e1, w_e3=w_e3, w_e2=w_e2)

    xp = x_prompt
    ckv_l, kr_l, sg_l, sd_l = [], [], [], []
    for l in range(DEPTH):
        xp, (ckv, kr, sg, sd) = trunk_layer(xp, l, W, c_ctx[None, :], None)
        ckv_l.append(ckv)
        kr_l.append(kr)
        sg_l.append(sg)
        sd_l.append(sd)

    xs = x_sample
    for l in range(DEPTH):
        xs, _ = trunk_layer(xs, l, W, c,
                            (cache_ckv[:, l], cache_krope[:, l], state_gla[:, l], state_gdn[:, l]))

    return (xp, xs, jnp.stack(ckv_l, axis=1), jnp.stack(kr_l, axis=1),
            jnp.stack(sg_l, axis=1), jnp.stack(sd_l, axis=1))
```

```python
import functools

import numpy as np
import jax
import jax.numpy as jnp
from jax import lax
from jax.experimental import pallas as pl
from jax.experimental.pallas import tpu as pltpu

F32 = jnp.float32
BF16 = jnp.bfloat16
I32 = jnp.int32

D_MODEL = 1024
DEPTH = 2
GRID_W = 64
EPS = 1e-6
CHUNK = 64

MLA_HEADS = 4
Q_LORA = 256
KV_LORA = 128
NOPE_DIM = 64
ROPE_DIM = 32
MLA_V_DIM = 64
ROPE_BASE = 10000.0
MLA_SCALE = (NOPE_DIM + ROPE_DIM) ** -0.5
MLA_QK_PAD = 128

GLA_HEADS = 4
GLA_DK = 64
GLA_DV = 64
GLA_GATE_RANK = 16
GLA_TAU = 16.0

GDN_HEADS = 8
GDN_DK = 64
GDN_DV = 64

N_EXPERTS = 16
EXPERT_FF = 1024
EC_CAPACITY = 2

LANES = 128
HEAD_W = 64
MLA_W = MLA_HEADS * MLA_V_DIM
GLA_W = GLA_HEADS * GLA_DV
GDN_W = GDN_HEADS * GDN_DV

COL_QLAT = 0
COL_KVLAT = 2
COL_GLA_Q = 3
COL_GLA_K = 5
COL_GLA_V = 7
COL_GLA_Z = 9
COL_GDN_Q = 11
COL_GDN_K = 15
COL_GDN_V = 19
COL_GDN_Z = 23
COL_SMALL = 27
PROJ_W = 28 * LANES
SM_KROPE = 0
SM_KROT = 32
SM_GLR = 64
SM_BETA = 96
SM_A = 112

VMEM_LIMIT = 56 * 1024 * 1024


def _cparams(sem=None):
    return pltpu.CompilerParams(dimension_semantics=sem, vmem_limit_bytes=VMEM_LIMIT)


def _dot(a, b):
    return jnp.dot(a.astype(BF16), b.astype(BF16), preferred_element_type=F32)


def _dot_nt(a, b):
    return lax.dot_general(a.astype(BF16), b.astype(BF16), (((1,), (1,)), ((), ())),
                           preferred_element_type=F32)


def _dot_tn(a, b):
    return lax.dot_general(a.astype(BF16), b.astype(BF16), (((0,), (0,)), ((), ())),
                           preferred_element_type=F32)


def _dot_x3(a, b):
    a1 = a.astype(BF16)
    a2 = (a - a1.astype(F32)).astype(BF16)
    b1 = b.astype(BF16)
    b2 = (b - b1.astype(F32)).astype(BF16)
    return (jnp.dot(a1, b1, preferred_element_type=F32) + jnp.dot(a1, b2, preferred_element_type=F32)
            + jnp.dot(a2, b1, preferred_element_type=F32))


def _split3(x):
    x1 = x.astype(BF16)
    r1 = x - x1.astype(F32)
    x2 = r1.astype(BF16)
    x3 = (r1 - x2.astype(F32)).astype(BF16)
    return x1, x2, x3


def _dot_sel(x, sel):
    x1, x2, x3 = _split3(x)
    s = sel.astype(BF16)
    return (jnp.dot(x1, s, preferred_element_type=F32) + jnp.dot(x2, s, preferred_element_type=F32)
            + jnp.dot(x3, s, preferred_element_type=F32))


def _sel_dot(sel, x):
    x1, x2, x3 = _split3(x)
    s = sel.astype(BF16)
    return (jnp.dot(s, x1, preferred_element_type=F32) + jnp.dot(s, x2, preferred_element_type=F32)
            + jnp.dot(s, x3, preferred_element_type=F32))


def _sel_dot_nt(sel, x):
    x1, x2, x3 = _split3(x)
    s = sel.astype(BF16)
    dn = (((1,), (1,)), ((), ()))
    return (lax.dot_general(s, x1, dn, preferred_element_type=F32)
            + lax.dot_general(s, x2, dn, preferred_element_type=F32)
            + lax.dot_general(s, x3, dn, preferred_element_type=F32))


def _silu(x):
    return x * jax.nn.sigmoid(x)


def _softplus(x):
    return jnp.maximum(x, 0.0) + jnp.log(1.0 + jnp.exp(-jnp.abs(x)))


def _log_sigmoid(x):
    return -_softplus(-x)


def _rms(x, g):
    return x * lax.rsqrt(jnp.mean(x * x, axis=-1, keepdims=True) + EPS) * g


def _iota2(shape, dim):
    return lax.broadcasted_iota(I32, shape, dim)


def _head_block_ones(width):
    r = _iota2((width, width), 0) // HEAD_W
    c = _iota2((width, width), 1) // HEAD_W
    return (r == c).astype(F32)


def _mod_kernel(cond_ref, w_ref, b_ref, o_ref):
    s = _silu(cond_ref[...])
    o_ref[0] = _dot(s, w_ref[0]) + b_ref[0]


def _modulation(cond, w_mod, b_mod):
    depth, d, d6 = w_mod.shape
    nj = d6 // d
    return pl.pallas_call(
        _mod_kernel,
        grid=(depth, nj),
        in_specs=[pl.BlockSpec((8, d), lambda l, j: (0, 0)),
                  pl.BlockSpec((1, d, d), lambda l, j: (l, 0, j)),
                  pl.BlockSpec((1, 1, d), lambda l, j: (l, 0, j))],
        out_specs=pl.BlockSpec((1, 8, d), lambda l, j: (l, 0, j)),
        out_shape=jax.ShapeDtypeStruct((depth, 8, d6), F32),
        compiler_params=_cparams(("arbitrary", "arbitrary")),
    )(cond, w_mod, b_mod.reshape(depth, 1, d6))


def _inproj_kernel(x_ref, mod_ref, nrm_ref, w_ref, o_ref, *, col_chunk):
    x = x_ref[...]
    m = mod_ref[0]
    xm = (_rms(x, nrm_ref[...]) * (1.0 + m[1:2]) + m[0:1]).astype(BF16)
    for j in range(o_ref.shape[1] // col_chunk):
        sl = slice(j * col_chunk, (j + 1) * col_chunk)
        o_ref[:, sl] = jnp.dot(xm, w_ref[:, sl], preferred_element_type=F32)


def _in_projection(x2d, modg, nrm0, w_in_p, tiles_per_batch, tm):
    t, d = x2d.shape
    pw = w_in_p.shape[1]
    per_batch = modg.shape[0] > 1
    return pl.pallas_call(
        functools.partial(_inproj_kernel, col_chunk=512),
        grid=(t // tm,),
        in_specs=[pl.BlockSpec((tm, d), lambda i: (i, 0)),
                  pl.BlockSpec((1, 6, d), (lambda i: (i // tiles_per_batch, 0, 0)) if per_batch
                               else (lambda i: (0, 0, 0))),
                  pl.BlockSpec((1, d), lambda i: (0, 0)),
                  pl.BlockSpec((d, pw), lambda i: (0, 0))],
        out_specs=pl.BlockSpec((tm, pw), lambda i: (i, 0)),
        out_shape=jax.ShapeDtypeStruct((t, pw), F32),
        compiler_params=_cparams(("arbitrary",)),
    )(x2d, modg, nrm0, w_in_p)


def _assemble_keys(k_nope, k_mid, k_last):
    parts = []
    for h in range(MLA_HEADS):
        parts += [k_nope[:, h * NOPE_DIM:(h + 1) * NOPE_DIM], k_mid, k_last]
    return jnp.concatenate(parts, axis=-1)


def _mla_prep_kernel(qlat_ref, kvlat_ref, sm_ref, t1_ref, t2_ref, t3_ref, gq_ref, wq_ref, gkv_ref, wkv_ref,
                     q_ref, k_ref, v_ref, ckv_ref, kr_ref):
    t1 = t1_ref[...]
    t2 = t2_ref[...]
    t3 = t3_ref[...]
    qn = _rms(qlat_ref[...], gq_ref[...])
    raw = _dot(qn, wq_ref[...])
    w = raw.shape[1]
    tile = lambda t: jnp.concatenate([t] * MLA_HEADS, axis=-1)
    q = raw * tile(t1) + pltpu.roll(raw, w - ROPE_DIM, 1) * tile(t2) + pltpu.roll(raw, ROPE_DIM, 1) * tile(t3)
    q_ref[...] = q.astype(q_ref.dtype)

    ckv = _rms(kvlat_ref[...], gkv_ref[...])
    ckv_ref[...] = ckv
    kv = _dot(ckv, wkv_ref[...])
    sm = sm_ref[...]
    k_rope = sm[:, SM_KROPE:SM_KROPE + ROPE_DIM]
    k_rot = sm[:, SM_KROT:SM_KROT + ROPE_DIM]
    kr_ref[...] = k_rope
    cs = slice(NOPE_DIM, NOPE_DIM + ROPE_DIM)
    kr = k_rope * t1[:, cs] + k_rot * t2[:, cs]
    k = _assemble_keys(kv[:, :MLA_HEADS * NOPE_DIM], kr, jnp.zeros_like(kr))
    k_ref[...] = k.astype(k_ref.dtype)
    v_ref[...] = kv[:, MLA_HEADS * NOPE_DIM:].astype(v_ref.dtype)


def _mla_prep(proj, tabs, gq, wq_ext, gkv, wkv_ext, tiles_per_seq, tm):
    t = proj.shape[0]
    t1, t2, t3 = tabs
    per_pos = t1.shape[0] > tm
    tab_map = (lambda i: (i % tiles_per_seq, 0)) if per_pos else (lambda i: (0, 0))
    qk_w = MLA_HEADS * MLA_QK_PAD
    return pl.pallas_call(
        _mla_prep_kernel,
        grid=(t // tm,),
        in_specs=[pl.BlockSpec((tm, Q_LORA), lambda i: (i, COL_QLAT * LANES // Q_LORA)),
                  pl.BlockSpec((tm, KV_LORA), lambda i: (i, COL_KVLAT)),
                  pl.BlockSpec((tm, LANES), lambda i: (i, COL_SMALL)),
                  pl.BlockSpec((tm, LANES), tab_map),
                  pl.BlockSpec((tm, LANES), tab_map),
                  pl.BlockSpec((tm, LANES), tab_map),
                  pl.BlockSpec((1, Q_LORA), lambda i: (0, 0)),
                  pl.BlockSpec((Q_LORA, qk_w), lambda i: (0, 0)),
                  pl.BlockSpec((1, KV_LORA), lambda i: (0, 0)),
                  pl.BlockSpec((KV_LORA, 2 * MLA_W), lambda i: (0, 0))],
        out_specs=[pl.BlockSpec((tm, qk_w), lambda i: (i, 0)),
                   pl.BlockSpec((tm, qk_w), lambda i: (i, 0)),
                   pl.BlockSpec((tm, MLA_W), lambda i: (i, 0)),
                   pl.BlockSpec((tm, KV_LORA), lambda i: (i, 0)),
                   pl.BlockSpec((tm, ROPE_DIM), lambda i: (i, 0))],
        out_shape=[jax.ShapeDtypeStruct((t, qk_w), BF16),
                   jax.ShapeDtypeStruct((t, qk_w), BF16),
                   jax.ShapeDtypeStruct((t, MLA_W), BF16),
                   jax.ShapeDtypeStruct((t, KV_LORA), F32),
                   jax.ShapeDtypeStruct((t, ROPE_DIM), F32)],
        compiler_params=_cparams(("arbitrary",)),
    )(proj, proj, proj, t1, t2, t3, gq, wq_ext, gkv, wkv_ext)


def _ctxkeys_kernel(ckv_ref, kr_ref, wkv_ref, k_ref, v_ref):
    kv = _dot(ckv_ref[...], wkv_ref[...])
    kr = kr_ref[...]
    k = _assemble_keys(kv[:, :MLA_HEADS * NOPE_DIM], jnp.zeros_like(kr), kr)
    k_ref[...] = k.astype(k_ref.dtype)
    v_ref[...] = kv[:, MLA_HEADS * NOPE_DIM:].astype(v_ref.dtype)


def _ctx_keys(ckv_c, kr_c, wkv_ext):
    t = ckv_c.shape[0]
    qk_w = MLA_HEADS * MLA_QK_PAD
    return pl.pallas_call(
        _ctxkeys_kernel,
        out_shape=[jax.ShapeDtypeStruct((t, qk_w), BF16), jax.ShapeDtypeStruct((t, MLA_W), BF16)],
        compiler_params=_cparams(),
    )(ckv_c, kr_c, wkv_ext)


def _attn_kernel(*refs, has_ctx):
    if has_ctx:
        q_ref, k_ref, v_ref, kc_ref, vc_ref, o_ref = refs
    else:
        q_ref, k_ref, v_ref, o_ref = refs
    outs = []
    for h in range(MLA_HEADS):
        qs = slice(h * MLA_QK_PAD, (h + 1) * MLA_QK_PAD)
        vs = slice(h * MLA_V_DIM, (h + 1) * MLA_V_DIM)
        q = q_ref[0, :, qs]
        s = _dot_nt(q, k_ref[0, :, qs]) * MLA_SCALE
        m = jnp.max(s, axis=-1, keepdims=True)
        if has_ctx:
            sc = _dot_nt(q, kc_ref[0, :, qs]) * MLA_SCALE
            m = jnp.maximum(m, jnp.max(sc, axis=-1, keepdims=True))
        p = jnp.exp(s - m)
        den = jnp.sum(p, axis=-1, keepdims=True)
        acc = _dot(p, v_ref[0, :, vs])
        if has_ctx:
            pc = jnp.exp(sc - m)
            den = den + jnp.sum(pc, axis=-1, keepdims=True)
            acc = acc + _dot(pc, vc_ref[0, :, vs])
        outs.append(acc / den)
    o_ref[0] = jnp.concatenate(outs, axis=-1)


def _attention(q, k, v, kc, vc, tq):
    b, n, qk_w = q.shape
    has_ctx = kc is not None
    in_specs = [pl.BlockSpec((1, tq, qk_w), lambda bi, i: (bi, i, 0)),
                pl.BlockSpec((1, n, qk_w), lambda bi, i: (bi, 0, 0)),
                pl.BlockSpec((1, n, MLA_W), lambda bi, i: (bi, 0, 0))]
    args = [q, k, v]
    if has_ctx:
        lc = kc.shape[1]
        in_specs += [pl.BlockSpec((1, lc, qk_w), lambda bi, i: (bi, 0, 0)),
                     pl.BlockSpec((1, lc, MLA_W), lambda bi, i: (bi, 0, 0))]
        args += [kc, vc]
    return pl.pallas_call(
        functools.partial(_attn_kernel, has_ctx=has_ctx),
        grid=(b, n // tq),
        in_specs=in_specs,
        out_specs=pl.BlockSpec((1, tq, MLA_W), lambda bi, i: (bi, i, 0)),
        out_shape=jax.ShapeDtypeStruct((b, n, MLA_W), F32),
        compiler_params=_cparams(("arbitrary", "arbitrary")),
    )(*args)


def _tri_masks():
    r = _iota2((CHUNK, CHUNK), 0)
    c = _iota2((CHUNK, CHUNK), 1)
    return r, c


def _out_norm_gate(o_f_ref, o_b_ref, z_ref, gn_ref, o_ref, n):
    bd = _head_block_ones(LANES) * (1.0 / HEAD_W)
    rows = min(n, 512)

    def blk(i, carry):
        r0 = pl.multiple_of(i * rows, rows)
        o = o_f_ref[pl.ds(r0, rows), :] + o_b_ref[pl.ds(r0, rows), :]
        ms = _dot_sel(o * o, bd)
        z = z_ref[0, pl.ds(r0, rows), :]
        o_ref[0, pl.ds(r0, rows), :] = o * lax.rsqrt(ms + EPS) * gn_ref[...] * _silu(z)
        return carry

    lax.fori_loop(0, n // rows, blk, 0)


def _gla_kernel(q_ref, k_ref, v_ref, z_ref, sm_ref, wg_ref, bg_ref, gn_ref, s0_ref,
                o_ref, st_ref, of_scr, ob_scr, st_scr, *, n):
    nc = n // CHUNK
    r, c = _tri_masks()
    tri = ((r >= c).astype(F32), (r <= c).astype(F32))
    keep = (r >= c, r <= c)
    for d in range(2):
        for h in range(2):
            st_scr[d, h] = s0_ref[0, d, h]

    def body(ci, carry):
        for d in range(2):
            cc = ci if d == 0 else nc - 1 - ci
            r0 = pl.multiple_of(cc * CHUNK, CHUNK)
            rows = pl.ds(r0, CHUNK)
            ds_ = slice(d * LANES, (d + 1) * LANES)
            logit = _dot(sm_ref[0, rows, :], wg_ref[0, :, ds_]) + bg_ref[0, :, ds_]
            la = _log_sigmoid(logit) * (1.0 / GLA_TAU)
            g = _sel_dot(tri[d], la)
            g_last = g[CHUNK - 1:CHUNK] if d == 0 else g[0:1]
            qd = q_ref[0, rows, :] * (GLA_DK ** -0.5) * jnp.exp(g)
            kc = k_ref[0, rows, :]
            k_inv = kc * jnp.exp(-g)
            k_end = kc * jnp.exp(g_last - g)
            decay = jnp.exp(g_last)
            vc = v_ref[0, rows, :]
            outs = []
            for h in range(2):
                hs = slice(h * HEAD_W, (h + 1) * HEAD_W)
                a = jnp.where(keep[d], _dot_nt(qd[:, hs], k_inv[:, hs]), 0.0)
                st = st_scr[d, h]
                outs.append(_dot(a, vc[:, hs]) + _dot_nt(qd[:, hs], st))
                st_scr[d, h] = st * decay[:, hs] + _dot_tn(vc[:, hs], k_end[:, hs])
            o = jnp.concatenate(outs, axis=-1)
            if d == 0:
                of_scr[rows, :] = o
            else:
                ob_scr[rows, :] = o
        return carry

    lax.fori_loop(0, nc, body, 0)
    for d in range(2):
        for h in range(2):
            st_ref[0, d, h] = st_scr[d, h]
    _out_norm_gate(of_scr, ob_scr, z_ref, gn_ref, o_ref, n)


def _gla(proj3, wg_p, bg_p, gn, s0t):
    b, n, _ = proj3.shape
    npair = GLA_HEADS // 2
    col = lambda base: (lambda bi, p: (bi, 0, base + p))
    s0r = _pair_major(s0t).reshape(b * npair, 2, 2, GLA_DV, GLA_DK)
    st_spec = pl.BlockSpec((1, 2, 2, GLA_DV, GLA_DK), lambda bi, p: (bi * npair + p, 0, 0, 0, 0))
    o, st = pl.pallas_call(
        functools.partial(_gla_kernel, n=n),
        grid=(b, npair),
        in_specs=[pl.BlockSpec((1, n, LANES), col(COL_GLA_Q)),
                  pl.BlockSpec((1, n, LANES), col(COL_GLA_K)),
                  pl.BlockSpec((1, n, LANES), col(COL_GLA_V)),
                  pl.BlockSpec((1, n, LANES), col(COL_GLA_Z)),
                  pl.BlockSpec((1, n, LANES), lambda bi, p: (bi, 0, COL_SMALL)),
                  pl.BlockSpec((1, LANES, 2 * LANES), lambda bi, p: (p, 0, 0)),
                  pl.BlockSpec((1, 1, 2 * LANES), lambda bi, p: (p, 0, 0)),
                  pl.BlockSpec((1, LANES), lambda bi, p: (0, 0)),
                  st_spec],
        out_specs=[pl.BlockSpec((1, n, LANES), lambda bi, p: (bi, 0, p)), st_spec],
        out_shape=[jax.ShapeDtypeStruct((b, n, GLA_W), F32),
                   jax.ShapeDtypeStruct(s0r.shape, F32)],
        scratch_shapes=[pltpu.VMEM((n, LANES), F32), pltpu.VMEM((n, LANES), F32),
                        pltpu.VMEM((2, 2, GLA_DV, GLA_DK), F32)],
        compiler_params=_cparams(("arbitrary", "arbitrary")),
    )(proj3, proj3, proj3, proj3, proj3, wg_p, bg_p, gn, s0r)
    return o, st


def _gdn_kernel(q_ref, k_ref, v_ref, z_ref, sm_ref, cq_ref, ck_ref, cv_ref, alog_ref, dtb_ref,
                eb_ref, eg_ref, selr_ref, gn_ref, s0_ref,
                o_ref, st_ref, of_scr, ob_scr, st_scr, *, n):
    nc = n // CHUNK
    r, c = _tri_masks()
    tri = ((r >= c).astype(F32), (r <= c).astype(F32))
    strict = (r > c, r < c)
    diag = r == c
    row_id = _iota2((CHUNK, LANES), 0)
    lane_id = _iota2((CHUNK, LANES), 1)
    head_ones = _head_block_ones(LANES)
    for d in range(2):
        for h in range(2):
            st_scr[d, h] = s0_ref[0, d, h]

    def conv_silu(x_ref, w_ref, r0, first, last):
        cur = x_ref[0, pl.ds(r0, CHUNK), :]
        prev_row = x_ref[0, pl.ds(jnp.maximum(r0 - 1, 0), 1), :] * jnp.where(first, 0.0, 1.0)
        next_row = x_ref[0, pl.ds(jnp.minimum(r0 + CHUNK, n - 1), 1), :] * jnp.where(last, 0.0, 1.0)
        prev = jnp.where(row_id == 0, prev_row, pltpu.roll(cur, 1, 0))
        nxt = jnp.where(row_id == CHUNK - 1, next_row, pltpu.roll(cur, CHUNK - 1, 0))
        return _silu(prev * w_ref[0:1, :] + cur * w_ref[1:2, :] + nxt * w_ref[2:3, :])

    def body(ci, carry):
        for d in range(2):
            cc = ci if d == 0 else nc - 1 - ci
            r0 = pl.multiple_of(cc * CHUNK, CHUNK)
            rows = pl.ds(r0, CHUNK)
            first = cc == 0
            last = cc == nc - 1
            qc = conv_silu(q_ref, cq_ref, r0, first, last)
            kc = conv_silu(k_ref, ck_ref, r0, first, last)
            vc = conv_silu(v_ref, cv_ref, r0, first, last)
            qn = qc * lax.rsqrt(_dot_sel(qc * qc, head_ones) + EPS) * (GDN_DK ** -0.5)
            kn = kc * lax.rsqrt(_dot_sel(kc * kc, head_ones) + EPS)

            sm = sm_ref[0, rows, :]
            beta = jax.nn.sigmoid(sm)
            la = -jnp.exp(alog_ref[...]) * _softplus(sm + dtb_ref[...])
            la = jnp.where(lane_id >= SM_A, la, 0.0)
            g = _sel_dot(tri[d], la)
            ds_ = slice(d * LANES, (d + 1) * LANES)
            beta_x = _dot_sel(beta, eb_ref[0, :, ds_])
            g_x = _dot_sel(g, eg_ref[0, :, ds_])
            g_rows = _sel_dot_nt(selr_ref[0], g)
            g_last = g_x[CHUNK - 1:CHUNK] if d == 0 else g_x[0:1]
            eg = jnp.exp(g_x)
            q_dec = qn * eg
            k_end = kn * jnp.exp(g_last - g_x)
            decay = jnp.exp(g_last)
            vb = vc * beta_x
            kbg = kn * beta_x * eg
            outs = []
            for h in range(2):
                hs = slice(h * HEAD_W, (h + 1) * HEAD_W)
                diff = g_x[:, hs] - g_rows[2 * d + h:2 * d + h + 1, :]
                e_strict = jnp.where(strict[d], jnp.exp(jnp.where(strict[d], diff, 0.0)), 0.0)
                e_incl = jnp.where(diag, 1.0, e_strict)
                kk = _dot_nt(kn[:, hs], kn[:, hs])
                m = beta_x[:, hs] * kk * e_strict
                p = -m
                nn = p
                for _ in range(5):
                    p = _dot_x3(p, p)
                    nn = nn + p + _dot_x3(nn, p)
                rhs = jnp.concatenate([vb[:, hs], kbg[:, hs]], axis=-1)
                uw = rhs + _dot(nn, rhs)
                u = uw[:, :HEAD_W]
                w = uw[:, HEAD_W:]
                qk = _dot_nt(qn[:, hs], kn[:, hs]) * e_incl
                st = st_scr[d, h]
                v_new = u - _dot_nt(w, st)
                outs.append(_dot_nt(q_dec[:, hs], st) + _dot(qk, v_new))
                st_scr[d, h] = st * decay[:, hs] + _dot_tn(v_new, k_end[:, hs])
            o = jnp.concatenate(outs, axis=-1)
            if d == 0:
                of_scr[rows, :] = o
            else:
                ob_scr[rows, :] = o
        return carry

    lax.fori_loop(0, nc, body, 0)
    for d in range(2):
        for h in range(2):
            st_ref[0, d, h] = st_scr[d, h]
    _out_norm_gate(of_scr, ob_scr, z_ref, gn_ref, o_ref, n)


def _gdn_select_constants():
    npair = GDN_HEADS // 2
    eb = np.zeros((npair, LANES, 2 * LANES), np.float32)
    eg = np.zeros((npair, LANES, 2 * LANES), np.float32)
    selr = np.zeros((npair, 8, LANES), np.float32)
    for p in range(npair):
        for d in range(2):
            for h in range(2):
                hh = 2 * p + h
                lanes = slice(d * LANES + h * HEAD_W, d * LANES + (h + 1) * HEAD_W)
                eb[p, SM_BETA + d * GDN_HEADS + hh, lanes] = 1.0
                eg[p, SM_A + d * GDN_HEADS + hh, lanes] = 1.0
                selr[p, 2 * d + h, SM_A + d * GDN_HEADS + hh] = 1.0
    return jnp.asarray(eb), jnp.asarray(eg), jnp.asarray(selr)


def _gdn(proj3, conv_w, alog_row, dtb_row, gn, s0t):
    b, n, _ = proj3.shape
    npair = GDN_HEADS // 2
    eb, eg, selr = _gdn_select_constants()
    col = lambda base: (lambda bi, p: (bi, 0, base + p))
    cw = GDN_HEADS * GDN_DK
    s0r = _pair_major(s0t).reshape(b * npair, 2, 2, GDN_DV, GDN_DK)
    st_spec = pl.BlockSpec((1, 2, 2, GDN_DV, GDN_DK), lambda bi, p: (bi * npair + p, 0, 0, 0, 0))
    pair_spec = lambda shp: pl.BlockSpec((1,) + shp, lambda bi, p: (p, 0, 0))
    o, st = pl.pallas_call(
        functools.partial(_gdn_kernel, n=n),
        grid=(b, npair),
        in_specs=[pl.BlockSpec((1, n, LANES), col(COL_GDN_Q)),
                  pl.BlockSpec((1, n, LANES), col(COL_GDN_K)),
                  pl.BlockSpec((1, n, LANES), col(COL_GDN_V)),
                  pl.BlockSpec((1, n, LANES), col(COL_GDN_Z)),
                  pl.BlockSpec((1, n, LANES), lambda bi, p: (bi, 0, COL_SMALL)),
                  pl.BlockSpec((3, LANES), lambda bi, p: (0, p)),
                  pl.BlockSpec((3, LANES), lambda bi, p: (0, cw // LANES + p)),
                  pl.BlockSpec((3, LANES), lambda bi, p: (0, 2 * cw // LANES + p)),
                  pl.BlockSpec((1, LANES), lambda bi, p: (0, 0)),
                  pl.BlockSpec((1, LANES), lambda bi, p: (0, 0)),
                  pair_spec((LANES, 2 * LANES)),
                  pair_spec((LANES, 2 * LANES)),
                  pair_spec((8, LANES)),
                  pl.BlockSpec((1, LANES), lambda bi, p: (0, 0)),
                  st_spec],
        out_specs=[pl.BlockSpec((1, n, LANES), lambda bi, p: (bi, 0, p)), st_spec],
        out_shape=[jax.ShapeDtypeStruct((b, n, GDN_W), F32),
                   jax.ShapeDtypeStruct(s0r.shape, F32)],
        scratch_shapes=[pltpu.VMEM((n, LANES), F32), pltpu.VMEM((n, LANES), F32),
                        pltpu.VMEM((2, 2, GDN_DV, GDN_DK), F32)],
        compiler_params=_cparams(("arbitrary", "arbitrary")),
    )(proj3, proj3, proj3, proj3, proj3, conv_w, conv_w, conv_w, alog_row, dtb_row, eb, eg, selr, gn, s0r)
    return o, st


def _outproj_kernel(omla_ref, ogla_ref, ogdn_ref, x_ref, mod_ref, nrm_ref, wout_ref, wr_ref,
                    x1_ref, xm_ref, aff_ref, afft_ref):
    y = (_dot(omla_ref[...], wout_ref[0:MLA_W, :])
         + _dot(ogla_ref[...], wout_ref[MLA_W:MLA_W + GLA_W, :])
         + _dot(ogdn_ref[...], wout_ref[MLA_W + GLA_W:, :]))
    m = mod_ref[0]
    x1 = x_ref[...] + m[2:3] * _rms(y, nrm_ref[1:2, :])
    x1_ref[...] = x1
    xm2 = _rms(x1, nrm_ref[2:3, :]) * (1.0 + m[4:5]) + m[3:4]
    xm_ref[...] = xm2
    logits = jnp.dot(xm2, wr_ref[...], preferred_element_type=F32, precision=lax.Precision.HIGHEST)
    lane = _iota2(logits.shape, 1)
    logits = jnp.where(lane < N_EXPERTS, logits, -jnp.inf)
    e = jnp.exp(logits - jnp.max(logits, axis=-1, keepdims=True))
    aff = e / jnp.sum(e, axis=-1, keepdims=True)
    aff_ref[...] = aff
    sel = (_iota2((N_EXPERTS, LANES), 0) == _iota2((N_EXPERTS, LANES), 1)).astype(F32)
    afft_ref[...] = _sel_dot_nt(sel, aff)


def _out_projection(o_mla, o_gla, o_gdn, x2d, modg, nrm, w_out, w_router_p, tiles_per_batch, tm):
    t, d = x2d.shape
    per_batch = modg.shape[0] > 1
    return pl.pallas_call(
        _outproj_kernel,
        grid=(t // tm,),
        in_specs=[pl.BlockSpec((tm, MLA_W), lambda i: (i, 0)),
                  pl.BlockSpec((tm, GLA_W), lambda i: (i, 0)),
                  pl.BlockSpec((tm, GDN_W), lambda i: (i, 0)),
                  pl.BlockSpec((tm, d), lambda i: (i, 0)),
                  pl.BlockSpec((1, 6, d), (lambda i: (i // tiles_per_batch, 0, 0)) if per_batch
                               else (lambda i: (0, 0, 0))),
                  pl.BlockSpec((4, d), lambda i: (0, 0)),
                  pl.BlockSpec((d, d), lambda i: (0, 0)),
                  pl.BlockSpec((d, LANES), lambda i: (0, 0))],
        out_specs=[pl.BlockSpec((tm, d), lambda i: (i, 0)),
                   pl.BlockSpec((tm, d), lambda i: (i, 0)),
                   pl.BlockSpec((tm, LANES), lambda i: (i, 0)),
                   pl.BlockSpec((N_EXPERTS, tm), lambda i: (0, i))],
        out_shape=[jax.ShapeDtypeStruct((t, d), F32),
                   jax.ShapeDtypeStruct((t, d), F32),
                   jax.ShapeDtypeStruct((t, LANES), F32),
                   jax.ShapeDtypeStruct((N_EXPERTS, t), F32)],
        compiler_params=_cparams(("arbitrary",)),
    )(o_mla, o_gla, o_gdn, x2d, modg, nrm, w_out, w_router_p)


def _route_kernel(afft_ref, rank_ref, pos_ref, *, cap):
    a = afft_ref[...]
    t = a.shape[1]

    def search(i, cur):
        cand = cur | jnp.left_shift(jnp.int32(1), 30 - i)
        cnt = jnp.sum((a >= lax.bitcast_convert_type(cand, F32)).astype(F32), axis=1, keepdims=True)
        return jnp.where(cnt >= cap, cand, cur)

    kth = lax.bitcast_convert_type(lax.fori_loop(0, 31, search, jnp.zeros((a.shape[0], 1), I32)), F32)
    gt = (a > kth).astype(F32)
    eq = (a == kth).astype(F32)
    need = cap - jnp.sum(gt, axis=1, keepdims=True)
    before = (_iota2((LANES, LANES), 0) < _iota2((LANES, LANES), 1)).astype(BF16)
    nb = t // LANES
    carry_eq = jnp.zeros((a.shape[0], 1), F32)
    carry = jnp.zeros((a.shape[0], 1), F32)
    for b in range(nb):
        sl = slice(b * LANES, (b + 1) * LANES)
        eqb = eq[:, sl]
        pre = jnp.dot(eqb.astype(BF16), before, preferred_element_type=F32) + carry_eq
        selb = jnp.maximum(gt[:, sl], jnp.where(pre < need, eqb, 0.0))
        carry_eq = carry_eq + jnp.sum(eqb, axis=1, keepdims=True)
        local = jnp.dot(selb.astype(BF16), before, preferred_element_type=F32)
        rank_ref[:, sl] = jnp.where(selb > 0.0, local, -1.0)
        pos_ref[:, sl] = local + carry
        carry = carry + jnp.sum(selb, axis=1, keepdims=True)


def _route(afft, cap):
    e, t = afft.shape
    return pl.pallas_call(
        functools.partial(_route_kernel, cap=cap),
        out_shape=[jax.ShapeDtypeStruct((e, t), F32), jax.ShapeDtypeStruct((e, t), F32)],
        compiler_params=_cparams(),
    )(afft)


def _compact_kernel(off_ref, rank_ref, aff_ref, idx_ref, gate_ref, *, nb):
    e = pl.program_id(0)
    idx_ref[...] = jnp.zeros(idx_ref.shape, I32)
    gate_ref[...] = jnp.zeros(gate_ref.shape, F32)
    slot = _iota2((LANES, LANES), 0)
    tok = _iota2((LANES, LANES), 0).astype(BF16)

    def blk(b, carry):
        rk = rank_ref[0, pl.ds(b, 1), :]
        onehot = (rk.astype(I32) == slot).astype(BF16)
        ids = jnp.dot(onehot, tok, preferred_element_type=F32).astype(I32) + b * LANES
        rows = pl.ds(off_ref[e, b], LANES)
        idx_ref[0, rows, :] = ids
        gate_ref[0, rows, :] = _sel_dot(onehot, aff_ref[pl.ds(pl.multiple_of(b * LANES, LANES), LANES), :])
        return carry

    lax.fori_loop(0, nb, blk, 0)


def _compact(off, rank, aff, cap):
    e, t = rank.shape
    nb = t // LANES
    rows = cap + LANES
    idx, gate = pl.pallas_call(
        functools.partial(_compact_kernel, nb=nb),
        grid_spec=pltpu.PrefetchScalarGridSpec(
            num_scalar_prefetch=1,
            grid=(e,),
            in_specs=[pl.BlockSpec((1, nb, LANES), lambda ei, off_r: (ei, 0, 0)),
                      pl.BlockSpec((t, LANES), lambda ei, off_r: (0, 0))],
            out_specs=[pl.BlockSpec((1, rows, LANES), lambda ei, off_r: (ei, 0, 0)),
                       pl.BlockSpec((1, rows, LANES), lambda ei, off_r: (ei, 0, 0))]),
        out_shape=[jax.ShapeDtypeStruct((e, rows, LANES), I32),
                   jax.ShapeDtypeStruct((e, rows, LANES), F32)],
        compiler_params=_cparams(("arbitrary",)),
    )(off, rank.reshape(e, nb, LANES), aff)
    return idx[:, :cap, 0], gate


def _gather_copy(x_hbm, xe_scr, sem, slot, src_row, dst_row, nrows):
    return pltpu.make_async_copy(x_hbm.at[pl.ds(src_row, nrows)], xe_scr.at[slot, pl.ds(dst_row, nrows)],
                                 sem.at[slot])


def _expert_kernel(idx_ref, x_hbm, gate_ref, w1_ref, w3_ref, w2_ref, ye_ref, xe_scr, sem, *, cap):
    e = pl.program_id(0)
    ne = pl.num_programs(0)

    def start_gather(ex, slot):
        def row(s, carry):
            _gather_copy(x_hbm, xe_scr, sem, slot, idx_ref[ex, s], s, 1).start()
            return carry
        lax.fori_loop(0, cap, row, 0)

    @pl.when(e == 0)
    def _():
        start_gather(0, 0)

    @pl.when(e + 1 < ne)
    def _():
        start_gather(e + 1, (e + 1) % 2)

    slot = e % 2
    _gather_copy(x_hbm, xe_scr, sem, slot, 0, 0, cap).wait()
    xe = xe_scr[slot].astype(BF16)
    ge = gate_ref[0]
    gate = jnp.sum(jnp.where(_iota2(ge.shape, 1) == e, ge, 0.0), axis=-1, keepdims=True)
    ff = w1_ref.shape[2]
    fchunk = 512
    y = jnp.zeros((cap, w2_ref.shape[2]), F32)
    for j in range(ff // fchunk):
        fs = slice(j * fchunk, (j + 1) * fchunk)
        h = _silu(_dot(xe, w1_ref[0, :, fs])) * _dot(xe, w3_ref[0, :, fs])
        y = y + _dot(h, w2_ref[0, fs, :])
    ye_ref[0] = y * gate


def _expert_ffn(idx, xm, gate, w1, w3, w2, cap):
    t, d = xm.shape
    e, _, ff = w1.shape
    wspec = lambda shp: pl.BlockSpec((1,) + shp, lambda ei, idx_r: (ei, 0, 0))
    return pl.pallas_call(
        functools.partial(_expert_kernel, cap=cap),
        grid_spec=pltpu.PrefetchScalarGridSpec(
            num_scalar_prefetch=1,
            grid=(e,),
            in_specs=[pl.BlockSpec(memory_space=pl.ANY), wspec((cap, LANES)),
                      wspec((d, ff)), wspec((d, ff)), wspec((ff, d))],
            out_specs=pl.BlockSpec((1, cap, d), lambda ei, idx_r: (ei, 0, 0)),
            scratch_shapes=[pltpu.VMEM((2, cap, d), F32), pltpu.SemaphoreType.DMA((2,))]),
        out_shape=jax.ShapeDtypeStruct((e, cap, d), F32),
        compiler_params=_cparams(("arbitrary",)),
    )(idx, xm, gate, w1, w3, w2)


def _scatter_kernel(idx_ref, lo_ref, ye_ref, o_ref, *, cap, nparts, part_rows):
    part = pl.program_id(0)
    e = pl.program_id(1)

    @pl.when(e == 0)
    def _():
        o_ref[...] = jnp.zeros(o_ref.shape, F32)

    base = part * part_rows

    def add(s, carry):
        t = idx_ref[e, s] - base
        o_ref[pl.ds(t, 1), :] = o_ref[pl.ds(t, 1), :] + ye_ref[0, pl.ds(s, 1), :]
        return carry

    lax.fori_loop(lo_ref[e, part], lo_ref[e, part + 1], add, 0)


def _scatter(idx, bounds, ye, t, nparts):
    e, cap, d = ye.shape
    part_rows = t // nparts
    return pl.pallas_call(
        functools.partial(_scatter_kernel, cap=cap, nparts=nparts, part_rows=part_rows),
        grid_spec=pltpu.PrefetchScalarGridSpec(
            num_scalar_prefetch=2,
            grid=(nparts, e),
            in_specs=[pl.BlockSpec((1, cap, d), lambda pi, ei, a, b: (ei, 0, 0))],
            out_specs=pl.BlockSpec((part_rows, d), lambda pi, ei, a, b: (pi, 0))),
        out_shape=jax.ShapeDtypeStruct((t, d), F32),
        compiler_params=_cparams(("arbitrary", "arbitrary")),
    )(idx, bounds, ye)


def _residual_kernel(x1_ref, moe_ref, mod_ref, nrm_ref, o_ref):
    m = mod_ref[0]
    o_ref[...] = x1_ref[...] + m[5:6] * _rms(moe_ref[...], nrm_ref[3:4, :])


def _ffn_residual(x1, moe, modg, nrm, tiles_per_batch, tm):
    t, d = x1.shape
    per_batch = modg.shape[0] > 1
    return pl.pallas_call(
        _residual_kernel,
        grid=(t // tm,),
        in_specs=[pl.BlockSpec((tm, d), lambda i: (i, 0)),
                  pl.BlockSpec((tm, d), lambda i: (i, 0)),
                  pl.BlockSpec((1, 6, d), (lambda i: (i // tiles_per_batch, 0, 0)) if per_batch
                               else (lambda i: (0, 0, 0))),
                  pl.BlockSpec((4, d), lambda i: (0, 0))],
        out_specs=pl.BlockSpec((tm, d), lambda i: (i, 0)),
        out_shape=jax.ShapeDtypeStruct((t, d), F32),
        compiler_params=_cparams(("arbitrary",)),
    )(x1, moe, modg, nrm)


def _rot_cols(w):
    s = w.shape[:-1]
    w2 = w.reshape(s + (ROPE_DIM // 2, 2))
    return jnp.stack([-w2[..., 1], w2[..., 0]], axis=-1).reshape(s + (ROPE_DIM,))


def _layer_weights(l, w_in, mla_q_norm, mla_wq_b, mla_kv_norm, mla_wkv_b, gla_wg, gla_bg, gla_out_norm,
                   gdn_conv, gdn_a_log, gdn_dt_bias, gdn_out_norm, w_out, sandwich_norms, w_router,
                   w_e1, w_e3, w_e2):
    wi = w_in[l]
    o = 0
    parts = {}
    for name, sz in (("q_lat", Q_LORA), ("kv_lat", KV_LORA), ("k_rope", ROPE_DIM),
                     ("gla_q", GLA_W), ("gla_k", GLA_W), ("gla_v", GLA_W), ("gla_glr", 2 * GLA_GATE_RANK),
                     ("gla_z", GLA_W), ("gdn_q", GDN_W), ("gdn_k", GDN_W), ("gdn_v", GDN_W), ("gdn_z", GDN_W),
                     ("gdn_b", 2 * GDN_HEADS), ("gdn_a", 2 * GDN_HEADS)):
        parts[name] = wi[:, o:o + sz]
        o += sz
    w_in_p = jnp.concatenate(
        [parts[k] for k in ("q_lat", "kv_lat", "gla_q", "gla_k", "gla_v", "gla_z",
                            "gdn_q", "gdn_k", "gdn_v", "gdn_z")]
        + [parts["k_rope"], _rot_cols(parts["k_rope"]), parts["gla_glr"], parts["gdn_b"], parts["gdn_a"]],
        axis=1).astype(BF16)

    wq = mla_wq_b[l].reshape(Q_LORA, MLA_HEADS, NOPE_DIM + ROPE_DIM)
    wq_ext = jnp.concatenate([wq, _rot_cols(wq[..., NOPE_DIM:])], axis=-1).reshape(Q_LORA, -1).astype(BF16)
    wkv = mla_wkv_b[l].reshape(KV_LORA, MLA_HEADS, NOPE_DIM + MLA_V_DIM)
    wkv_ext = jnp.concatenate([wkv[..., :NOPE_DIM].reshape(KV_LORA, -1),
                               wkv[..., NOPE_DIM:].reshape(KV_LORA, -1)], axis=1).astype(BF16)

    npair = GLA_HEADS // 2
    wg_p = jnp.zeros((npair, LANES, 2 * LANES), F32)
    bg_p = jnp.zeros((npair, 1, 2 * LANES), F32)
    for p in range(npair):
        for d in range(2):
            rows = slice(SM_GLR + d * GLA_GATE_RANK, SM_GLR + (d + 1) * GLA_GATE_RANK)
            wg_p = wg_p.at[p, rows, d * LANES:(d + 1) * LANES].set(gla_wg[l, d][:, p * LANES:(p + 1) * LANES])
            bg_p = bg_p.at[p, 0, d * LANES:(d + 1) * LANES].set(gla_bg[l, d][p * LANES:(p + 1) * LANES])

    alog_row = jnp.zeros((1, LANES), F32).at[0, SM_A:].set(gdn_a_log[l].reshape(-1))
    dtb_row = jnp.zeros((1, LANES), F32).at[0, SM_A:].set(gdn_dt_bias[l].reshape(-1))
    w_router_p = jnp.zeros((D_MODEL, LANES), F32).at[:, :N_EXPERTS].set(w_router[l])
    return dict(
        w_in=w_in_p, gq=mla_q_norm[l][None], wq=wq_ext, gkv=mla_kv_norm[l][None], wkv=wkv_ext,
        wg=wg_p.astype(BF16), bg=bg_p, gla_gn=jnp.tile(gla_out_norm[l], 2)[None],
        conv=gdn_conv[l], alog=alog_row, dtb=dtb_row, gdn_gn=jnp.tile(gdn_out_norm[l], 2)[None],
        w_out=w_out[l].astype(BF16), nrm=sandwich_norms[l], w_router=w_router_p,
        w1=w_e1[l].astype(BF16), w3=w_e3[l].astype(BF16), w2=w_e2[l].astype(BF16))


def _rope_tables(n, rotate):
    ones = np.ones((1, NOPE_DIM), np.float32)
    if not rotate:
        t1 = np.concatenate([ones, np.ones((1, ROPE_DIM), np.float32), np.zeros((1, ROPE_DIM), np.float32)], 1)
        z = np.zeros_like(t1)
        return tuple(jnp.asarray(np.broadcast_to(t, (n, LANES)).copy()) for t in (t1, z, z))
    rows = n // GRID_W
    row = jnp.repeat(jnp.arange(rows), GRID_W).astype(F32)
    col = jnp.tile(jnp.arange(GRID_W), rows).astype(F32)
    half = ROPE_DIM // 2
    freqs = ROPE_BASE ** (-jnp.arange(0, half, 2, dtype=F32) / half)
    ang = jnp.concatenate([row[:, None] * freqs, col[:, None] * freqs], axis=-1)
    cos = jnp.repeat(jnp.cos(ang), 2, axis=-1)
    sin = jnp.repeat(jnp.sin(ang), 2, axis=-1)
    z32 = jnp.zeros((n, ROPE_DIM), F32)
    t1 = jnp.concatenate([jnp.ones((n, NOPE_DIM), F32), cos, z32], axis=1)
    t2 = jnp.concatenate([jnp.zeros((n, NOPE_DIM), F32), sin, z32], axis=1)
    t3 = jnp.concatenate([jnp.zeros((n, NOPE_DIM), F32), z32, jnp.ones((n, ROPE_DIM), F32)], axis=1)
    return t1, t2, t3


def _trunk_layer(x, lw, modg, ctx_cache, tabs):
    b, n, d = x.shape
    t = b * n
    tm = min(512, n)
    tiles_per_batch = n // tm
    x2d = x.reshape(t, d)

    proj = _in_projection(x2d, modg, lw["nrm"][0:1], lw["w_in"], tiles_per_batch, tm)
    proj3 = proj.reshape(b, n, PROJ_W)

    q, k, v, ckv, k_rope = _mla_prep(proj, tabs, lw["gq"], lw["wq"], lw["gkv"], lw["wkv"], tiles_per_batch, tm)
    qk_w = MLA_HEADS * MLA_QK_PAD
    if ctx_cache is None:
        kc = vc = None
        s0_gla = jnp.zeros((b, 2, GLA_HEADS, GLA_DV, GLA_DK), F32)
        s0_gdn = jnp.zeros((b, 2, GDN_HEADS, GDN_DV, GDN_DK), F32)
    else:
        ckv_c, kr_c, s0_gla, s0_gdn = ctx_cache
        lc = ckv_c.shape[1]
        kc, vc = _ctx_keys(ckv_c.reshape(b * lc, KV_LORA), kr_c.reshape(b * lc, ROPE_DIM), lw["wkv"])
        kc = kc.reshape(b, lc, qk_w)
        vc = vc.reshape(b, lc, MLA_W)
        s0_gla = jnp.swapaxes(s0_gla, -1, -2)
        s0_gdn = jnp.swapaxes(s0_gdn, -1, -2)
    o_mla = _attention(q.reshape(b, n, qk_w), k.reshape(b, n, qk_w), v.reshape(b, n, MLA_W), kc, vc,
                       tq=min(256, n))

    o_gla, st_gla = _gla(proj3, lw["wg"], lw["bg"], lw["gla_gn"], s0_gla)
    o_gdn, st_gdn = _gdn(proj3, lw["conv"], lw["alog"], lw["dtb"], lw["gdn_gn"], s0_gdn)

    x1, xm2, aff, afft = _out_projection(o_mla.reshape(t, MLA_W), o_gla.reshape(t, GLA_W),
                                        o_gdn.reshape(t, GDN_W), x2d, modg, lw["nrm"], lw["w_out"],
                                        lw["w_router"], tiles_per_batch, tm)

    cap = EC_CAPACITY * t // N_EXPERTS
    rank, pos = _route(afft, cap)
    off = pos[:, ::LANES].astype(I32)
    idx, gate = _compact(off, rank, aff, cap)
    ye = _expert_ffn(idx, xm2, gate, lw["w1"], lw["w3"], lw["w2"], cap)
    nparts = 2
    part_blocks = t // nparts // LANES
    bounds = jnp.concatenate([off[:, ::part_blocks], jnp.full((N_EXPERTS, 1), cap, I32)], axis=1)
    moe = _scatter(idx, bounds, ye, t, nparts)
    x2 = _ffn_residual(x1, moe, modg, lw["nrm"], tiles_per_batch, tm)

    new = None
    if ctx_cache is None:
        st_gla = jnp.swapaxes(st_gla.reshape(b, GLA_HEADS // 2, 2, 2, GLA_DV, GLA_DK), 1, 2)
        st_gdn = jnp.swapaxes(st_gdn.reshape(b, GDN_HEADS // 2, 2, 2, GDN_DV, GDN_DK), 1, 2)
        new = (ckv.reshape(b, n, KV_LORA), k_rope.reshape(b, n, ROPE_DIM),
               jnp.swapaxes(st_gla.reshape(b, 2, GLA_HEADS, GLA_DV, GLA_DK), -1, -2),
               jnp.swapaxes(st_gdn.reshape(b, 2, GDN_HEADS, GDN_DV, GDN_DK), -1, -2))
    return x2.reshape(b, n, d), new


def _pair_major(s0):
    b, two, h = s0.shape[:3]
    return jnp.swapaxes(s0.reshape(b, 2, h // 2, 2, s0.shape[-2], s0.shape[-1]), 1, 2)


def kernel(x_prompt, x_sample, cache_ckv, cache_krope, state_gla, state_gdn, c, c_ctx, w_in, mla_q_norm,
           mla_wq_b, mla_kv_norm, mla_wkv_b, gla_wg, gla_bg, gla_out_norm, gdn_conv, gdn_a_log, gdn_dt_bias,
           gdn_out_norm, w_out, w_mod, b_mod, sandwich_norms, w_router, w_e1, w_e3, w_e2):
    depth = w_in.shape[0]
    nb_s = x_sample.shape[0]
    cond = jnp.zeros((8, D_MODEL), F32).at[0].set(c_ctx).at[1:1 + nb_s].set(c)
    mod = _modulation(cond, w_mod, b_mod)

    tabs_ctx = _rope_tables(8, rotate=False)
    tabs_smp = _rope_tables(x_sample.shape[1], rotate=True)
    tabs_ctx = tuple(jnp.broadcast_to(t[:1], (min(512, x_prompt.shape[1]), LANES)) for t in tabs_ctx)

    lws = [_layer_weights(l, w_in, mla_q_norm, mla_wq_b, mla_kv_norm, mla_wkv_b, gla_wg, gla_bg, gla_out_norm,
                          gdn_conv, gdn_a_log, gdn_dt_bias, gdn_out_norm, w_out, sandwich_norms, w_router,
                          w_e1, w_e3, w_e2) for l in range(depth)]

    xp = x_prompt
    ckv_l, kr_l, sg_l, sd_l = [], [], [], []
    for l in range(depth):
        modg = mod[l, 0:1].reshape(1, 6, D_MODEL)
        xp, (ckv, kr, sg, sd) = _trunk_layer(xp, lws[l], modg, None, tabs_ctx)
        ckv_l.append(ckv)
        kr_l.append(kr)
        sg_l.append(sg)
        sd_l.append(sd)

    xs = x_sample
    for l in range(depth):
        modg = mod[l, 1:1 + nb_s].reshape(nb_s, 6, D_MODEL)
        xs, _ = _trunk_layer(xs, lws[l], modg,
                             (cache_ckv[:, l], cache_krope[:, l], state_gla[:, l], state_gdn[:, l]), tabs_smp)

    return (xp, xs, jnp.stack(ckv_l, axis=1), jnp.stack(kr_l, axis=1),
            jnp.stack(sg_l, axis=1), jnp.stack(sd_l, axis=1))
```

```python
import functools

import numpy as np
import jax
import jax.numpy as jnp
from jax import lax
from jax.experimental import pallas as pl
from jax.experimental.pallas import tpu as pltpu

F32 = jnp.float32
BF16 = jnp.bfloat16
I32 = jnp.int32

D_MODEL = 1024
DEPTH = 2
GRID_W = 64
EPS = 1e-6
CHUNK = 64

MLA_HEADS = 4
Q_LORA = 256
KV_LORA = 128
NOPE_DIM = 64
ROPE_DIM = 32
MLA_V_DIM = 64
ROPE_BASE = 10000.0
MLA_SCALE = (NOPE_DIM + ROPE_DIM) ** -0.5
MLA_QK_PAD = 128

GLA_HEADS = 4
GLA_DK = 64
GLA_DV = 64
GLA_GATE_RANK = 16
GLA_TAU = 16.0

GDN_HEADS = 8
GDN_DK = 64
GDN_DV = 64

N_EXPERTS = 16
EXPERT_FF = 1024
EC_CAPACITY = 2

LANES = 128
HEAD_W = 64
MLA_W = MLA_HEADS * MLA_V_DIM
GLA_W = GLA_HEADS * GLA_DV
GDN_W = GDN_HEADS * GDN_DV

COL_QLAT = 0
COL_KVLAT = 2
COL_GLA_Q = 3
COL_GLA_K = 5
COL_GLA_V = 7
COL_GLA_Z = 9
COL_GDN_Q = 11
COL_GDN_K = 15
COL_GDN_V = 19
COL_GDN_Z = 23
COL_SMALL = 27
PROJ_W = 28 * LANES
SM_KROPE = 0
SM_KROT = 32
SM_GLR = 64
SM_BETA = 96
SM_A = 112

VMEM_LIMIT = 56 * 1024 * 1024


def _cparams(sem=None):
    return pltpu.CompilerParams(dimension_semantics=sem, vmem_limit_bytes=VMEM_LIMIT)


def _dot(a, b):
    return jnp.dot(a.astype(BF16), b.astype(BF16), preferred_element_type=F32)


def _dot_nt(a, b):
    return lax.dot_general(a.astype(BF16), b.astype(BF16), (((1,), (1,)), ((), ())),
                           preferred_element_type=F32)


def _dot_tn(a, b):
    return lax.dot_general(a.astype(BF16), b.astype(BF16), (((0,), (0,)), ((), ())),
                           preferred_element_type=F32)


def _dot_x3(a, b):
    a1 = a.astype(BF16)
    a2 = (a - a1.astype(F32)).astype(BF16)
    b1 = b.astype(BF16)
    b2 = (b - b1.astype(F32)).astype(BF16)
    return (jnp.dot(a1, b1, preferred_element_type=F32) + jnp.dot(a1, b2, preferred_element_type=F32)
            + jnp.dot(a2, b1, preferred_element_type=F32))


def _split3(x):
    x1 = x.astype(BF16)
    r1 = x - x1.astype(F32)
    x2 = r1.astype(BF16)
    x3 = (r1 - x2.astype(F32)).astype(BF16)
    return x1, x2, x3


def _dot_sel(x, sel):
    x1, x2, x3 = _split3(x)
    s = sel.astype(BF16)
    return (jnp.dot(x1, s, preferred_element_type=F32) + jnp.dot(x2, s, preferred_element_type=F32)
            + jnp.dot(x3, s, preferred_element_type=F32))


def _sel_dot(sel, x):
    x1, x2, x3 = _split3(x)
    s = sel.astype(BF16)
    return (jnp.dot(s, x1, preferred_element_type=F32) + jnp.dot(s, x2, preferred_element_type=F32)
            + jnp.dot(s, x3, preferred_element_type=F32))


def _sel_dot_nt(sel, x):
    x1, x2, x3 = _split3(x)
    s = sel.astype(BF16)
    dn = (((1,), (1,)), ((), ()))
    return (lax.dot_general(s, x1, dn, preferred_element_type=F32)
            + lax.dot_general(s, x2, dn, preferred_element_type=F32)
            + lax.dot_general(s, x3, dn, preferred_element_type=F32))


def _silu(x):
    return x * jax.nn.sigmoid(x)


def _softplus(x):
    return jnp.maximum(x, 0.0) + jnp.log(1.0 + jnp.exp(-jnp.abs(x)))


def _log_sigmoid(x):
    return -_softplus(-x)


def _rms(x, g):
    return x * lax.rsqrt(jnp.mean(x * x, axis=-1, keepdims=True) + EPS) * g


def _iota2(shape, dim):
    return lax.broadcasted_iota(I32, shape, dim)


def _head_block_ones(width):
    r = _iota2((width, width), 0) // HEAD_W
    c = _iota2((width, width), 1) // HEAD_W
    return (r == c).astype(F32)


def _mod_kernel(cond_ref, w_ref, b_ref, o_ref):
    s = _silu(cond_ref[...])
    o_ref[0] = _dot(s, w_ref[0]) + b_ref[0]


def _modulation(cond, w_mod, b_mod):
    depth, d, d6 = w_mod.shape
    nj = d6 // d
    return pl.pallas_call(
        _mod_kernel,
        grid=(depth, nj),
        in_specs=[pl.BlockSpec((8, d), lambda l, j: (0, 0)),
                  pl.BlockSpec((1, d, d), lambda l, j: (l, 0, j)),
                  pl.BlockSpec((1, 1, d), lambda l, j: (l, 0, j))],
        out_specs=pl.BlockSpec((1, 8, d), lambda l, j: (l, 0, j)),
        out_shape=jax.ShapeDtypeStruct((depth, 8, d6), F32),
        compiler_params=_cparams(("arbitrary", "arbitrary")),
    )(cond, w_mod, b_mod.reshape(depth, 1, d6))


def _inproj_kernel(x_ref, mod_ref, nrm_ref, w_ref, o_ref, *, col_chunk):
    x = x_ref[...]
    m = mod_ref[0]
    xm = (_rms(x, nrm_ref[...]) * (1.0 + m[1:2]) + m[0:1]).astype(BF16)
    for j in range(o_ref.shape[1] // col_chunk):
        sl = slice(j * col_chunk, (j + 1) * col_chunk)
        o_ref[:, sl] = jnp.dot(xm, w_ref[:, sl], preferred_element_type=F32)


def _in_projection(x2d, modg, nrm0, w_in_p, tiles_per_batch, tm):
    t, d = x2d.shape
    pw = w_in_p.shape[1]
    per_batch = modg.shape[0] > 1
    return pl.pallas_call(
        functools.partial(_inproj_kernel, col_chunk=512),
        grid=(t // tm,),
        in_specs=[pl.BlockSpec((tm, d), lambda i: (i, 0)),
                  pl.BlockSpec((1, 6, d), (lambda i: (i // tiles_per_batch, 0, 0)) if per_batch
                               else (lambda i: (0, 0, 0))),
                  pl.BlockSpec((1, d), lambda i: (0, 0)),
                  pl.BlockSpec((d, pw), lambda i: (0, 0))],
        out_specs=pl.BlockSpec((tm, pw), lambda i: (i, 0)),
        out_shape=jax.ShapeDtypeStruct((t, pw), F32),
        compiler_params=_cparams(("arbitrary",)),
    )(x2d, modg, nrm0, w_in_p)


def _assemble_keys(k_nope, k_mid, k_last):
    parts = []
    for h in range(MLA_HEADS):
        parts += [k_nope[:, h * NOPE_DIM:(h + 1) * NOPE_DIM], k_mid, k_last]
    return jnp.concatenate(parts, axis=-1)


def _mla_prep_kernel(qlat_ref, kvlat_ref, sm_ref, t1_ref, t2_ref, t3_ref, gq_ref, wq_ref, gkv_ref, wkv_ref,
                     q_ref, k_ref, v_ref, ckv_ref, kr_ref):
    t1 = t1_ref[...]
    t2 = t2_ref[...]
    t3 = t3_ref[...]
    qn = _rms(qlat_ref[...], gq_ref[...])
    raw = _dot(qn, wq_ref[...])
    w = raw.shape[1]
    tile = lambda t: jnp.concatenate([t] * MLA_HEADS, axis=-1)
    q = raw * tile(t1) + pltpu.roll(raw, w - ROPE_DIM, 1) * tile(t2) + pltpu.roll(raw, ROPE_DIM, 1) * tile(t3)
    q_ref[...] = q.astype(q_ref.dtype)

    ckv = _rms(kvlat_ref[...], gkv_ref[...])
    ckv_ref[...] = ckv
    kv = _dot(ckv, wkv_ref[...])
    sm = sm_ref[...]
    k_rope = sm[:, SM_KROPE:SM_KROPE + ROPE_DIM]
    k_rot = sm[:, SM_KROT:SM_KROT + ROPE_DIM]
    kr_ref[...] = k_rope
    cs = slice(NOPE_DIM, NOPE_DIM + ROPE_DIM)
    kr = k_rope * t1[:, cs] + k_rot * t2[:, cs]
    k = _assemble_keys(kv[:, :MLA_HEADS * NOPE_DIM], kr, jnp.zeros_like(kr))
    k_ref[...] = k.astype(k_ref.dtype)
    v_ref[...] = kv[:, MLA_HEADS * NOPE_DIM:].astype(v_ref.dtype)


def _mla_prep(proj, tabs, gq, wq_ext, gkv, wkv_ext, tiles_per_seq, tm):
    t = proj.shape[0]
    t1, t2, t3 = tabs
    per_pos = t1.shape[0] > tm
    tab_map = (lambda i: (i % tiles_per_seq, 0)) if per_pos else (lambda i: (0, 0))
    qk_w = MLA_HEADS * MLA_QK_PAD
    return pl.pallas_call(
        _mla_prep_kernel,
        grid=(t // tm,),
        in_specs=[pl.BlockSpec((tm, Q_LORA), lambda i: (i, COL_QLAT * LANES // Q_LORA)),
                  pl.BlockSpec((tm, KV_LORA), lambda i: (i, COL_KVLAT)),
                  pl.BlockSpec((tm, LANES), lambda i: (i, COL_SMALL)),
                  pl.BlockSpec((tm, LANES), tab_map),
                  pl.BlockSpec((tm, LANES), tab_map),
                  pl.BlockSpec((tm, LANES), tab_map),
                  pl.BlockSpec((1, Q_LORA), lambda i: (0, 0)),
                  pl.BlockSpec((Q_LORA, qk_w), lambda i: (0, 0)),
                  pl.BlockSpec((1, KV_LORA), lambda i: (0, 0)),
                  pl.BlockSpec((KV_LORA, 2 * MLA_W), lambda i: (0, 0))],
        out_specs=[pl.BlockSpec((tm, qk_w), lambda i: (i, 0)),
                   pl.BlockSpec((tm, qk_w), lambda i: (i, 0)),
                   pl.BlockSpec((tm, MLA_W), lambda i: (i, 0)),
                   pl.BlockSpec((tm, KV_LORA), lambda i: (i, 0)),
                   pl.BlockSpec((tm, ROPE_DIM), lambda i: (i, 0))],
        out_shape=[jax.ShapeDtypeStruct((t, qk_w), BF16),
                   jax.ShapeDtypeStruct((t, qk_w), BF16),
                   jax.ShapeDtypeStruct((t, MLA_W), BF16),
                   jax.ShapeDtypeStruct((t, KV_LORA), F32),
                   jax.ShapeDtypeStruct((t, ROPE_DIM), F32)],
        compiler_params=_cparams(("arbitrary",)),
    )(proj, proj, proj, t1, t2, t3, gq, wq_ext, gkv, wkv_ext)


def _ctxkeys_kernel(ckv_ref, kr_ref, wkv_ref, k_ref, v_ref):
    kv = _dot(ckv_ref[...], wkv_ref[...])
    kr = kr_ref[...]
    k = _assemble_keys(kv[:, :MLA_HEADS * NOPE_DIM], jnp.zeros_like(kr), kr)
    k_ref[...] = k.astype(k_ref.dtype)
    v_ref[...] = kv[:, MLA_HEADS * NOPE_DIM:].astype(v_ref.dtype)


def _ctx_keys(ckv_c, kr_c, wkv_ext):
    t = ckv_c.shape[0]
    qk_w = MLA_HEADS * MLA_QK_PAD
    return pl.pallas_call(
        _ctxkeys_kernel,
        out_shape=[jax.ShapeDtypeStruct((t, qk_w), BF16), jax.ShapeDtypeStruct((t, MLA_W), BF16)],
        compiler_params=_cparams(),
    )(ckv_c, kr_c, wkv_ext)


def _attn_kernel(*refs, has_ctx):
    if has_ctx:
        q_ref, k_ref, v_ref, kc_ref, vc_ref, o_ref = refs
    else:
        q_ref, k_ref, v_ref, o_ref = refs
    outs = []
    for h in range(MLA_HEADS):
        qs = slice(h * MLA_QK_PAD, (h + 1) * MLA_QK_PAD)
        vs = slice(h * MLA_V_DIM, (h + 1) * MLA_V_DIM)
        q = q_ref[0, :, qs]
        s = _dot_nt(q, k_ref[0, :, qs]) * MLA_SCALE
        m = jnp.max(s, axis=-1, keepdims=True)
        if has_ctx:
            sc = _dot_nt(q, kc_ref[0, :, qs]) * MLA_SCALE
            m = jnp.maximum(m, jnp.max(sc, axis=-1, keepdims=True))
        p = jnp.exp(s - m)
        den = jnp.sum(p, axis=-1, keepdims=True)
        acc = _dot(p, v_ref[0, :, vs])
        if has_ctx:
            pc = jnp.exp(sc - m)
            den = den + jnp.sum(pc, axis=-1, keepdims=True)
            acc = acc + _dot(pc, vc_ref[0, :, vs])
        outs.append(acc / den)
    o_ref[0] = jnp.concatenate(outs, axis=-1)


def _attention(q, k, v, kc, vc, tq):
    b, n, qk_w = q.shape
    has_ctx = kc is not None
    in_specs = [pl.BlockSpec((1, tq, qk_w), lambda bi, i: (bi, i, 0)),
                pl.BlockSpec((1, n, qk_w), lambda bi, i: (bi, 0, 0)),
                pl.BlockSpec((1, n, MLA_W), lambda bi, i: (bi, 0, 0))]
    args = [q, k, v]
    if has_ctx:
        lc = kc.shape[1]
        in_specs += [pl.BlockSpec((1, lc, qk_w), lambda bi, i: (bi, 0, 0)),
                     pl.BlockSpec((1, lc, MLA_W), lambda bi, i: (bi, 0, 0))]
        args += [kc, vc]
    return pl.pallas_call(
        functools.partial(_attn_kernel, has_ctx=has_ctx),
        grid=(b, n // tq),
        in_specs=in_specs,
        out_specs=pl.BlockSpec((1, tq, MLA_W), lambda bi, i: (bi, i, 0)),
        out_shape=jax.ShapeDtypeStruct((b, n, MLA_W), F32),
        compiler_params=_cparams(("arbitrary", "arbitrary")),
    )(*args)


def _tri_masks():
    r = _iota2((CHUNK, CHUNK), 0)
    c = _iota2((CHUNK, CHUNK), 1)
    return r, c


def _out_norm_gate(o_f_ref, o_b_ref, z_ref, gn_ref, o_ref, n):
    bd = _head_block_ones(LANES) * (1.0 / HEAD_W)
    rows = min(n, 512)

    def blk(i, carry):
        r0 = pl.multiple_of(i * rows, rows)
        o = o_f_ref[pl.ds(r0, rows), :] + o_b_ref[pl.ds(r0, rows), :]
        ms = _dot_sel(o * o, bd)
        z = z_ref[0, pl.ds(r0, rows), :]
        o_ref[0, pl.ds(r0, rows), :] = o * lax.rsqrt(ms + EPS) * gn_ref[...] * _silu(z)
        return carry

    lax.fori_loop(0, n // rows, blk, 0)


def _gla_kernel(q_ref, k_ref, v_ref, z_ref, sm_ref, wg_ref, bg_ref, gn_ref, s0_ref,
                o_ref, st_ref, of_scr, ob_scr, st_scr, *, n):
    nc = n // CHUNK
    unroll = 2
    r, c = _tri_masks()
    tri = ((r >= c).astype(F32), (r <= c).astype(F32))
    lo_lanes = _iota2((CHUNK, LANES), 1) < HEAD_W
    r2 = _iota2((LANES, LANES), 0)
    c2 = _iota2((LANES, LANES), 1)
    same_head = (r2 // CHUNK) == (c2 // CHUNK)
    keep2 = (same_head & (r2 >= c2), same_head & (r2 <= c2))

    def stack2(x):
        return jnp.concatenate([jnp.where(lo_lanes, x, 0.0), jnp.where(lo_lanes, 0.0, x)], axis=0)

    def fold2(x):
        return x[:CHUNK] + x[CHUNK:]

    zeros = jnp.zeros((GLA_DV, GLA_DK), F32)
    for d in range(2):
        st_scr[d] = jnp.concatenate([jnp.concatenate([s0_ref[0, d, 0], zeros], axis=1),
                                     jnp.concatenate([zeros, s0_ref[0, d, 1]], axis=1)], axis=0)

    def body(it, carry):
        ch = []
        for j in range(unroll):
            for d in range(2):
                ci = it * unroll + j
                cc = ci if d == 0 else nc - 1 - ci
                rows = pl.ds(pl.multiple_of(cc * CHUNK, CHUNK), CHUNK)
                ch.append(dict(d=d, rows=rows, sm=sm_ref[0, rows, :], q=q_ref[0, rows, :], k=k_ref[0, rows, :],
                               v=v_ref[0, rows, :]))
        logits = [_dot(k["sm"], wg_ref[0, :, k["d"] * LANES:(k["d"] + 1) * LANES])
                  + bg_ref[0, :, k["d"] * LANES:(k["d"] + 1) * LANES] for k in ch]
        gs = [_sel_dot(tri[k["d"]], _log_sigmoid(lg) * (1.0 / GLA_TAU)) for k, lg in zip(ch, logits)]
        for k, g in zip(ch, gs):
            g_last = g[CHUNK - 1:CHUNK] if k["d"] == 0 else g[0:1]
            k["qd"] = (k["q"] * (GLA_DK ** -0.5) * jnp.exp(g)).astype(BF16)
            k["k_inv"] = k["k"] * jnp.exp(-g)
            k["k_end"] = k["k"] * jnp.exp(g_last - g)
            k["decay"] = jnp.exp(g_last)
        a_s = [jnp.where(keep2[k["d"]], _dot_nt(stack2(k["qd"].astype(F32)), stack2(k["k_inv"])), 0.0) for k in ch]
        intra = [fold2(_dot(a, stack2(k["v"]))) for k, a in zip(ch, a_s)]
        kvs = [jnp.where(same_head, _dot_tn(k["v"], k["k_end"]), 0.0) for k in ch]
        st = [st_scr[0], st_scr[1]]
        inter = []
        for k, kv in zip(ch, kvs):
            inter.append(_dot_nt(k["qd"], st[k["d"]]))
            st[k["d"]] = st[k["d"]] * k["decay"] + kv
        for k, oi, ox in zip(ch, intra, inter):
            if k["d"] == 0:
                of_scr[k["rows"], :] = oi + ox
            else:
                ob_scr[k["rows"], :] = oi + ox
        st_scr[0] = st[0]
        st_scr[1] = st[1]
        return carry

    lax.fori_loop(0, nc // unroll, body, 0)
    for d in range(2):
        st = st_scr[d]
        st_ref[0, d, 0] = st[:GLA_DV, :GLA_DK]
        st_ref[0, d, 1] = st[GLA_DV:, GLA_DK:]
    _out_norm_gate(of_scr, ob_scr, z_ref, gn_ref, o_ref, n)


def _gla(proj3, wg_p, bg_p, gn, s0t):
    b, n, _ = proj3.shape
    npair = GLA_HEADS // 2
    col = lambda base: (lambda bi, p: (bi, 0, base + p))
    s0r = _pair_major(s0t).reshape(b * npair, 2, 2, GLA_DV, GLA_DK)
    st_spec = pl.BlockSpec((1, 2, 2, GLA_DV, GLA_DK), lambda bi, p: (bi * npair + p, 0, 0, 0, 0))
    o, st = pl.pallas_call(
        functools.partial(_gla_kernel, n=n),
        grid=(b, npair),
        in_specs=[pl.BlockSpec((1, n, LANES), col(COL_GLA_Q)),
                  pl.BlockSpec((1, n, LANES), col(COL_GLA_K)),
                  pl.BlockSpec((1, n, LANES), col(COL_GLA_V)),
                  pl.BlockSpec((1, n, LANES), col(COL_GLA_Z)),
                  pl.BlockSpec((1, n, LANES), lambda bi, p: (bi, 0, COL_SMALL)),
                  pl.BlockSpec((1, LANES, 2 * LANES), lambda bi, p: (p, 0, 0)),
                  pl.BlockSpec((1, 1, 2 * LANES), lambda bi, p: (p, 0, 0)),
                  pl.BlockSpec((1, LANES), lambda bi, p: (0, 0)),
                  st_spec],
        out_specs=[pl.BlockSpec((1, n, LANES), lambda bi, p: (bi, 0, p)), st_spec],
        out_shape=[jax.ShapeDtypeStruct((b, n, GLA_W), F32),
                   jax.ShapeDtypeStruct(s0r.shape, F32)],
        scratch_shapes=[pltpu.VMEM((n, LANES), F32), pltpu.VMEM((n, LANES), F32),
                        pltpu.VMEM((2, 2 * GLA_DV, 2 * GLA_DK), F32)],
        compiler_params=_cparams(("arbitrary", "arbitrary")),
    )(proj3, proj3, proj3, proj3, proj3, wg_p, bg_p, gn, s0r)
    return o, st


def _gdn_kernel(q_ref, k_ref, v_ref, z_ref, sm_ref, cq_ref, ck_ref, cv_ref, alog_ref, dtb_ref,
                ex_ref, gn_ref, s0_ref,
                o_ref, st_ref, u_scr, w_scr, qd_scr, ke_scr, qk_scr, dec_scr, of_scr, ob_scr, st_scr, *, n):
    nc = n // CHUNK
    unroll = 4
    r, c = _tri_masks()
    tri = ((r >= c).astype(F32), (r <= c).astype(F32))
    row_id = _iota2((CHUNK, LANES), 0)
    lane_id = _iota2((CHUNK, LANES), 1)
    lo_lanes = lane_id < HEAD_W
    head_ones = _head_block_ones(LANES)
    r2 = _iota2((LANES, LANES), 0)
    c2 = _iota2((LANES, LANES), 1)
    same_head = (r2 // CHUNK) == (c2 // CHUNK)
    strict2 = (same_head & (r2 > c2), same_head & (r2 < c2))
    diag2 = r2 == c2
    blk16 = (r2 // 16) == (c2 // 16)

    def conv_silu(x_ref, w_ref, r0, first, last):
        cur = x_ref[0, pl.ds(r0, CHUNK), :]
        prev_row = x_ref[0, pl.ds(jnp.maximum(r0 - 1, 0), 1), :] * jnp.where(first, 0.0, 1.0)
        next_row = x_ref[0, pl.ds(jnp.minimum(r0 + CHUNK, n - 1), 1), :] * jnp.where(last, 0.0, 1.0)
        prev = jnp.where(row_id == 0, prev_row, pltpu.roll(cur, 1, 0))
        nxt = jnp.where(row_id == CHUNK - 1, next_row, pltpu.roll(cur, CHUNK - 1, 0))
        return _silu(prev * w_ref[0:1, :] + cur * w_ref[1:2, :] + nxt * w_ref[2:3, :])

    def stack2(x):
        return jnp.concatenate([jnp.where(lo_lanes, x, 0.0), jnp.where(lo_lanes, 0.0, x)], axis=0)

    def fold2(x):
        return x[:CHUNK] + x[CHUNK:]

    def bcast2(x):
        sw = pltpu.roll(x, HEAD_W, 1)
        return jnp.concatenate([jnp.where(lo_lanes, x, sw), jnp.where(lo_lanes, sw, x)], axis=0)

    def prep(it, carry):
        chunks = []
        for j in range(unroll):
            cc = it * unroll + j
            r0 = pl.multiple_of(cc * CHUNK, CHUNK)
            first = cc == 0
            last = cc == nc - 1
            chunks.append(dict(cc=cc, rows=pl.ds(r0, CHUNK), sm=sm_ref[0, pl.ds(r0, CHUNK), :],
                               qc=conv_silu(q_ref, cq_ref, r0, first, last),
                               kc=conv_silu(k_ref, ck_ref, r0, first, last),
                               vc=conv_silu(v_ref, cv_ref, r0, first, last)))
        qss = [_dot_sel(c["qc"] * c["qc"], head_ones) for c in chunks]
        kss = [_dot_sel(c["kc"] * c["kc"], head_ones) for c in chunks]
        for c, qs, ks in zip(chunks, qss, kss):
            c["qn"] = c["qc"] * lax.rsqrt(qs + EPS) * (GDN_DK ** -0.5)
            c["kn"] = c["kc"] * lax.rsqrt(ks + EPS)
            c["k2"] = stack2(c["kn"]).astype(BF16)
            la = -jnp.exp(alog_ref[...]) * _softplus(c["sm"] + dtb_ref[...])
            c["gates"] = jnp.where(lane_id >= SM_A, la, jax.nn.sigmoid(c["sm"]))
        kks = [_dot_nt(c["k2"], c["k2"]) for c in chunks]
        qk0s = [_dot_nt(stack2(c["qn"]), c["k2"]) for c in chunks]
        bls = [_dot_sel(c["gates"], ex_ref[0]) for c in chunks]
        for c, kk, qk0, bl in zip(chunks, kks, qk0s, bls):
            c["kk"], c["qk0"], c["bl"] = kk, qk0, bl

        ch = [dict(c=c, d=d, beta_x=c["bl"][:, 2 * d * LANES:(2 * d + 1) * LANES])
              for c in chunks for d in range(2)]
        g_xs = [_sel_dot(tri[k["d"]], k["c"]["bl"][:, (2 * k["d"] + 1) * LANES:(2 * k["d"] + 2) * LANES])
                for k in ch]
        ones8 = jnp.ones((8, LANES), F32)
        g_rows = [_sel_dot_nt(ones8, jnp.concatenate([jnp.where(lane_id == 0, g, 0.0),
                                                      jnp.where(lane_id == HEAD_W, g, 0.0)], axis=0))[0:1]
                  for g in g_xs]
        for k, g_x, g_row in zip(ch, g_xs, g_rows):
            d, c = k["d"], k["c"]
            k["g_x"] = g_x
            k["g_last"] = g_x[CHUNK - 1:CHUNK] if d == 0 else g_x[0:1]
            diff = bcast2(g_x) - g_row
            k["e_strict"] = jnp.where(strict2[d], jnp.exp(jnp.where(strict2[d], diff, 0.0)), 0.0)
            m = bcast2(k["beta_x"]) * c["kk"] * k["e_strict"]
            k["md"] = jnp.where(blk16, m, 0.0)
            k["lo"] = m - k["md"]
            k["eg"] = jnp.exp(g_x)
            k["rhs"] = jnp.concatenate([stack2(c["vc"] * k["beta_x"]),
                                        stack2(c["kn"] * k["beta_x"] * k["eg"])], axis=1)
        ps = [-k["md"] for k in ch]
        nds = list(ps)
        for _ in range(3):
            ps = [_dot(p, p) for p in ps]
            ts = [_dot(nd, p) for nd, p in zip(nds, ps)]
            nds = [nd + p + t for nd, p, t in zip(nds, ps, ts)]
        ts = [_dot(nd, k["rhs"]) for nd, k in zip(nds, ch)]
        xs = [k["rhs"] + t for k, t in zip(ch, ts)]
        for _ in range(CHUNK // 16 - 1):
            ts = [_dot(k["lo"], x) for k, x in zip(ch, xs)]
            ys = [k["rhs"] - t for k, t in zip(ch, ts)]
            ts = [_dot(nd, y) for nd, y in zip(nds, ys)]
            xs = [y + t for y, t in zip(ys, ts)]
        for k, x in zip(ch, xs):
            d, c = k["d"], k["c"]
            u_scr[d, c["rows"], :] = fold2(x[:, :LANES])
            w_scr[d, c["rows"], :] = fold2(x[:, LANES:]).astype(BF16)
            qd_scr[d, c["rows"], :] = (c["qn"] * k["eg"]).astype(BF16)
            ke_scr[d, c["rows"], :] = (c["kn"] * jnp.exp(k["g_last"] - k["g_x"])).astype(BF16)
            qk_scr[d, pl.ds(pl.multiple_of(c["cc"] * LANES, LANES), LANES), :] = (
                c["qk0"] * jnp.where(diag2, 1.0, k["e_strict"])).astype(BF16)
            dec_scr[d, pl.ds(pl.multiple_of(c["cc"] * 8, 8), 8), :] = jnp.broadcast_to(
                jnp.exp(k["g_last"]), (8, LANES))
        return carry

    lax.fori_loop(0, nc // unroll, prep, 0)

    zeros = jnp.zeros((GDN_DV, GDN_DK), F32)
    for d in range(2):
        st_scr[d] = jnp.concatenate([jnp.concatenate([s0_ref[0, d, 0], zeros], axis=1),
                                     jnp.concatenate([zeros, s0_ref[0, d, 1]], axis=1)], axis=0)

    def step(ci, carry):
        loaded = []
        for d in range(2):
            cc = ci if d == 0 else nc - 1 - ci
            rows = pl.ds(pl.multiple_of(cc * CHUNK, CHUNK), CHUNK)
            loaded.append((rows, st_scr[d], u_scr[d, rows, :], w_scr[d, rows, :], qd_scr[d, rows, :],
                           ke_scr[d, rows, :], qk_scr[d, pl.ds(pl.multiple_of(cc * LANES, LANES), LANES), :],
                           dec_scr[d, pl.ds(pl.multiple_of(cc * 8, 8), 1), :]))
        ws = [_dot_nt(w, st) for _, st, _, w, _, _, _, _ in loaded]
        qs = [_dot_nt(qd, st) for _, st, _, _, qd, _, _, _ in loaded]
        v_news = [ld[2] - t for ld, t in zip(loaded, ws)]
        os_ = [_dot(ld[6], stack2(v)) for ld, v in zip(loaded, v_news)]
        kvs = [_dot_tn(v, ld[5]) for ld, v in zip(loaded, v_news)]
        for d, (ld, q, o2, kv) in enumerate(zip(loaded, qs, os_, kvs)):
            rows, st, decay = ld[0], ld[1], ld[7]
            st_scr[d] = st * decay + jnp.where(same_head, kv, 0.0)
            if d == 0:
                of_scr[rows, :] = q + fold2(o2)
            else:
                ob_scr[rows, :] = q + fold2(o2)
        return carry

    lax.fori_loop(0, nc, step, 0)
    for d in range(2):
        st = st_scr[d]
        st_ref[0, d, 0] = st[:GDN_DV, :GDN_DK]
        st_ref[0, d, 1] = st[GDN_DV:, GDN_DK:]
    _out_norm_gate(of_scr, ob_scr, z_ref, gn_ref, o_ref, n)


def _gdn_select_constants():
    npair = GDN_HEADS // 2
    ex = np.zeros((npair, LANES, 4 * LANES), np.float32)
    for p in range(npair):
        for d in range(2):
            for h in range(2):
                hh = 2 * p + h
                lanes_b = slice(2 * d * LANES + h * HEAD_W, 2 * d * LANES + (h + 1) * HEAD_W)
                lanes_g = slice((2 * d + 1) * LANES + h * HEAD_W, (2 * d + 1) * LANES + (h + 1) * HEAD_W)
                ex[p, SM_BETA + d * GDN_HEADS + hh, lanes_b] = 1.0
                ex[p, SM_A + d * GDN_HEADS + hh, lanes_g] = 1.0
    return jnp.asarray(ex)


def _gdn(proj3, conv_w, alog_row, dtb_row, gn, s0t):
    b, n, _ = proj3.shape
    npair = GDN_HEADS // 2
    ex = _gdn_select_constants()
    nc = n // CHUNK
    col = lambda base: (lambda bi, p: (bi, 0, base + p))
    cw = GDN_HEADS * GDN_DK
    s0r = _pair_major(s0t).reshape(b * npair, 2, 2, GDN_DV, GDN_DK)
    st_spec = pl.BlockSpec((1, 2, 2, GDN_DV, GDN_DK), lambda bi, p: (bi * npair + p, 0, 0, 0, 0))
    pair_spec = lambda shp: pl.BlockSpec((1,) + shp, lambda bi, p: (p, 0, 0))
    o, st = pl.pallas_call(
        functools.partial(_gdn_kernel, n=n),
        grid=(b, npair),
        in_specs=[pl.BlockSpec((1, n, LANES), col(COL_GDN_Q)),
                  pl.BlockSpec((1, n, LANES), col(COL_GDN_K)),
                  pl.BlockSpec((1, n, LANES), col(COL_GDN_V)),
                  pl.BlockSpec((1, n, LANES), col(COL_GDN_Z)),
                  pl.BlockSpec((1, n, LANES), lambda bi, p: (bi, 0, COL_SMALL)),
                  pl.BlockSpec((3, LANES), lambda bi, p: (0, p)),
                  pl.BlockSpec((3, LANES), lambda bi, p: (0, cw // LANES + p)),
                  pl.BlockSpec((3, LANES), lambda bi, p: (0, 2 * cw // LANES + p)),
                  pl.BlockSpec((1, LANES), lambda bi, p: (0, 0)),
                  pl.BlockSpec((1, LANES), lambda bi, p: (0, 0)),
                  pair_spec((LANES, 4 * LANES)),
                  pl.BlockSpec((1, LANES), lambda bi, p: (0, 0)),
                  st_spec],
        out_specs=[pl.BlockSpec((1, n, LANES), lambda bi, p: (bi, 0, p)), st_spec],
        out_shape=[jax.ShapeDtypeStruct((b, n, GDN_W), F32),
                   jax.ShapeDtypeStruct(s0r.shape, F32)],
        scratch_shapes=[pltpu.VMEM((2, n, LANES), F32),
                        pltpu.VMEM((2, n, LANES), BF16),
                        pltpu.VMEM((2, n, LANES), BF16),
                        pltpu.VMEM((2, n, LANES), BF16),
                        pltpu.VMEM((2, nc * LANES, LANES), BF16),
                        pltpu.VMEM((2, nc * 8, LANES), F32),
                        pltpu.VMEM((n, LANES), F32), pltpu.VMEM((n, LANES), F32),
                        pltpu.VMEM((2, 2 * GDN_DV, 2 * GDN_DK), F32)],
        compiler_params=_cparams(("arbitrary", "arbitrary")),
    )(proj3, proj3, proj3, proj3, proj3, conv_w, conv_w, conv_w, alog_row, dtb_row, ex, gn, s0r)
    return o, st


def _outproj_kernel(omla_ref, ogla_ref, ogdn_ref, x_ref, mod_ref, nrm_ref, wout_ref, wr_ref,
                    x1_ref, xm_ref, afft_ref):
    y = (_dot(omla_ref[...], wout_ref[0:MLA_W, :])
         + _dot(ogla_ref[...], wout_ref[MLA_W:MLA_W + GLA_W, :])
         + _dot(ogdn_ref[...], wout_ref[MLA_W + GLA_W:, :]))
    m = mod_ref[0]
    x1 = x_ref[...] + m[2:3] * _rms(y, nrm_ref[1:2, :])
    x1_ref[...] = x1
    xm2 = _rms(x1, nrm_ref[2:3, :]) * (1.0 + m[4:5]) + m[3:4]
    d = xm2.shape[1]
    xm_ref[:, :d] = xm2
    logits = jnp.dot(xm2, wr_ref[...], preferred_element_type=F32, precision=lax.Precision.HIGHEST)
    lane = _iota2(logits.shape, 1)
    logits = jnp.where(lane < N_EXPERTS, logits, -jnp.inf)
    e = jnp.exp(logits - jnp.max(logits, axis=-1, keepdims=True))
    aff = e / jnp.sum(e, axis=-1, keepdims=True)
    xm_ref[:, d:] = aff
    sel = (_iota2((N_EXPERTS, LANES), 0) == _iota2((N_EXPERTS, LANES), 1)).astype(F32)
    afft_ref[...] = _sel_dot_nt(sel, aff)


def _out_projection(o_mla, o_gla, o_gdn, x2d, modg, nrm, w_out, w_router_p, tiles_per_batch, tm):
    t, d = x2d.shape
    per_batch = modg.shape[0] > 1
    return pl.pallas_call(
        _outproj_kernel,
        grid=(t // tm,),
        in_specs=[pl.BlockSpec((tm, MLA_W), lambda i: (i, 0)),
                  pl.BlockSpec((tm, GLA_W), lambda i: (i, 0)),
                  pl.BlockSpec((tm, GDN_W), lambda i: (i, 0)),
                  pl.BlockSpec((tm, d), lambda i: (i, 0)),
                  pl.BlockSpec((1, 6, d), (lambda i: (i // tiles_per_batch, 0, 0)) if per_batch
                               else (lambda i: (0, 0, 0))),
                  pl.BlockSpec((4, d), lambda i: (0, 0)),
                  pl.BlockSpec((d, d), lambda i: (0, 0)),
                  pl.BlockSpec((d, LANES), lambda i: (0, 0))],
        out_specs=[pl.BlockSpec((tm, d), lambda i: (i, 0)),
                   pl.BlockSpec((tm, d + LANES), lambda i: (i, 0)),
                   pl.BlockSpec((N_EXPERTS, tm), lambda i: (0, i))],
        out_shape=[jax.ShapeDtypeStruct((t, d), F32),
                   jax.ShapeDtypeStruct((t, d + LANES), F32),
                   jax.ShapeDtypeStruct((N_EXPERTS, t), F32)],
        compiler_params=_cparams(("arbitrary",)),
    )(o_mla, o_gla, o_gdn, x2d, modg, nrm, w_out, w_router_p)


def _route_kernel(afft_ref, rank_ref, pos_ref, *, cap):
    a = afft_ref[...]
    t = a.shape[1]

    def search(i, cur):
        cand = cur | jnp.left_shift(jnp.int32(1), 30 - i)
        cnt = jnp.sum((a >= lax.bitcast_convert_type(cand, F32)).astype(F32), axis=1, keepdims=True)
        return jnp.where(cnt >= cap, cand, cur)

    kth = lax.bitcast_convert_type(lax.fori_loop(0, 31, search, jnp.zeros((a.shape[0], 1), I32)), F32)
    gt = (a > kth).astype(F32)
    eq = (a == kth).astype(F32)
    need = cap - jnp.sum(gt, axis=1, keepdims=True)
    before = (_iota2((LANES, LANES), 0) < _iota2((LANES, LANES), 1)).astype(BF16)
    nb = t // LANES
    carry_eq = jnp.zeros((a.shape[0], 1), F32)
    carry = jnp.zeros((a.shape[0], 1), F32)
    for b in range(nb):
        sl = slice(b * LANES, (b + 1) * LANES)
        eqb = eq[:, sl]
        pre = jnp.dot(eqb.astype(BF16), before, preferred_element_type=F32) + carry_eq
        selb = jnp.maximum(gt[:, sl], jnp.where(pre < need, eqb, 0.0))
        carry_eq = carry_eq + jnp.sum(eqb, axis=1, keepdims=True)
        local = jnp.dot(selb.astype(BF16), before, preferred_element_type=F32)
        rank_ref[:, sl] = jnp.where(selb > 0.0, local, -1.0)
        pos_ref[:, sl] = local + carry
        carry = carry + jnp.sum(selb, axis=1, keepdims=True)


def _route(afft, cap):
    e, t = afft.shape
    return pl.pallas_call(
        functools.partial(_route_kernel, cap=cap),
        out_shape=[jax.ShapeDtypeStruct((e, t), F32), jax.ShapeDtypeStruct((e, t), F32)],
        compiler_params=_cparams(),
    )(afft)


def _compact_kernel(off_ref, rank_ref, idx_ref, *, nb):
    e = pl.program_id(0)
    idx_ref[...] = jnp.zeros(idx_ref.shape, I32)
    slot = _iota2((LANES, LANES), 0)
    tok = _iota2((LANES, LANES), 0).astype(BF16)
    group = 4

    def blks(g, carry):
        bs = [g * group + j for j in range(group)]
        onehots = [(rank_ref[0, pl.ds(b, 1), :].astype(I32) == slot).astype(BF16) for b in bs]
        ids = [jnp.dot(oh, tok, preferred_element_type=F32).astype(I32) + b * LANES for oh, b in zip(onehots, bs)]
        for b, v in zip(bs, ids):
            idx_ref[0, pl.ds(off_ref[e, b], LANES), :] = v
        return carry

    lax.fori_loop(0, nb // group, blks, 0)


def _compact(off, rank, cap):
    e, t = rank.shape
    nb = t // LANES
    rows = cap + LANES
    idx = pl.pallas_call(
        functools.partial(_compact_kernel, nb=nb),
        grid_spec=pltpu.PrefetchScalarGridSpec(
            num_scalar_prefetch=1,
            grid=(e,),
            in_specs=[pl.BlockSpec((1, nb, LANES), lambda ei, off_r: (ei, 0, 0))],
            out_specs=pl.BlockSpec((1, rows, LANES), lambda ei, off_r: (ei, 0, 0))),
        out_shape=jax.ShapeDtypeStruct((e, rows, LANES), I32),
        compiler_params=_cparams(("arbitrary",)),
    )(off, rank.reshape(e, nb, LANES))
    return idx[:, :cap, 0]


def _gather_copy(x_hbm, xe_scr, sem, slot, src_row, dst_row, nrows):
    return pltpu.make_async_copy(x_hbm.at[pl.ds(src_row, nrows)], xe_scr.at[slot, pl.ds(dst_row, nrows)],
                                 sem.at[slot])


def _expert_kernel(idx_ref, x_hbm, w1_ref, w3_ref, w2_ref, ye_ref, xe_scr, xb_scr, sem, *, cap):
    e = pl.program_id(0)
    ne = pl.num_programs(0)

    slot = e % 2
    nxt = (e + 1) % ne
    nxt_slot = 1 - slot

    @pl.when(e == 0)
    def _():
        def row(s, carry):
            _gather_copy(x_hbm, xe_scr, sem, 0, idx_ref[0, s], s, 1).start()
            return carry
        lax.fori_loop(0, cap, row, 0)

    _gather_copy(x_hbm, xe_scr, sem, slot, 0, 0, cap).wait()
    d = xb_scr.shape[1]
    xb_scr[...] = xe_scr[slot, :, :d].astype(BF16)
    ge = xe_scr[slot, :, d:]
    gate = jnp.sum(jnp.where(_iota2(ge.shape, 1) == e, ge, 0.0), axis=-1, keepdims=True)
    ff = w1_ref.shape[2]
    fchunk = 512
    ndots = 3 * (ff // fchunk)
    per_dot = -(-cap // ndots)

    def prefetch(i):
        for s in range(i * per_dot, min((i + 1) * per_dot, cap)):
            _gather_copy(x_hbm, xe_scr, sem, nxt_slot, idx_ref[nxt, s], s, 1).start()

    y = jnp.zeros((cap, w2_ref.shape[2]), F32)
    for j in range(ff // fchunk):
        fs = slice(j * fchunk, (j + 1) * fchunk)
        prefetch(3 * j)
        h1 = _dot(xb_scr[...], w1_ref[0, :, fs])
        prefetch(3 * j + 1)
        h3 = _dot(xb_scr[...], w3_ref[0, :, fs])
        prefetch(3 * j + 2)
        y = y + _dot(_silu(h1) * h3, w2_ref[0, fs, :])
    ye_ref[0] = y * gate

    @pl.when(e == ne - 1)
    def _():
        _gather_copy(x_hbm, xe_scr, sem, nxt_slot, 0, 0, cap).wait()


def _expert_ffn(idx, xm, w1, w3, w2, cap):
    e, d, ff = w1.shape
    wspec = lambda shp: pl.BlockSpec((1,) + shp, lambda ei, idx_r: (ei, 0, 0))
    return pl.pallas_call(
        functools.partial(_expert_kernel, cap=cap),
        grid_spec=pltpu.PrefetchScalarGridSpec(
            num_scalar_prefetch=1,
            grid=(e,),
            in_specs=[pl.BlockSpec(memory_space=pl.ANY),
                      wspec((d, ff)), wspec((d, ff)), wspec((ff, d))],
            out_specs=pl.BlockSpec((1, cap, d), lambda ei, idx_r: (ei, 0, 0)),
            scratch_shapes=[pltpu.VMEM((2, cap, xm.shape[1]), F32), pltpu.VMEM((cap, d), BF16),
                            pltpu.SemaphoreType.DMA((2,))]),
        out_shape=jax.ShapeDtypeStruct((e, cap, d), F32),
        compiler_params=_cparams(("arbitrary",)),
    )(idx, xm, w1, w3, w2)


def _scatter_kernel(idx_ref, lo_ref, ye_ref, o_ref, *, cap, nparts, part_rows):
    part = pl.program_id(0)
    e = pl.program_id(1)

    @pl.when(e == 0)
    def _():
        o_ref[...] = jnp.zeros(o_ref.shape, F32)

    base = part * part_rows

    lo = lo_ref[e, part]
    hi = lo_ref[e, part + 1]
    group = 4
    ngroups = (hi - lo) // group

    def add_group(g, carry):
        s0 = lo + g * group
        ts = [idx_ref[e, s0 + j] - base for j in range(group)]
        new = [o_ref[pl.ds(t, 1), :] + ye_ref[0, pl.ds(s0 + j, 1), :] for j, t in enumerate(ts)]
        for t, v in zip(ts, new):
            o_ref[pl.ds(t, 1), :] = v
        return carry

    lax.fori_loop(0, ngroups, add_group, 0)

    def add(s, carry):
        t = idx_ref[e, s] - base
        o_ref[pl.ds(t, 1), :] = o_ref[pl.ds(t, 1), :] + ye_ref[0, pl.ds(s, 1), :]
        return carry

    lax.fori_loop(lo + ngroups * group, hi, add, 0)


def _scatter(idx, bounds, ye, t, nparts):
    e, cap, d = ye.shape
    part_rows = t // nparts
    return pl.pallas_call(
        functools.partial(_scatter_kernel, cap=cap, nparts=nparts, part_rows=part_rows),
        grid_spec=pltpu.PrefetchScalarGridSpec(
            num_scalar_prefetch=2,
            grid=(nparts, e),
            in_specs=[pl.BlockSpec((1, cap, d), lambda pi, ei, a, b: (ei, 0, 0))],
            out_specs=pl.BlockSpec((part_rows, d), lambda pi, ei, a, b: (pi, 0))),
        out_shape=jax.ShapeDtypeStruct((t, d), F32),
        compiler_params=_cparams(("arbitrary", "arbitrary")),
    )(idx, bounds, ye)


def _residual_kernel(x1_ref, moe_ref, mod_ref, nrm_ref, o_ref):
    m = mod_ref[0]
    o_ref[...] = x1_ref[...] + m[5:6] * _rms(moe_ref[...], nrm_ref[3:4, :])


def _ffn_residual(x1, moe, modg, nrm, tiles_per_batch, tm):
    t, d = x1.shape
    per_batch = modg.shape[0] > 1
    return pl.pallas_call(
        _residual_kernel,
        grid=(t // tm,),
        in_specs=[pl.BlockSpec((tm, d), lambda i: (i, 0)),
                  pl.BlockSpec((tm, d), lambda i: (i, 0)),
                  pl.BlockSpec((1, 6, d), (lambda i: (i // tiles_per_batch, 0, 0)) if per_batch
                               else (lambda i: (0, 0, 0))),
                  pl.BlockSpec((4, d), lambda i: (0, 0))],
        out_specs=pl.BlockSpec((tm, d), lambda i: (i, 0)),
        out_shape=jax.ShapeDtypeStruct((t, d), F32),
        compiler_params=_cparams(("arbitrary",)),
    )(x1, moe, modg, nrm)


def _rot_cols(w):
    s = w.shape[:-1]
    w2 = w.reshape(s + (ROPE_DIM // 2, 2))
    return jnp.stack([-w2[..., 1], w2[..., 0]], axis=-1).reshape(s + (ROPE_DIM,))


def _layer_weights(l, w_in, mla_q_norm, mla_wq_b, mla_kv_norm, mla_wkv_b, gla_wg, gla_bg, gla_out_norm,
                   gdn_conv, gdn_a_log, gdn_dt_bias, gdn_out_norm, w_out, sandwich_norms, w_router,
                   w_e1, w_e3, w_e2):
    wi = w_in[l]
    o = 0
    parts = {}
    for name, sz in (("q_lat", Q_LORA), ("kv_lat", KV_LORA), ("k_rope", ROPE_DIM),
                     ("gla_q", GLA_W), ("gla_k", GLA_W), ("gla_v", GLA_W), ("gla_glr", 2 * GLA_GATE_RANK),
                     ("gla_z", GLA_W), ("gdn_q", GDN_W), ("gdn_k", GDN_W), ("gdn_v", GDN_W), ("gdn_z", GDN_W),
                     ("gdn_b", 2 * GDN_HEADS), ("gdn_a", 2 * GDN_HEADS)):
        parts[name] = wi[:, o:o + sz]
        o += sz
    w_in_p = jnp.concatenate(
        [parts[k] for k in ("q_lat", "kv_lat", "gla_q", "gla_k", "gla_v", "gla_z",
                            "gdn_q", "gdn_k", "gdn_v", "gdn_z")]
        + [parts["k_rope"], _rot_cols(parts["k_rope"]), parts["gla_glr"], parts["gdn_b"], parts["gdn_a"]],
        axis=1).astype(BF16)

    wq = mla_wq_b[l].reshape(Q_LORA, MLA_HEADS, NOPE_DIM + ROPE_DIM)
    wq_ext = jnp.concatenate([wq, _rot_cols(wq[..., NOPE_DIM:])], axis=-1).reshape(Q_LORA, -1).astype(BF16)
    wkv = mla_wkv_b[l].reshape(KV_LORA, MLA_HEADS, NOPE_DIM + MLA_V_DIM)
    wkv_ext = jnp.concatenate([wkv[..., :NOPE_DIM].reshape(KV_LORA, -1),
                               wkv[..., NOPE_DIM:].reshape(KV_LORA, -1)], axis=1).astype(BF16)

    npair = GLA_HEADS // 2
    wg_p = jnp.zeros((npair, LANES, 2 * LANES), F32)
    bg_p = jnp.zeros((npair, 1, 2 * LANES), F32)
    for p in range(npair):
        for d in range(2):
            rows = slice(SM_GLR + d * GLA_GATE_RANK, SM_GLR + (d + 1) * GLA_GATE_RANK)
            wg_p = wg_p.at[p, rows, d * LANES:(d + 1) * LANES].set(gla_wg[l, d][:, p * LANES:(p + 1) * LANES])
            bg_p = bg_p.at[p, 0, d * LANES:(d + 1) * LANES].set(gla_bg[l, d][p * LANES:(p + 1) * LANES])

    alog_row = jnp.zeros((1, LANES), F32).at[0, SM_A:].set(gdn_a_log[l].reshape(-1))
    dtb_row = jnp.zeros((1, LANES), F32).at[0, SM_A:].set(gdn_dt_bias[l].reshape(-1))
    w_router_p = jnp.zeros((D_MODEL, LANES), F32).at[:, :N_EXPERTS].set(w_router[l])
    return dict(
        w_in=w_in_p, gq=mla_q_norm[l][None], wq=wq_ext, gkv=mla_kv_norm[l][None], wkv=wkv_ext,
        wg=wg_p.astype(BF16), bg=bg_p, gla_gn=jnp.tile(gla_out_norm[l], 2)[None],
        conv=gdn_conv[l], alog=alog_row, dtb=dtb_row, gdn_gn=jnp.tile(gdn_out_norm[l], 2)[None],
        w_out=w_out[l].astype(BF16), nrm=sandwich_norms[l], w_router=w_router_p,
        w1=w_e1[l].astype(BF16), w3=w_e3[l].astype(BF16), w2=w_e2[l].astype(BF16))


def _rope_tables(n, rotate):
    ones = np.ones((1, NOPE_DIM), np.float32)
    if not rotate:
        t1 = np.concatenate([ones, np.ones((1, ROPE_DIM), np.float32), np.zeros((1, ROPE_DIM), np.float32)], 1)
        z = np.zeros_like(t1)
        return tuple(jnp.asarray(np.broadcast_to(t, (n, LANES)).copy()) for t in (t1, z, z))
    rows = n // GRID_W
    row = jnp.repeat(jnp.arange(rows), GRID_W).astype(F32)
    col = jnp.tile(jnp.arange(GRID_W), rows).astype(F32)
    half = ROPE_DIM // 2
    freqs = ROPE_BASE ** (-jnp.arange(0, half, 2, dtype=F32) / half)
    ang = jnp.concatenate([row[:, None] * freqs, col[:, None] * freqs], axis=-1)
    cos = jnp.repeat(jnp.cos(ang), 2, axis=-1)
    sin = jnp.repeat(jnp.sin(ang), 2, axis=-1)
    z32 = jnp.zeros((n, ROPE_DIM), F32)
    t1 = jnp.concatenate([jnp.ones((n, NOPE_DIM), F32), cos, z32], axis=1)
    t2 = jnp.concatenate([jnp.zeros((n, NOPE_DIM), F32), sin, z32], axis=1)
    t3 = jnp.concatenate([jnp.zeros((n, NOPE_DIM), F32), z32, jnp.ones((n, ROPE_DIM), F32)], axis=1)
    return t1, t2, t3


def _trunk_layer(x, lw, modg, ctx_cache, tabs):
    b, n, d = x.shape
    t = b * n
    tm = min(512, n)
    tiles_per_batch = n // tm
    x2d = x.reshape(t, d)

    proj = _in_projection(x2d, modg, lw["nrm"][0:1], lw["w_in"], tiles_per_batch, tm)
    proj3 = proj.reshape(b, n, PROJ_W)

    q, k, v, ckv, k_rope = _mla_prep(proj, tabs, lw["gq"], lw["wq"], lw["gkv"], lw["wkv"], tiles_per_batch, tm)
    qk_w = MLA_HEADS * MLA_QK_PAD
    if ctx_cache is None:
        kc = vc = None
        s0_gla = jnp.zeros((b, 2, GLA_HEADS, GLA_DV, GLA_DK), F32)
        s0_gdn = jnp.zeros((b, 2, GDN_HEADS, GDN_DV, GDN_DK), F32)
    else:
        ckv_c, kr_c, s0_gla, s0_gdn = ctx_cache
        lc = ckv_c.shape[1]
        kc, vc = _ctx_keys(ckv_c.reshape(b * lc, KV_LORA), kr_c.reshape(b * lc, ROPE_DIM), lw["wkv"])
        kc = kc.reshape(b, lc, qk_w)
        vc = vc.reshape(b, lc, MLA_W)
        s0_gla = jnp.swapaxes(s0_gla, -1, -2)
        s0_gdn = jnp.swapaxes(s0_gdn, -1, -2)
    o_mla = _attention(q.reshape(b, n, qk_w), k.reshape(b, n, qk_w), v.reshape(b, n, MLA_W), kc, vc,
                       tq=min(256, n))

    o_gla, st_gla = _gla(proj3, lw["wg"], lw["bg"], lw["gla_gn"], s0_gla)
    o_gdn, st_gdn = _gdn(proj3, lw["conv"], lw["alog"], lw["dtb"], lw["gdn_gn"], s0_gdn)

    x1, xm2, afft = _out_projection(o_mla.reshape(t, MLA_W), o_gla.reshape(t, GLA_W),
                                        o_gdn.reshape(t, GDN_W), x2d, modg, lw["nrm"], lw["w_out"],
                                        lw["w_router"], tiles_per_batch, tm)

    cap = EC_CAPACITY * t // N_EXPERTS
    rank, pos = _route(afft, cap)
    off = pos[:, ::LANES].astype(I32)
    idx = _compact(off, rank, cap)
    ye = _expert_ffn(idx, xm2, lw["w1"], lw["w3"], lw["w2"], cap)
    nparts = 2
    part_blocks = t // nparts // LANES
    bounds = jnp.concatenate([off[:, ::part_blocks], jnp.full((N_EXPERTS, 1), cap, I32)], axis=1)
    moe = _scatter(idx, bounds, ye, t, nparts)
    x2 = _ffn_residual(x1, moe, modg, lw["nrm"], tiles_per_batch, tm)

    new = None
    if ctx_cache is None:
        st_gla = jnp.swapaxes(st_gla.reshape(b, GLA_HEADS // 2, 2, 2, GLA_DV, GLA_DK), 1, 2)
        st_gdn = jnp.swapaxes(st_gdn.reshape(b, GDN_HEADS // 2, 2, 2, GDN_DV, GDN_DK), 1, 2)
        new = (ckv.reshape(b, n, KV_LORA), k_rope.reshape(b, n, ROPE_DIM),
               jnp.swapaxes(st_gla.reshape(b, 2, GLA_HEADS, GLA_DV, GLA_DK), -1, -2),
               jnp.swapaxes(st_gdn.reshape(b, 2, GDN_HEADS, GDN_DV, GDN_DK), -1, -2))
    return x2.reshape(b, n, d), new


def _pair_major(s0):
    b, two, h = s0.shape[:3]
    return jnp.swapaxes(s0.reshape(b, 2, h // 2, 2, s0.shape[-2], s0.shape[-1]), 1, 2)


def kernel(x_prompt, x_sample, cache_ckv, cache_krope, state_gla, state_gdn, c, c_ctx, w_in, mla_q_norm,
           mla_wq_b, mla_kv_norm, mla_wkv_b, gla_wg, gla_bg, gla_out_norm, gdn_conv, gdn_a_log, gdn_dt_bias,
           gdn_out_norm, w_out, w_mod, b_mod, sandwich_norms, w_router, w_e1, w_e3, w_e2):
    depth = w_in.shape[0]
    nb_s = x_sample.shape[0]
    cond = jnp.zeros((8, D_MODEL), F32).at[0].set(c_ctx).at[1:1 + nb_s].set(c)
    mod = _modulation(cond, w_mod, b_mod)

    tabs_ctx = _rope_tables(8, rotate=False)
    tabs_smp = _rope_tables(x_sample.shape[1], rotate=True)
    tabs_ctx = tuple(jnp.broadcast_to(t[:1], (min(512, x_prompt.shape[1]), LANES)) for t in tabs_ctx)

    lws = [_layer_weights(l, w_in, mla_q_norm, mla_wq_b, mla_kv_norm, mla_wkv_b, gla_wg, gla_bg, gla_out_norm,
                          gdn_conv, gdn_a_log, gdn_dt_bias, gdn_out_norm, w_out, sandwich_norms, w_router,
                          w_e1, w_e3, w_e2) for l in range(depth)]

    xp = x_prompt
    ckv_l, kr_l, sg_l, sd_l = [], [], [], []
    for l in range(depth):
        modg = mod[l, 0:1].reshape(1, 6, D_MODEL)
        xp, (ckv, kr, sg, sd) = _trunk_layer(xp, lws[l], modg, None, tabs_ctx)
        ckv_l.append(ckv)
        kr_l.append(kr)
        sg_l.append(sg)
        sd_l.append(sd)

    xs = x_sample
    for l in range(depth):
        modg = mod[l, 1:1 + nb_s].reshape(nb_s, 6, D_MODEL)
        xs, _ = _trunk_layer(xs, lws[l], modg,
                             (cache_ckv[:, l], cache_krope[:, l], state_gla[:, l], state_gdn[:, l]), tabs_smp)

    return (xp, xs, jnp.stack(ckv_l, axis=1), jnp.stack(kr_l, axis=1),
            jnp.stack(sg_l, axis=1), jnp.stack(sd_l, axis=1))
```

```python
import functools

import numpy as np
import jax
import jax.numpy as jnp
from jax import lax
from jax.experimental import pallas as pl
from jax.experimental.pallas import tpu as pltpu

F32 = jnp.float32
BF16 = jnp.bfloat16
I32 = jnp.int32

D_MODEL = 1024
DEPTH = 2
GRID_W = 64
EPS = 1e-6
CHUNK = 64

MLA_HEADS = 4
Q_LORA = 256
KV_LORA = 128
NOPE_DIM = 64
ROPE_DIM = 32
MLA_V_DIM = 64
ROPE_BASE = 10000.0
MLA_SCALE = (NOPE_DIM + ROPE_DIM) ** -0.5
MLA_QK_PAD = 128

GLA_HEADS = 4
GLA_DK = 64
GLA_DV = 64
GLA_GATE_RANK = 16
GLA_TAU = 16.0

GDN_HEADS = 8
GDN_DK = 64
GDN_DV = 64

N_EXPERTS = 16
EXPERT_FF = 1024
EC_CAPACITY = 2

LANES = 128
HEAD_W = 64
MLA_W = MLA_HEADS * MLA_V_DIM
GLA_W = GLA_HEADS * GLA_DV
GDN_W = GDN_HEADS * GDN_DV

COL_QLAT = 0
COL_KVLAT = 2
COL_GLA_Q = 3
COL_GLA_K = 5
COL_GLA_V = 7
COL_GLA_Z = 9
COL_GDN_Q = 11
COL_GDN_K = 15
COL_GDN_V = 19
COL_GDN_Z = 23
COL_SMALL = 27
PROJ_W = 28 * LANES
SM_KROPE = 0
SM_KROT = 32
SM_GLR = 64
SM_BETA = 96
SM_A = 112

VMEM_LIMIT = 56 * 1024 * 1024


def _cparams(sem=None, **kw):
    return pltpu.CompilerParams(dimension_semantics=sem, vmem_limit_bytes=VMEM_LIMIT, **kw)


def _dot(a, b):
    return jnp.dot(a.astype(BF16), b.astype(BF16), preferred_element_type=F32)


def _dot_nt(a, b):
    return lax.dot_general(a.astype(BF16), b.astype(BF16), (((1,), (1,)), ((), ())),
                           preferred_element_type=F32)


def _dot_tn(a, b):
    return lax.dot_general(a.astype(BF16), b.astype(BF16), (((0,), (0,)), ((), ())),
                           preferred_element_type=F32)


def _dot_x3(a, b):
    a1 = a.astype(BF16)
    a2 = (a - a1.astype(F32)).astype(BF16)
    b1 = b.astype(BF16)
    b2 = (b - b1.astype(F32)).astype(BF16)
    return (jnp.dot(a1, b1, preferred_element_type=F32) + jnp.dot(a1, b2, preferred_element_type=F32)
            + jnp.dot(a2, b1, preferred_element_type=F32))


def _split(x, pieces):
    out = []
    for _ in range(pieces):
        p = x.astype(BF16)
        out.append(p)
        x = x - p.astype(F32)
    return out


def _dot_sel(x, sel, pieces=2):
    s = sel.astype(BF16)
    return sum(jnp.dot(p, s, preferred_element_type=F32) for p in _split(x, pieces))


def _sel_dot(sel, x, pieces=2):
    s = sel.astype(BF16)
    return sum(jnp.dot(s, p, preferred_element_type=F32) for p in _split(x, pieces))


def _sel_dot_nt(sel, x, pieces=2):
    s = sel.astype(BF16)
    dn = (((1,), (1,)), ((), ()))
    return sum(lax.dot_general(s, p, dn, preferred_element_type=F32) for p in _split(x, pieces))


def _silu(x):
    return x * jax.nn.sigmoid(x)


def _softplus(x):
    return jnp.maximum(x, 0.0) + jnp.log(1.0 + jnp.exp(-jnp.abs(x)))


def _log_sigmoid(x):
    return -_softplus(-x)


def _rms(x, g):
    return x * lax.rsqrt(jnp.mean(x * x, axis=-1, keepdims=True) + EPS) * g


def _iota2(shape, dim):
    return lax.broadcasted_iota(I32, shape, dim)


def _head_block_ones(width):
    r = _iota2((width, width), 0) // HEAD_W
    c = _iota2((width, width), 1) // HEAD_W
    return (r == c).astype(F32)


def _mod_kernel(cond_ref, w_ref, b_ref, o_ref):
    s = _silu(cond_ref[...])
    o_ref[0] = _dot(s, w_ref[0]) + b_ref[0]


def _modulation(cond, w_mod, b_mod):
    depth, d, d6 = w_mod.shape
    nj = d6 // d
    return pl.pallas_call(
        _mod_kernel,
        grid=(depth, nj),
        in_specs=[pl.BlockSpec((8, d), lambda l, j: (0, 0)),
                  pl.BlockSpec((1, d, d), lambda l, j: (l, 0, j)),
                  pl.BlockSpec((1, 1, d), lambda l, j: (l, 0, j))],
        out_specs=pl.BlockSpec((1, 8, d), lambda l, j: (l, 0, j)),
        out_shape=jax.ShapeDtypeStruct((depth, 8, d6), F32),
        compiler_params=_cparams(("arbitrary", "arbitrary")),
    )(cond, w_mod, b_mod.reshape(depth, 1, d6))


def _inproj_kernel(x_ref, mod_ref, nrm_ref, w_ref, o_ref, *, col_chunk):
    x = x_ref[...]
    m = mod_ref[0]
    xm = (_rms(x, nrm_ref[...]) * (1.0 + m[1:2]) + m[0:1]).astype(BF16)
    for j in range(o_ref.shape[1] // col_chunk):
        sl = slice(j * col_chunk, (j + 1) * col_chunk)
        o_ref[:, sl] = jnp.dot(xm, w_ref[:, sl], preferred_element_type=F32)


def _in_projection(x2d, modg, nrm0, w_in_p, tiles_per_batch, tm):
    t, d = x2d.shape
    pw = w_in_p.shape[1]
    per_batch = modg.shape[0] > 1
    return pl.pallas_call(
        functools.partial(_inproj_kernel, col_chunk=512),
        grid=(t // tm,),
        in_specs=[pl.BlockSpec((tm, d), lambda i: (i, 0)),
                  pl.BlockSpec((1, 6, d), (lambda i: (i // tiles_per_batch, 0, 0)) if per_batch
                               else (lambda i: (0, 0, 0))),
                  pl.BlockSpec((1, d), lambda i: (0, 0)),
                  pl.BlockSpec((d, pw), lambda i: (0, 0))],
        out_specs=pl.BlockSpec((tm, pw), lambda i: (i, 0)),
        out_shape=jax.ShapeDtypeStruct((t, pw), F32),
        compiler_params=_cparams(("arbitrary",)),
    )(x2d, modg, nrm0, w_in_p)


def _assemble_keys(k_nope, k_mid, k_last):
    parts = []
    for h in range(MLA_HEADS):
        parts += [k_nope[:, h * NOPE_DIM:(h + 1) * NOPE_DIM], k_mid, k_last]
    return jnp.concatenate(parts, axis=-1)


def _mla_prep_kernel(qlat_ref, kvlat_ref, sm_ref, t1_ref, t2_ref, t3_ref, gq_ref, wq_ref, gkv_ref, wkv_ref,
                     q_ref, k_ref, v_ref, ckv_ref, kr_ref):
    t1 = t1_ref[...]
    t2 = t2_ref[...]
    t3 = t3_ref[...]
    qn = _rms(qlat_ref[...], gq_ref[...])
    raw = _dot(qn, wq_ref[...])
    w = raw.shape[1]
    tile = lambda t: jnp.concatenate([t] * MLA_HEADS, axis=-1)
    q = raw * tile(t1) + pltpu.roll(raw, w - ROPE_DIM, 1) * tile(t2) + pltpu.roll(raw, ROPE_DIM, 1) * tile(t3)
    q_ref[...] = q.astype(q_ref.dtype)

    ckv = _rms(kvlat_ref[...], gkv_ref[...])
    ckv_ref[...] = ckv
    kv = _dot(ckv, wkv_ref[...])
    sm = sm_ref[...]
    k_rope = sm[:, SM_KROPE:SM_KROPE + ROPE_DIM]
    k_rot = sm[:, SM_KROT:SM_KROT + ROPE_DIM]
    kr_ref[...] = k_rope
    cs = slice(NOPE_DIM, NOPE_DIM + ROPE_DIM)
    kr = k_rope * t1[:, cs] + k_rot * t2[:, cs]
    k = _assemble_keys(kv[:, :MLA_HEADS * NOPE_DIM], kr, jnp.zeros_like(kr))
    k_ref[...] = k.astype(k_ref.dtype)
    v_ref[...] = kv[:, MLA_HEADS * NOPE_DIM:].astype(v_ref.dtype)


def _mla_prep(proj, tabs, gq, wq_ext, gkv, wkv_ext, tiles_per_seq, tm):
    t = proj.shape[0]
    t1, t2, t3 = tabs
    per_pos = t1.shape[0] > tm
    tab_map = (lambda i: (i % tiles_per_seq, 0)) if per_pos else (lambda i: (0, 0))
    qk_w = MLA_HEADS * MLA_QK_PAD
    return pl.pallas_call(
        _mla_prep_kernel,
        grid=(t // tm,),
        in_specs=[pl.BlockSpec((tm, Q_LORA), lambda i: (i, COL_QLAT * LANES // Q_LORA)),
                  pl.BlockSpec((tm, KV_LORA), lambda i: (i, COL_KVLAT)),
                  pl.BlockSpec((tm, LANES), lambda i: (i, COL_SMALL)),
                  pl.BlockSpec((tm, LANES), tab_map),
                  pl.BlockSpec((tm, LANES), tab_map),
                  pl.BlockSpec((tm, LANES), tab_map),
                  pl.BlockSpec((1, Q_LORA), lambda i: (0, 0)),
                  pl.BlockSpec((Q_LORA, qk_w), lambda i: (0, 0)),
                  pl.BlockSpec((1, KV_LORA), lambda i: (0, 0)),
                  pl.BlockSpec((KV_LORA, 2 * MLA_W), lambda i: (0, 0))],
        out_specs=[pl.BlockSpec((tm, qk_w), lambda i: (i, 0)),
                   pl.BlockSpec((tm, qk_w), lambda i: (i, 0)),
                   pl.BlockSpec((tm, MLA_W), lambda i: (i, 0)),
                   pl.BlockSpec((tm, KV_LORA), lambda i: (i, 0)),
                   pl.BlockSpec((tm, ROPE_DIM), lambda i: (i, 0))],
        out_shape=[jax.ShapeDtypeStruct((t, qk_w), BF16),
                   jax.ShapeDtypeStruct((t, qk_w), BF16),
                   jax.ShapeDtypeStruct((t, MLA_W), BF16),
                   jax.ShapeDtypeStruct((t, KV_LORA), F32),
                   jax.ShapeDtypeStruct((t, ROPE_DIM), F32)],
        compiler_params=_cparams(("arbitrary",)),
    )(proj, proj, proj, t1, t2, t3, gq, wq_ext, gkv, wkv_ext)


def _ctxkeys_kernel(ckv_ref, kr_ref, wkv_ref, k_ref, v_ref):
    kv = _dot(ckv_ref[...], wkv_ref[...])
    kr = kr_ref[...]
    k = _assemble_keys(kv[:, :MLA_HEADS * NOPE_DIM], jnp.zeros_like(kr), kr)
    k_ref[...] = k.astype(k_ref.dtype)
    v_ref[...] = kv[:, MLA_HEADS * NOPE_DIM:].astype(v_ref.dtype)


def _ctx_keys(ckv_c, kr_c, wkv_ext):
    t = ckv_c.shape[0]
    qk_w = MLA_HEADS * MLA_QK_PAD
    return pl.pallas_call(
        _ctxkeys_kernel,
        out_shape=[jax.ShapeDtypeStruct((t, qk_w), BF16), jax.ShapeDtypeStruct((t, MLA_W), BF16)],
        compiler_params=_cparams(),
    )(ckv_c, kr_c, wkv_ext)


def _attn_kernel(*refs, has_ctx):
    if has_ctx:
        q_ref, k_ref, v_ref, kc_ref, vc_ref, o_ref = refs
    else:
        q_ref, k_ref, v_ref, o_ref = refs
    outs = []
    for h in range(MLA_HEADS):
        qs = slice(h * MLA_QK_PAD, (h + 1) * MLA_QK_PAD)
        vs = slice(h * MLA_V_DIM, (h + 1) * MLA_V_DIM)
        q = q_ref[0, :, qs]
        s = _dot_nt(q, k_ref[0, :, qs]) * MLA_SCALE
        m = jnp.max(s, axis=-1, keepdims=True)
        if has_ctx:
            sc = _dot_nt(q, kc_ref[0, :, qs]) * MLA_SCALE
            m = jnp.maximum(m, jnp.max(sc, axis=-1, keepdims=True))
        p = jnp.exp(s - m)
        den = jnp.sum(p, axis=-1, keepdims=True)
        acc = _dot(p, v_ref[0, :, vs])
        if has_ctx:
            pc = jnp.exp(sc - m)
            den = den + jnp.sum(pc, axis=-1, keepdims=True)
            acc = acc + _dot(pc, vc_ref[0, :, vs])
        outs.append(acc / den)
    o_ref[0] = jnp.concatenate(outs, axis=-1)


def _attention(q, k, v, kc, vc, tq):
    b, n, qk_w = q.shape
    has_ctx = kc is not None
    in_specs = [pl.BlockSpec((1, tq, qk_w), lambda bi, i: (bi, i, 0)),
                pl.BlockSpec((1, n, qk_w), lambda bi, i: (bi, 0, 0)),
                pl.BlockSpec((1, n, MLA_W), lambda bi, i: (bi, 0, 0))]
    args = [q, k, v]
    if has_ctx:
        lc = kc.shape[1]
        in_specs += [pl.BlockSpec((1, lc, qk_w), lambda bi, i: (bi, 0, 0)),
                     pl.BlockSpec((1, lc, MLA_W), lambda bi, i: (bi, 0, 0))]
        args += [kc, vc]
    return pl.pallas_call(
        functools.partial(_attn_kernel, has_ctx=has_ctx),
        grid=(b, n // tq),
        in_specs=in_specs,
        out_specs=pl.BlockSpec((1, tq, MLA_W), lambda bi, i: (bi, i, 0)),
        out_shape=jax.ShapeDtypeStruct((b, n, MLA_W), F32),
        compiler_params=_cparams(("arbitrary", "arbitrary")),
    )(*args)


def _tri_masks():
    r = _iota2((CHUNK, CHUNK), 0)
    c = _iota2((CHUNK, CHUNK), 1)
    return r, c


def _out_norm_gate(o_f_ref, o_b_ref, z_ref, gn_ref, o_ref, n):
    bd = _head_block_ones(LANES) * (1.0 / HEAD_W)
    rows = min(n, 512)

    def blk(i, carry):
        r0 = pl.multiple_of(i * rows, rows)
        o = o_f_ref[pl.ds(r0, rows), :] + o_b_ref[pl.ds(r0, rows), :]
        ms = _dot_sel(o * o, bd)
        z = z_ref[0, pl.ds(r0, rows), :]
        o_ref[0, pl.ds(r0, rows), :] = o * lax.rsqrt(ms + EPS) * gn_ref[...] * _silu(z)
        return carry

    lax.fori_loop(0, n // rows, blk, 0)


def _gla_kernel(q_ref, k_ref, v_ref, z_ref, sm_ref, wg_ref, bg_ref, gn_ref, s0_ref,
                o_ref, st_ref, of_scr, ob_scr, st_scr, *, n):
    nc = n // CHUNK
    unroll = 4
    r, c = _tri_masks()
    tri = ((r >= c).astype(F32), (r <= c).astype(F32))
    lo_lanes = _iota2((CHUNK, LANES), 1) < HEAD_W
    r2 = _iota2((LANES, LANES), 0)
    c2 = _iota2((LANES, LANES), 1)
    same_head = (r2 // CHUNK) == (c2 // CHUNK)
    keep2 = (same_head & (r2 >= c2), same_head & (r2 <= c2))

    def stack2(x):
        return jnp.concatenate([jnp.where(lo_lanes, x, 0.0), jnp.where(lo_lanes, 0.0, x)], axis=0)

    def fold2(x):
        return x[:CHUNK] + x[CHUNK:]

    zeros = jnp.zeros((GLA_DV, GLA_DK), F32)
    for d in range(2):
        st_scr[d] = jnp.concatenate([jnp.concatenate([s0_ref[0, d, 0], zeros], axis=1),
                                     jnp.concatenate([zeros, s0_ref[0, d, 1]], axis=1)], axis=0)

    def body(it, carry):
        ch = []
        for j in range(unroll):
            for d in range(2):
                ci = it * unroll + j
                cc = ci if d == 0 else nc - 1 - ci
                rows = pl.ds(pl.multiple_of(cc * CHUNK, CHUNK), CHUNK)
                ch.append(dict(d=d, rows=rows, sm=sm_ref[0, rows, :], q=q_ref[0, rows, :], k=k_ref[0, rows, :],
                               v=v_ref[0, rows, :]))
        logits = [_dot(k["sm"], wg_ref[0, :, k["d"] * LANES:(k["d"] + 1) * LANES])
                  + bg_ref[0, :, k["d"] * LANES:(k["d"] + 1) * LANES] for k in ch]
        gs = [_sel_dot(tri[k["d"]], _log_sigmoid(lg) * (1.0 / GLA_TAU)) for k, lg in zip(ch, logits)]
        for k, g in zip(ch, gs):
            g_last = g[CHUNK - 1:CHUNK] if k["d"] == 0 else g[0:1]
            k["qd"] = (k["q"] * (GLA_DK ** -0.5) * jnp.exp(g)).astype(BF16)
            k["k_inv"] = k["k"] * jnp.exp(-g)
            k["k_end"] = k["k"] * jnp.exp(g_last - g)
            k["decay"] = jnp.exp(g_last)
        a_s = [jnp.where(keep2[k["d"]], _dot_nt(stack2(k["qd"].astype(F32)), stack2(k["k_inv"])), 0.0) for k in ch]
        intra = [fold2(_dot(a, stack2(k["v"]))) for k, a in zip(ch, a_s)]
        kvs = [jnp.where(same_head, _dot_tn(k["v"], k["k_end"]), 0.0) for k in ch]
        st = [st_scr[0], st_scr[1]]
        inter = []
        for k, kv in zip(ch, kvs):
            inter.append(_dot_nt(k["qd"], st[k["d"]]))
            st[k["d"]] = st[k["d"]] * k["decay"] + kv
        for k, oi, ox in zip(ch, intra, inter):
            if k["d"] == 0:
                of_scr[k["rows"], :] = oi + ox
            else:
                ob_scr[k["rows"], :] = oi + ox
        st_scr[0] = st[0]
        st_scr[1] = st[1]
        return carry

    lax.fori_loop(0, nc // unroll, body, 0)
    for d in range(2):
        st = st_scr[d]
        st_ref[0, d, 0] = st[:GLA_DV, :GLA_DK]
        st_ref[0, d, 1] = st[GLA_DV:, GLA_DK:]
    _out_norm_gate(of_scr, ob_scr, z_ref, gn_ref, o_ref, n)


def _gla(proj3, wg_p, bg_p, gn, s0t):
    b, n, _ = proj3.shape
    npair = GLA_HEADS // 2
    col = lambda base: (lambda bi, p: (bi, 0, base + p))
    s0r = _pair_major(s0t).reshape(b * npair, 2, 2, GLA_DV, GLA_DK)
    st_spec = pl.BlockSpec((1, 2, 2, GLA_DV, GLA_DK), lambda bi, p: (bi * npair + p, 0, 0, 0, 0))
    o, st = pl.pallas_call(
        functools.partial(_gla_kernel, n=n),
        grid=(b, npair),
        in_specs=[pl.BlockSpec((1, n, LANES), col(COL_GLA_Q)),
                  pl.BlockSpec((1, n, LANES), col(COL_GLA_K)),
                  pl.BlockSpec((1, n, LANES), col(COL_GLA_V)),
                  pl.BlockSpec((1, n, LANES), col(COL_GLA_Z)),
                  pl.BlockSpec((1, n, LANES), lambda bi, p: (bi, 0, COL_SMALL)),
                  pl.BlockSpec((1, LANES, 2 * LANES), lambda bi, p: (p, 0, 0)),
                  pl.BlockSpec((1, 1, 2 * LANES), lambda bi, p: (p, 0, 0)),
                  pl.BlockSpec((1, LANES), lambda bi, p: (0, 0)),
                  st_spec],
        out_specs=[pl.BlockSpec((1, n, LANES), lambda bi, p: (bi, 0, p)), st_spec],
        out_shape=[jax.ShapeDtypeStruct((b, n, GLA_W), F32),
                   jax.ShapeDtypeStruct(s0r.shape, F32)],
        scratch_shapes=[pltpu.VMEM((n, LANES), F32), pltpu.VMEM((n, LANES), F32),
                        pltpu.VMEM((2, 2 * GLA_DV, 2 * GLA_DK), F32)],
        compiler_params=_cparams(("arbitrary", "arbitrary")),
    )(proj3, proj3, proj3, proj3, proj3, wg_p, bg_p, gn, s0r)
    return o, st


def _gdn_kernel(q_ref, k_ref, v_ref, z_ref, sm_ref, cq_ref, ck_ref, cv_ref, alog_ref, dtb_ref,
                ex_ref, gn_ref, s0_ref,
                o_ref, st_ref, a_scr, b_scr, qe_scr, dec_scr, of_scr, ob_scr, st_scr, *, n):
    nc = n // CHUNK
    unroll = 4
    o_scr = (of_scr, ob_scr)
    r, c = _tri_masks()
    tri = ((r >= c).astype(F32), (r <= c).astype(F32))
    row_id = _iota2((CHUNK, LANES), 0)
    lane_id = _iota2((CHUNK, LANES), 1)
    lo_lanes = lane_id < HEAD_W
    head_ones = _head_block_ones(LANES)
    r2 = _iota2((LANES, LANES), 0)
    c2 = _iota2((LANES, LANES), 1)
    same_head = (r2 // CHUNK) == (c2 // CHUNK)
    strict2 = (same_head & (r2 > c2), same_head & (r2 < c2))
    diag2 = r2 == c2
    blk16 = (r2 // 16) == (c2 // 16)

    def conv_silu(x_ref, w_ref, r0, first, last):
        cur = x_ref[0, pl.ds(r0, CHUNK), :]
        prev_row = x_ref[0, pl.ds(jnp.maximum(r0 - 1, 0), 1), :] * jnp.where(first, 0.0, 1.0)
        next_row = x_ref[0, pl.ds(jnp.minimum(r0 + CHUNK, n - 1), 1), :] * jnp.where(last, 0.0, 1.0)
        prev = jnp.where(row_id == 0, prev_row, pltpu.roll(cur, 1, 0))
        nxt = jnp.where(row_id == CHUNK - 1, next_row, pltpu.roll(cur, CHUNK - 1, 0))
        return _silu(prev * w_ref[0:1, :] + cur * w_ref[1:2, :] + nxt * w_ref[2:3, :])

    def stack2(x):
        return jnp.concatenate([jnp.where(lo_lanes, x, 0.0), jnp.where(lo_lanes, 0.0, x)], axis=0)

    def fold2(x):
        return x[:CHUNK] + x[CHUNK:]

    def bcast2(x):
        sw = pltpu.roll(x, HEAD_W, 1)
        return jnp.concatenate([jnp.where(lo_lanes, x, sw), jnp.where(lo_lanes, sw, x)], axis=0)

    def prep(it, carry):
        chunks = []
        for j in range(unroll):
            cc = it * unroll + j
            r0 = pl.multiple_of(cc * CHUNK, CHUNK)
            first = cc == 0
            last = cc == nc - 1
            chunks.append(dict(cc=cc, rows=pl.ds(r0, CHUNK), sm=sm_ref[0, pl.ds(r0, CHUNK), :],
                               qc=conv_silu(q_ref, cq_ref, r0, first, last),
                               kc=conv_silu(k_ref, ck_ref, r0, first, last),
                               vc=conv_silu(v_ref, cv_ref, r0, first, last)))
        qss = [_dot_sel(c["qc"] * c["qc"], head_ones) for c in chunks]
        kss = [_dot_sel(c["kc"] * c["kc"], head_ones) for c in chunks]
        for c, qs, ks in zip(chunks, qss, kss):
            c["qn"] = c["qc"] * lax.rsqrt(qs + EPS) * (GDN_DK ** -0.5)
            c["kn"] = c["kc"] * lax.rsqrt(ks + EPS)
            c["k2"] = stack2(c["kn"]).astype(BF16)
            la = -jnp.exp(alog_ref[...]) * _softplus(c["sm"] + dtb_ref[...])
            c["gates"] = jnp.where(lane_id >= SM_A, la, jax.nn.sigmoid(c["sm"]))
        kks = [_dot_nt(c["k2"], c["k2"]) for c in chunks]
        qk0s = [_dot_nt(stack2(c["qn"]), c["k2"]) for c in chunks]
        bls = [_dot_sel(c["gates"], ex_ref[0]) for c in chunks]
        for c, kk, qk0, bl in zip(chunks, kks, qk0s, bls):
            c["kk"], c["qk0"], c["bl"] = kk, qk0, bl

        ch = [dict(c=c, d=d, beta_x=c["bl"][:, 2 * d * LANES:(2 * d + 1) * LANES])
              for c in chunks for d in range(2)]
        g_xs = [_sel_dot(tri[k["d"]], k["c"]["bl"][:, (2 * k["d"] + 1) * LANES:(2 * k["d"] + 2) * LANES])
                for k in ch]
        ones8 = jnp.ones((8, LANES), F32)
        g_rows = [_sel_dot_nt(ones8, jnp.concatenate([jnp.where(lane_id == 0, g, 0.0),
                                                      jnp.where(lane_id == HEAD_W, g, 0.0)], axis=0))[0:1]
                  for g in g_xs]
        for k, g_x, g_row in zip(ch, g_xs, g_rows):
            d, c = k["d"], k["c"]
            k["g_x"] = g_x
            k["g_last"] = g_x[CHUNK - 1:CHUNK] if d == 0 else g_x[0:1]
            diff = bcast2(g_x) - g_row
            k["e_strict"] = jnp.where(strict2[d], jnp.exp(jnp.where(strict2[d], diff, 0.0)), 0.0)
            m = bcast2(k["beta_x"]) * c["kk"] * k["e_strict"]
            k["md"] = jnp.where(blk16, m, 0.0)
            k["lo"] = m - k["md"]
            k["eg"] = jnp.exp(g_x)
            k["rhs"] = jnp.concatenate([stack2(c["vc"] * k["beta_x"]),
                                        stack2(c["kn"] * k["beta_x"] * k["eg"])], axis=1)
        ps = [-k["md"] for k in ch]
        nds = list(ps)
        for _ in range(3):
            ps = [_dot(p, p) for p in ps]
            ts = [_dot(nd, p) for nd, p in zip(nds, ps)]
            nds = [nd + p + t for nd, p, t in zip(nds, ps, ts)]
        ts = [_dot(nd, k["rhs"]) for nd, k in zip(nds, ch)]
        xs = [k["rhs"] + t for k, t in zip(ch, ts)]
        for _ in range(CHUNK // 16 - 1):
            ts = [_dot(k["lo"], x) for k, x in zip(ch, xs)]
            ys = [k["rhs"] - t for k, t in zip(ch, ts)]
            ts = [_dot(nd, y) for nd, y in zip(nds, ys)]
            xs = [y + t for y, t in zip(ys, ts)]
        uws = [jnp.concatenate([fold2(x[:, :LANES]), fold2(x[:, LANES:])], axis=1) for x in xs]
        k_ends = [k["c"]["kn"] * jnp.exp(k["g_last"] - k["g_x"]) for k in ch]
        abs_ = [_dot_tn(uw, ke) for uw, ke in zip(uws, k_ends)]
        qus = [_dot(k["c"]["qk0"] * jnp.where(diag2, 1.0, k["e_strict"]), x) for k, x in zip(ch, xs)]
        for k, ab, qu in zip(ch, abs_, qus):
            d, c = k["d"], k["c"]
            blk = pl.ds(pl.multiple_of(c["cc"] * LANES, LANES), LANES)
            b_scr[d, blk, :] = jnp.where(same_head, ab[:LANES], 0.0)
            a_scr[d, blk, :] = jnp.where(same_head, ab[LANES:], 0.0).astype(BF16)
            qe_scr[d, c["rows"], :] = (c["qn"] * k["eg"] - fold2(qu[:, LANES:])).astype(BF16)
            o_scr[d][c["rows"], :] = fold2(qu[:, :LANES])
            dec_scr[d, pl.ds(pl.multiple_of(c["cc"] * 8, 8), 8), :] = jnp.broadcast_to(
                jnp.exp(k["g_last"]), (8, LANES))
        return carry

    lax.fori_loop(0, nc // unroll, prep, 0)

    zeros = jnp.zeros((GDN_DV, GDN_DK), F32)
    for d in range(2):
        st_scr[d] = jnp.concatenate([jnp.concatenate([s0_ref[0, d, 0], zeros], axis=1),
                                     jnp.concatenate([zeros, s0_ref[0, d, 1]], axis=1)], axis=0)

    def step(ci, carry):
        loaded = []
        for d in range(2):
            cc = ci if d == 0 else nc - 1 - ci
            rows = pl.ds(pl.multiple_of(cc * CHUNK, CHUNK), CHUNK)
            blk = pl.ds(pl.multiple_of(cc * LANES, LANES), LANES)
            loaded.append((rows, st_scr[d], a_scr[d, blk, :], b_scr[d, blk, :], qe_scr[d, rows, :],
                           o_scr[d][rows, :], dec_scr[d, pl.ds(pl.multiple_of(cc * 8, 8), 1), :]))
        ts = [_dot(st, a) for _, st, a, _, _, _, _ in loaded]
        qs = [_dot_nt(qe, st) for _, st, _, _, qe, _, _ in loaded]
        for d, ((rows, st, _, b, _, o_part, decay), t, q) in enumerate(zip(loaded, ts, qs)):
            st_scr[d] = st * decay - t + b
            o_scr[d][rows, :] = o_part + q
        return carry

    lax.fori_loop(0, nc, step, 0)
    for d in range(2):
        st = st_scr[d]
        st_ref[0, d, 0] = st[:GDN_DV, :GDN_DK]
        st_ref[0, d, 1] = st[GDN_DV:, GDN_DK:]
    _out_norm_gate(of_scr, ob_scr, z_ref, gn_ref, o_ref, n)


def _gdn_select_constants():
    npair = GDN_HEADS // 2
    ex = np.zeros((npair, LANES, 4 * LANES), np.float32)
    for p in range(npair):
        for d in range(2):
            for h in range(2):
                hh = 2 * p + h
                lanes_b = slice(2 * d * LANES + h * HEAD_W, 2 * d * LANES + (h + 1) * HEAD_W)
                lanes_g = slice((2 * d + 1) * LANES + h * HEAD_W, (2 * d + 1) * LANES + (h + 1) * HEAD_W)
                ex[p, SM_BETA + d * GDN_HEADS + hh, lanes_b] = 1.0
                ex[p, SM_A + d * GDN_HEADS + hh, lanes_g] = 1.0
    return jnp.asarray(ex)


def _gdn(proj3, conv_w, alog_row, dtb_row, gn, s0t):
    b, n, _ = proj3.shape
    npair = GDN_HEADS // 2
    ex = _gdn_select_constants()
    nc = n // CHUNK
    col = lambda base: (lambda bi, p: (bi, 0, base + p))
    cw = GDN_HEADS * GDN_DK
    s0r = _pair_major(s0t).reshape(b * npair, 2, 2, GDN_DV, GDN_DK)
    st_spec = pl.BlockSpec((1, 2, 2, GDN_DV, GDN_DK), lambda bi, p: (bi * npair + p, 0, 0, 0, 0))
    pair_spec = lambda shp: pl.BlockSpec((1,) + shp, lambda bi, p: (p, 0, 0))
    o, st = pl.pallas_call(
        functools.partial(_gdn_kernel, n=n),
        grid=(b, npair),
        in_specs=[pl.BlockSpec((1, n, LANES), col(COL_GDN_Q)),
                  pl.BlockSpec((1, n, LANES), col(COL_GDN_K)),
                  pl.BlockSpec((1, n, LANES), col(COL_GDN_V)),
                  pl.BlockSpec((1, n, LANES), col(COL_GDN_Z)),
                  pl.BlockSpec((1, n, LANES), lambda bi, p: (bi, 0, COL_SMALL)),
                  pl.BlockSpec((3, LANES), lambda bi, p: (0, p)),
                  pl.BlockSpec((3, LANES), lambda bi, p: (0, cw // LANES + p)),
                  pl.BlockSpec((3, LANES), lambda bi, p: (0, 2 * cw // LANES + p)),
                  pl.BlockSpec((1, LANES), lambda bi, p: (0, 0)),
                  pl.BlockSpec((1, LANES), lambda bi, p: (0, 0)),
                  pair_spec((LANES, 4 * LANES)),
                  pl.BlockSpec((1, LANES), lambda bi, p: (0, 0)),
                  st_spec],
        out_specs=[pl.BlockSpec((1, n, LANES), lambda bi, p: (bi, 0, p)), st_spec],
        out_shape=[jax.ShapeDtypeStruct((b, n, GDN_W), F32),
                   jax.ShapeDtypeStruct(s0r.shape, F32)],
        scratch_shapes=[pltpu.VMEM((2, nc * LANES, LANES), BF16),
                        pltpu.VMEM((2, nc * LANES, LANES), F32),
                        pltpu.VMEM((2, n, LANES), BF16),
                        pltpu.VMEM((2, nc * 8, LANES), F32),
                        pltpu.VMEM((n, LANES), F32), pltpu.VMEM((n, LANES), F32),
                        pltpu.VMEM((2, 2 * GDN_DV, 2 * GDN_DK), F32)],
        compiler_params=_cparams(("arbitrary", "arbitrary")),
    )(proj3, proj3, proj3, proj3, proj3, conv_w, conv_w, conv_w, alog_row, dtb_row, ex, gn, s0r)
    return o, st


def _outproj_kernel(omla_ref, ogla_ref, ogdn_ref, x_ref, mod_ref, nrm_ref, wout_ref, wr_ref,
                    x1_ref, xm_ref, afft_ref):
    y = (_dot(omla_ref[...], wout_ref[0:MLA_W, :])
         + _dot(ogla_ref[...], wout_ref[MLA_W:MLA_W + GLA_W, :])
         + _dot(ogdn_ref[...], wout_ref[MLA_W + GLA_W:, :]))
    m = mod_ref[0]
    x1 = x_ref[...] + m[2:3] * _rms(y, nrm_ref[1:2, :])
    x1_ref[...] = x1
    xm2 = _rms(x1, nrm_ref[2:3, :]) * (1.0 + m[4:5]) + m[3:4]
    d = xm2.shape[1]
    xm_ref[:, :d] = xm2
    logits = _dot_x3(xm2, wr_ref[...])
    lane = _iota2(logits.shape, 1)
    logits = jnp.where(lane < N_EXPERTS, logits, -jnp.inf)
    e = jnp.exp(logits - jnp.max(logits, axis=-1, keepdims=True))
    aff = e / jnp.sum(e, axis=-1, keepdims=True)
    xm_ref[:, d:] = aff
    sel = (_iota2((N_EXPERTS, LANES), 0) == _iota2((N_EXPERTS, LANES), 1)).astype(F32)
    afft_ref[...] = _sel_dot_nt(sel, aff, pieces=3)


def _out_projection(o_mla, o_gla, o_gdn, x2d, modg, nrm, w_out, w_router_p, tiles_per_batch, tm):
    t, d = x2d.shape
    per_batch = modg.shape[0] > 1
    return pl.pallas_call(
        _outproj_kernel,
        grid=(t // tm,),
        in_specs=[pl.BlockSpec((tm, MLA_W), lambda i: (i, 0)),
                  pl.BlockSpec((tm, GLA_W), lambda i: (i, 0)),
                  pl.BlockSpec((tm, GDN_W), lambda i: (i, 0)),
                  pl.BlockSpec((tm, d), lambda i: (i, 0)),
                  pl.BlockSpec((1, 6, d), (lambda i: (i // tiles_per_batch, 0, 0)) if per_batch
                               else (lambda i: (0, 0, 0))),
                  pl.BlockSpec((4, d), lambda i: (0, 0)),
                  pl.BlockSpec((d, d), lambda i: (0, 0)),
                  pl.BlockSpec((d, LANES), lambda i: (0, 0))],
        out_specs=[pl.BlockSpec((tm, d), lambda i: (i, 0)),
                   pl.BlockSpec((tm, d + LANES), lambda i: (i, 0)),
                   pl.BlockSpec((N_EXPERTS, tm), lambda i: (0, i))],
        out_shape=[jax.ShapeDtypeStruct((t, d), F32),
                   jax.ShapeDtypeStruct((t, d + LANES), F32),
                   jax.ShapeDtypeStruct((N_EXPERTS, t), F32)],
        compiler_params=_cparams(("arbitrary",)),
    )(o_mla, o_gla, o_gdn, x2d, modg, nrm, w_out, w_router_p)


def _route_kernel(afft_ref, rank_ref, pos_ref, *, cap):
    a = afft_ref[...]
    t = a.shape[1]

    def search(i, cur):
        cand = cur | jnp.left_shift(jnp.int32(1), 30 - i)
        cnt = jnp.sum((a >= lax.bitcast_convert_type(cand, F32)).astype(F32), axis=1, keepdims=True)
        return jnp.where(cnt >= cap, cand, cur)

    kth = lax.bitcast_convert_type(lax.fori_loop(0, 31, search, jnp.zeros((a.shape[0], 1), I32)), F32)
    gt = (a > kth).astype(F32)
    eq = (a == kth).astype(F32)
    need = cap - jnp.sum(gt, axis=1, keepdims=True)
    before = (_iota2((LANES, LANES), 0) < _iota2((LANES, LANES), 1)).astype(BF16)
    nb = t // LANES
    carry_eq = jnp.zeros((a.shape[0], 1), F32)
    carry = jnp.zeros((a.shape[0], 1), F32)
    for b in range(nb):
        sl = slice(b * LANES, (b + 1) * LANES)
        eqb = eq[:, sl]
        pre = jnp.dot(eqb.astype(BF16), before, preferred_element_type=F32) + carry_eq
        selb = jnp.maximum(gt[:, sl], jnp.where(pre < need, eqb, 0.0))
        carry_eq = carry_eq + jnp.sum(eqb, axis=1, keepdims=True)
        local = jnp.dot(selb.astype(BF16), before, preferred_element_type=F32)
        rank_ref[:, sl] = jnp.where(selb > 0.0, local, -1.0)
        pos_ref[:, sl] = local + carry
        carry = carry + jnp.sum(selb, axis=1, keepdims=True)


def _route(afft, cap):
    e, t = afft.shape
    return pl.pallas_call(
        functools.partial(_route_kernel, cap=cap),
        out_shape=[jax.ShapeDtypeStruct((e, t), F32), jax.ShapeDtypeStruct((e, t), F32)],
        compiler_params=_cparams(),
    )(afft)


def _compact_kernel(off_ref, rank_ref, idx_ref, *, nb):
    e = pl.program_id(0)
    idx_ref[...] = jnp.zeros(idx_ref.shape, I32)
    slot = _iota2((LANES, LANES), 0)
    tok = _iota2((LANES, LANES), 0).astype(BF16)
    group = 4

    def blks(g, carry):
        bs = [g * group + j for j in range(group)]
        onehots = [(rank_ref[0, pl.ds(b, 1), :].astype(I32) == slot).astype(BF16) for b in bs]
        ids = [jnp.dot(oh, tok, preferred_element_type=F32).astype(I32) + b * LANES for oh, b in zip(onehots, bs)]
        for b, v in zip(bs, ids):
            idx_ref[0, pl.ds(off_ref[e, b], LANES), :] = v
        return carry

    lax.fori_loop(0, nb // group, blks, 0)


def _compact(off, rank, cap):
    e, t = rank.shape
    nb = t // LANES
    rows = cap + LANES
    idx = pl.pallas_call(
        functools.partial(_compact_kernel, nb=nb),
        grid_spec=pltpu.PrefetchScalarGridSpec(
            num_scalar_prefetch=1,
            grid=(e,),
            in_specs=[pl.BlockSpec((1, nb, LANES), lambda ei, off_r: (ei, 0, 0))],
            out_specs=pl.BlockSpec((1, rows, LANES), lambda ei, off_r: (ei, 0, 0))),
        out_shape=jax.ShapeDtypeStruct((e, rows, LANES), I32),
        compiler_params=_cparams(("arbitrary",)),
    )(off, rank.reshape(e, nb, LANES))
    return idx[:, :cap, 0]


def _gather_copy(x_hbm, xe_scr, sem, slot, src_row, dst_row, nrows):
    return pltpu.make_async_copy(x_hbm.at[pl.ds(src_row, nrows)], xe_scr.at[slot, pl.ds(dst_row, nrows)],
                                 sem.at[slot])


def _expert_kernel(idx_ref, x_hbm, w1_ref, w3_ref, w2_ref, ye_ref, xe_scr, xb_scr, sem, *, cap):
    e = pl.program_id(0)
    ne = pl.num_programs(0)

    slot = e % 2
    nxt = (e + 1) % ne
    nxt_slot = 1 - slot

    @pl.when(e == 0)
    def _():
        def row(s, carry):
            _gather_copy(x_hbm, xe_scr, sem, 0, idx_ref[0, s], s, 1).start()
            return carry
        lax.fori_loop(0, cap, row, 0)

    _gather_copy(x_hbm, xe_scr, sem, slot, 0, 0, cap).wait()
    d = xb_scr.shape[1]
    xb_scr[...] = xe_scr[slot, :, :d].astype(BF16)
    ge = xe_scr[slot, :, d:]
    gate = jnp.sum(jnp.where(_iota2(ge.shape, 1) == e, ge, 0.0), axis=-1, keepdims=True)
    ff = w1_ref.shape[2]
    fchunk = 512
    ndots = 3 * (ff // fchunk)
    per_dot = -(-cap // ndots)

    def prefetch(i):
        for s in range(i * per_dot, min((i + 1) * per_dot, cap)):
            _gather_copy(x_hbm, xe_scr, sem, nxt_slot, idx_ref[nxt, s], s, 1).start()

    y = jnp.zeros((cap, w2_ref.shape[2]), F32)
    for j in range(ff // fchunk):
        fs = slice(j * fchunk, (j + 1) * fchunk)
        prefetch(3 * j)
        h1 = _dot(xb_scr[...], w1_ref[0, :, fs])
        prefetch(3 * j + 1)
        h3 = _dot(xb_scr[...], w3_ref[0, :, fs])
        prefetch(3 * j + 2)
        y = y + _dot(_silu(h1) * h3, w2_ref[0, fs, :])
    ye_ref[0] = y * gate

    @pl.when(e == ne - 1)
    def _():
        _gather_copy(x_hbm, xe_scr, sem, nxt_slot, 0, 0, cap).wait()


def _expert_ffn(idx, xm, w1, w3, w2, cap):
    e, d, ff = w1.shape
    wspec = lambda shp: pl.BlockSpec((1,) + shp, lambda ei, idx_r: (ei, 0, 0))
    return pl.pallas_call(
        functools.partial(_expert_kernel, cap=cap),
        grid_spec=pltpu.PrefetchScalarGridSpec(
            num_scalar_prefetch=1,
            grid=(e,),
            in_specs=[pl.BlockSpec(memory_space=pl.ANY),
                      wspec((d, ff)), wspec((d, ff)), wspec((ff, d))],
            out_specs=pl.BlockSpec((1, cap, d), lambda ei, idx_r: (ei, 0, 0)),
            scratch_shapes=[pltpu.VMEM((2, cap, xm.shape[1]), F32), pltpu.VMEM((cap, d), BF16),
                            pltpu.SemaphoreType.DMA((2,))]),
        out_shape=jax.ShapeDtypeStruct((e, cap, d), F32),
        compiler_params=_cparams(("arbitrary",)),
    )(idx, xm, w1, w3, w2)


def _scatter_kernel(idx_ref, lo_ref, ye_ref, o_ref, *, cap, nparts, part_rows):
    part = pl.program_id(0)
    e = pl.program_id(1)

    @pl.when(e == 0)
    def _():
        o_ref[...] = jnp.zeros(o_ref.shape, F32)

    base = part * part_rows

    lo = lo_ref[e, part]
    hi = lo_ref[e, part + 1]
    group = 8
    ngroups = (hi - lo) // group

    def add_group(g, carry):
        s0 = lo + g * group
        ts = [idx_ref[e, s0 + j] - base for j in range(group)]
        new = [o_ref[pl.ds(t, 1), :] + ye_ref[0, pl.ds(s0 + j, 1), :] for j, t in enumerate(ts)]
        for t, v in zip(ts, new):
            o_ref[pl.ds(t, 1), :] = v
        return carry

    lax.fori_loop(0, ngroups, add_group, 0)

    def add(s, carry):
        t = idx_ref[e, s] - base
        o_ref[pl.ds(t, 1), :] = o_ref[pl.ds(t, 1), :] + ye_ref[0, pl.ds(s, 1), :]
        return carry

    lax.fori_loop(lo + ngroups * group, hi, add, 0)


def _scatter(idx, bounds, ye, t, nparts):
    e, cap, d = ye.shape
    part_rows = t // nparts
    return pl.pallas_call(
        functools.partial(_scatter_kernel, cap=cap, nparts=nparts, part_rows=part_rows),
        grid_spec=pltpu.PrefetchScalarGridSpec(
            num_scalar_prefetch=2,
            grid=(nparts, e),
            in_specs=[pl.BlockSpec((1, cap, d), lambda pi, ei, a, b: (ei, 0, 0))],
            out_specs=pl.BlockSpec((part_rows, d), lambda pi, ei, a, b: (pi, 0))),
        out_shape=jax.ShapeDtypeStruct((t, d), F32),
        compiler_params=_cparams(("arbitrary", "arbitrary")),
    )(idx, bounds, ye)


def _residual_kernel(x1_ref, moe_ref, mod_ref, nrm_ref, o_ref):
    m = mod_ref[0]
    o_ref[...] = x1_ref[...] + m[5:6] * _rms(moe_ref[...], nrm_ref[3:4, :])


def _ffn_residual(x1, moe, modg, nrm, tiles_per_batch, tm):
    t, d = x1.shape
    per_batch = modg.shape[0] > 1
    return pl.pallas_call(
        _residual_kernel,
        grid=(t // tm,),
        in_specs=[pl.BlockSpec((tm, d), lambda i: (i, 0)),
                  pl.BlockSpec((tm, d), lambda i: (i, 0)),
                  pl.BlockSpec((1, 6, d), (lambda i: (i // tiles_per_batch, 0, 0)) if per_batch
                               else (lambda i: (0, 0, 0))),
                  pl.BlockSpec((4, d), lambda i: (0, 0))],
        out_specs=pl.BlockSpec((tm, d), lambda i: (i, 0)),
        out_shape=jax.ShapeDtypeStruct((t, d), F32),
        compiler_params=_cparams(("arbitrary",)),
    )(x1, moe, modg, nrm)


def _rot_cols(w):
    s = w.shape[:-1]
    w2 = w.reshape(s + (ROPE_DIM // 2, 2))
    return jnp.stack([-w2[..., 1], w2[..., 0]], axis=-1).reshape(s + (ROPE_DIM,))


def _layer_weights(l, w_in, mla_q_norm, mla_wq_b, mla_kv_norm, mla_wkv_b, gla_wg, gla_bg, gla_out_norm,
                   gdn_conv, gdn_a_log, gdn_dt_bias, gdn_out_norm, w_out, sandwich_norms, w_router,
                   w_e1, w_e3, w_e2):
    wi = w_in[l]
    o = 0
    parts = {}
    for name, sz in (("q_lat", Q_LORA), ("kv_lat", KV_LORA), ("k_rope", ROPE_DIM),
                     ("gla_q", GLA_W), ("gla_k", GLA_W), ("gla_v", GLA_W), ("gla_glr", 2 * GLA_GATE_RANK),
                     ("gla_z", GLA_W), ("gdn_q", GDN_W), ("gdn_k", GDN_W), ("gdn_v", GDN_W), ("gdn_z", GDN_W),
                     ("gdn_b", 2 * GDN_HEADS), ("gdn_a", 2 * GDN_HEADS)):
        parts[name] = wi[:, o:o + sz]
        o += sz
    w_in_p = jnp.concatenate(
        [parts[k] for k in ("q_lat", "kv_lat", "gla_q", "gla_k", "gla_v", "gla_z",
                            "gdn_q", "gdn_k", "gdn_v", "gdn_z")]
        + [parts["k_rope"], _rot_cols(parts["k_rope"]), parts["gla_glr"], parts["gdn_b"], parts["gdn_a"]],
        axis=1).astype(BF16)

    wq = mla_wq_b[l].reshape(Q_LORA, MLA_HEADS, NOPE_DIM + ROPE_DIM)
    wq_ext = jnp.concatenate([wq, _rot_cols(wq[..., NOPE_DIM:])], axis=-1).reshape(Q_LORA, -1).astype(BF16)
    wkv = mla_wkv_b[l].reshape(KV_LORA, MLA_HEADS, NOPE_DIM + MLA_V_DIM)
    wkv_ext = jnp.concatenate([wkv[..., :NOPE_DIM].reshape(KV_LORA, -1),
                               wkv[..., NOPE_DIM:].reshape(KV_LORA, -1)], axis=1).astype(BF16)

    npair = GLA_HEADS // 2
    wg_p = jnp.zeros((npair, LANES, 2 * LANES), F32)
    bg_p = jnp.zeros((npair, 1, 2 * LANES), F32)
    for p in range(npair):
        for d in range(2):
            rows = slice(SM_GLR + d * GLA_GATE_RANK, SM_GLR + (d + 1) * GLA_GATE_RANK)
            wg_p = wg_p.at[p, rows, d * LANES:(d + 1) * LANES].set(gla_wg[l, d][:, p * LANES:(p + 1) * LANES])
            bg_p = bg_p.at[p, 0, d * LANES:(d + 1) * LANES].set(gla_bg[l, d][p * LANES:(p + 1) * LANES])

    alog_row = jnp.zeros((1, LANES), F32).at[0, SM_A:].set(gdn_a_log[l].reshape(-1))
    dtb_row = jnp.zeros((1, LANES), F32).at[0, SM_A:].set(gdn_dt_bias[l].reshape(-1))
    w_router_p = jnp.zeros((D_MODEL, LANES), F32).at[:, :N_EXPERTS].set(w_router[l])
    return dict(
        w_in=w_in_p, gq=mla_q_norm[l][None], wq=wq_ext, gkv=mla_kv_norm[l][None], wkv=wkv_ext,
        wg=wg_p.astype(BF16), bg=bg_p, gla_gn=jnp.tile(gla_out_norm[l], 2)[None],
        conv=gdn_conv[l], alog=alog_row, dtb=dtb_row, gdn_gn=jnp.tile(gdn_out_norm[l], 2)[None],
        w_out=w_out[l].astype(BF16), nrm=sandwich_norms[l], w_router=w_router_p,
        w1=w_e1[l].astype(BF16), w3=w_e3[l].astype(BF16), w2=w_e2[l].astype(BF16))


def _rope_tables(n, rotate):
    ones = np.ones((1, NOPE_DIM), np.float32)
    if not rotate:
        t1 = np.concatenate([ones, np.ones((1, ROPE_DIM), np.float32), np.zeros((1, ROPE_DIM), np.float32)], 1)
        z = np.zeros_like(t1)
        return tuple(jnp.asarray(np.broadcast_to(t, (n, LANES)).copy()) for t in (t1, z, z))
    rows = n // GRID_W
    row = jnp.repeat(jnp.arange(rows), GRID_W).astype(F32)
    col = jnp.tile(jnp.arange(GRID_W), rows).astype(F32)
    half = ROPE_DIM // 2
    freqs = ROPE_BASE ** (-jnp.arange(0, half, 2, dtype=F32) / half)
    ang = jnp.concatenate([row[:, None] * freqs, col[:, None] * freqs], axis=-1)
    cos = jnp.repeat(jnp.cos(ang), 2, axis=-1)
    sin = jnp.repeat(jnp.sin(ang), 2, axis=-1)
    z32 = jnp.zeros((n, ROPE_DIM), F32)
    t1 = jnp.concatenate([jnp.ones((n, NOPE_DIM), F32), cos, z32], axis=1)
    t2 = jnp.concatenate([jnp.zeros((n, NOPE_DIM), F32), sin, z32], axis=1)
    t3 = jnp.concatenate([jnp.zeros((n, NOPE_DIM), F32), z32, jnp.ones((n, ROPE_DIM), F32)], axis=1)
    return t1, t2, t3


def _trunk_layer(x, lw, modg, ctx_cache, tabs):
    b, n, d = x.shape
    t = b * n
    tm = min(512, n)
    tiles_per_batch = n // tm
    x2d = x.reshape(t, d)

    proj = _in_projection(x2d, modg, lw["nrm"][0:1], lw["w_in"], tiles_per_batch, tm)
    proj3 = proj.reshape(b, n, PROJ_W)

    q, k, v, ckv, k_rope = _mla_prep(proj, tabs, lw["gq"], lw["wq"], lw["gkv"], lw["wkv"], tiles_per_batch, tm)
    qk_w = MLA_HEADS * MLA_QK_PAD
    if ctx_cache is None:
        kc = vc = None
        s0_gla = jnp.zeros((b, 2, GLA_HEADS, GLA_DV, GLA_DK), F32)
        s0_gdn = jnp.zeros((b, 2, GDN_HEADS, GDN_DV, GDN_DK), F32)
    else:
        ckv_c, kr_c, s0_gla, s0_gdn = ctx_cache
        lc = ckv_c.shape[1]
        kc, vc = _ctx_keys(ckv_c.reshape(b * lc, KV_LORA), kr_c.reshape(b * lc, ROPE_DIM), lw["wkv"])
        kc = kc.reshape(b, lc, qk_w)
        vc = vc.reshape(b, lc, MLA_W)
        s0_gla = jnp.swapaxes(s0_gla, -1, -2)
        s0_gdn = jnp.swapaxes(s0_gdn, -1, -2)
    o_mla = _attention(q.reshape(b, n, qk_w), k.reshape(b, n, qk_w), v.reshape(b, n, MLA_W), kc, vc,
                       tq=min(256, n))

    o_gla, st_gla = _gla(proj3, lw["wg"], lw["bg"], lw["gla_gn"], s0_gla)
    o_gdn, st_gdn = _gdn(proj3, lw["conv"], lw["alog"], lw["dtb"], lw["gdn_gn"], s0_gdn)

    x1, xm2, afft = _out_projection(o_mla.reshape(t, MLA_W), o_gla.reshape(t, GLA_W),
                                        o_gdn.reshape(t, GDN_W), x2d, modg, lw["nrm"], lw["w_out"],
                                        lw["w_router"], tiles_per_batch, tm)

    cap = EC_CAPACITY * t // N_EXPERTS
    rank, pos = _route(afft, cap)
    off = pos[:, ::LANES].astype(I32)
    idx = _compact(off, rank, cap)
    ye = _expert_ffn(idx, xm2, lw["w1"], lw["w3"], lw["w2"], cap)
    nparts = 2
    part_blocks = t // nparts // LANES
    bounds = jnp.concatenate([off[:, ::part_blocks], jnp.full((N_EXPERTS, 1), cap, I32)], axis=1)
    moe = _scatter(idx, bounds, ye, t, nparts)
    x2 = _ffn_residual(x1, moe, modg, lw["nrm"], tiles_per_batch, tm)

    new = None
    if ctx_cache is None:
        st_gla = jnp.swapaxes(st_gla.reshape(b, GLA_HEADS // 2, 2, 2, GLA_DV, GLA_DK), 1, 2)
        st_gdn = jnp.swapaxes(st_gdn.reshape(b, GDN_HEADS // 2, 2, 2, GDN_DV, GDN_DK), 1, 2)
        new = (ckv.reshape(b, n, KV_LORA), k_rope.reshape(b, n, ROPE_DIM),
               jnp.swapaxes(st_gla.reshape(b, 2, GLA_HEADS, GLA_DV, GLA_DK), -1, -2),
               jnp.swapaxes(st_gdn.reshape(b, 2, GDN_HEADS, GDN_DV, GDN_DK), -1, -2))
    return x2.reshape(b, n, d), new


def _pair_major(s0):
    b, two, h = s0.shape[:3]
    return jnp.swapaxes(s0.reshape(b, 2, h // 2, 2, s0.shape[-2], s0.shape[-1]), 1, 2)


def kernel(x_prompt, x_sample, cache_ckv, cache_krope, state_gla, state_gdn, c, c_ctx, w_in, mla_q_norm,
           mla_wq_b, mla_kv_norm, mla_wkv_b, gla_wg, gla_bg, gla_out_norm, gdn_conv, gdn_a_log, gdn_dt_bias,
           gdn_out_norm, w_out, w_mod, b_mod, sandwich_norms, w_router, w_e1, w_e3, w_e2):
    depth = w_in.shape[0]
    nb_s = x_sample.shape[0]
    cond = jnp.zeros((8, D_MODEL), F32).at[0].set(c_ctx).at[1:1 + nb_s].set(c)
    mod = _modulation(cond, w_mod, b_mod)

    tabs_ctx = _rope_tables(8, rotate=False)
    tabs_smp = _rope_tables(x_sample.shape[1], rotate=True)
    tabs_ctx = tuple(jnp.broadcast_to(t[:1], (min(512, x_prompt.shape[1]), LANES)) for t in tabs_ctx)

    lws = [_layer_weights(l, w_in, mla_q_norm, mla_wq_b, mla_kv_norm, mla_wkv_b, gla_wg, gla_bg, gla_out_norm,
                          gdn_conv, gdn_a_log, gdn_dt_bias, gdn_out_norm, w_out, sandwich_norms, w_router,
                          w_e1, w_e3, w_e2) for l in range(depth)]

    xp = x_prompt
    ckv_l, kr_l, sg_l, sd_l = [], [], [], []
    for l in range(depth):
        modg = mod[l, 0:1].reshape(1, 6, D_MODEL)
        xp, (ckv, kr, sg, sd) = _trunk_layer(xp, lws[l], modg, None, tabs_ctx)
        ckv_l.append(ckv)
        kr_l.append(kr)
        sg_l.append(sg)
        sd_l.append(sd)

    xs = x_sample
    for l in range(depth):
        modg = mod[l, 1:1 + nb_s].reshape(nb_s, 6, D_MODEL)
        xs, _ = _trunk_layer(xs, lws[l], modg,
                             (cache_ckv[:, l], cache_krope[:, l], state_gla[:, l], state_gdn[:, l]), tabs_smp)

    return (xp, xs, jnp.stack(ckv_l, axis=1), jnp.stack(kr_l, axis=1),
            jnp.stack(sg_l, axis=1), jnp.stack(sd_l, axis=1))
```

```python
import functools

import numpy as np
import jax
import jax.numpy as jnp
from jax import lax
from jax.experimental import pallas as pl
from jax.experimental.pallas import tpu as pltpu

F32 = jnp.float32
BF16 = jnp.bfloat16
I32 = jnp.int32

D_MODEL = 1024
DEPTH = 2
GRID_W = 64
EPS = 1e-6
CHUNK = 64

MLA_HEADS = 4
Q_LORA = 256
KV_LORA = 128
NOPE_DIM = 64
ROPE_DIM = 32
MLA_V_DIM = 64
ROPE_BASE = 10000.0
MLA_SCALE = (NOPE_DIM + ROPE_DIM) ** -0.5
MLA_QK_PAD = 128

GLA_HEADS = 4
GLA_DK = 64
GLA_DV = 64
GLA_GATE_RANK = 16
GLA_TAU = 16.0

GDN_HEADS = 8
GDN_DK = 64
GDN_DV = 64

N_EXPERTS = 16
EXPERT_FF = 1024
EC_CAPACITY = 2

LANES = 128
HEAD_W = 64
MLA_W = MLA_HEADS * MLA_V_DIM
GLA_W = GLA_HEADS * GLA_DV
GDN_W = GDN_HEADS * GDN_DV

COL_QLAT = 0
COL_KVLAT = 2
COL_GLA_Q = 3
COL_GLA_K = 5
COL_GLA_V = 7
COL_GLA_Z = 9
COL_GDN_Q = 11
COL_GDN_K = 15
COL_GDN_V = 19
COL_GDN_Z = 23
COL_SMALL = 27
PROJ_W = 28 * LANES
SM_KROPE = 0
SM_KROT = 32
SM_GLR = 64
SM_BETA = 96
SM_A = 112

VMEM_LIMIT = 56 * 1024 * 1024


def _cparams(sem=None, **kw):
    return pltpu.CompilerParams(dimension_semantics=sem, vmem_limit_bytes=VMEM_LIMIT, **kw)


def _dot(a, b):
    return jnp.dot(a.astype(BF16), b.astype(BF16), preferred_element_type=F32)


def _dot_nt(a, b):
    return lax.dot_general(a.astype(BF16), b.astype(BF16), (((1,), (1,)), ((), ())),
                           preferred_element_type=F32)


def _dot_tn(a, b):
    return lax.dot_general(a.astype(BF16), b.astype(BF16), (((0,), (0,)), ((), ())),
                           preferred_element_type=F32)


def _dot_x3(a, b):
    a1 = a.astype(BF16)
    a2 = (a - a1.astype(F32)).astype(BF16)
    b1 = b.astype(BF16)
    b2 = (b - b1.astype(F32)).astype(BF16)
    return (jnp.dot(a1, b1, preferred_element_type=F32) + jnp.dot(a1, b2, preferred_element_type=F32)
            + jnp.dot(a2, b1, preferred_element_type=F32))


def _split(x, pieces):
    out = []
    for _ in range(pieces):
        p = x.astype(BF16)
        out.append(p)
        x = x - p.astype(F32)
    return out


def _dot_sel(x, sel, pieces=2):
    s = sel.astype(BF16)
    return sum(jnp.dot(p, s, preferred_element_type=F32) for p in _split(x, pieces))


def _sel_dot(sel, x, pieces=2):
    s = sel.astype(BF16)
    return sum(jnp.dot(s, p, preferred_element_type=F32) for p in _split(x, pieces))


def _sel_dot_nt(sel, x, pieces=2):
    s = sel.astype(BF16)
    dn = (((1,), (1,)), ((), ()))
    return sum(lax.dot_general(s, p, dn, preferred_element_type=F32) for p in _split(x, pieces))


def _silu(x):
    return x * jax.nn.sigmoid(x)


def _softplus(x):
    return jnp.maximum(x, 0.0) + jnp.log(1.0 + jnp.exp(-jnp.abs(x)))


def _log_sigmoid(x):
    return -_softplus(-x)


def _rms(x, g):
    return x * lax.rsqrt(jnp.mean(x * x, axis=-1, keepdims=True) + EPS) * g


def _iota2(shape, dim):
    return lax.broadcasted_iota(I32, shape, dim)


def _head_block_ones(width):
    r = _iota2((width, width), 0) // HEAD_W
    c = _iota2((width, width), 1) // HEAD_W
    return (r == c).astype(F32)


def _mod_kernel(cond_ref, w_ref, b_ref, o_ref):
    s = _silu(cond_ref[...])
    o_ref[0] = _dot(s, w_ref[0]) + b_ref[0]


def _modulation(cond, w_mod, b_mod):
    depth, d, d6 = w_mod.shape
    nj = d6 // d
    return pl.pallas_call(
        _mod_kernel,
        grid=(depth, nj),
        in_specs=[pl.BlockSpec((8, d), lambda l, j: (0, 0)),
                  pl.BlockSpec((1, d, d), lambda l, j: (l, 0, j)),
                  pl.BlockSpec((1, 1, d), lambda l, j: (l, 0, j))],
        out_specs=pl.BlockSpec((1, 8, d), lambda l, j: (l, 0, j)),
        out_shape=jax.ShapeDtypeStruct((depth, 8, d6), F32),
        compiler_params=_cparams(("arbitrary", "arbitrary")),
    )(cond, w_mod, b_mod.reshape(depth, 1, d6))


def _inproj_kernel(x_ref, mod_ref, nrm_ref, w_ref, o_ref, *, col_chunk):
    x = x_ref[...]
    m = mod_ref[0]
    xm = (_rms(x, nrm_ref[...]) * (1.0 + m[1:2]) + m[0:1]).astype(BF16)
    for j in range(o_ref.shape[1] // col_chunk):
        sl = slice(j * col_chunk, (j + 1) * col_chunk)
        o_ref[:, sl] = jnp.dot(xm, w_ref[:, sl], preferred_element_type=F32)


def _in_projection(x2d, modg, nrm0, w_in_p, tiles_per_batch, tm):
    t, d = x2d.shape
    pw = w_in_p.shape[1]
    per_batch = modg.shape[0] > 1
    return pl.pallas_call(
        functools.partial(_inproj_kernel, col_chunk=512),
        grid=(t // tm,),
        in_specs=[pl.BlockSpec((tm, d), lambda i: (i, 0)),
                  pl.BlockSpec((1, 6, d), (lambda i: (i // tiles_per_batch, 0, 0)) if per_batch
                               else (lambda i: (0, 0, 0))),
                  pl.BlockSpec((1, d), lambda i: (0, 0)),
                  pl.BlockSpec((d, pw), lambda i: (0, 0))],
        out_specs=pl.BlockSpec((tm, pw), lambda i: (i, 0)),
        out_shape=jax.ShapeDtypeStruct((t, pw), F32),
        compiler_params=_cparams(("arbitrary",)),
    )(x2d, modg, nrm0, w_in_p)


def _assemble_keys(k_nope, k_mid, k_last):
    parts = []
    for h in range(MLA_HEADS):
        parts += [k_nope[:, h * NOPE_DIM:(h + 1) * NOPE_DIM], k_mid, k_last]
    return jnp.concatenate(parts, axis=-1)


def _mla_prep_kernel(qlat_ref, kvlat_ref, sm_ref, t1_ref, t2_ref, t3_ref, gq_ref, wq_ref, gkv_ref, wkv_ref,
                     q_ref, k_ref, v_ref, ckv_ref, kr_ref):
    t1 = t1_ref[...]
    t2 = t2_ref[...]
    t3 = t3_ref[...]
    qn = _rms(qlat_ref[...], gq_ref[...])
    raw = _dot(qn, wq_ref[...])
    w = raw.shape[1]
    tile = lambda t: jnp.concatenate([t] * MLA_HEADS, axis=-1)
    q = raw * tile(t1) + pltpu.roll(raw, w - ROPE_DIM, 1) * tile(t2) + pltpu.roll(raw, ROPE_DIM, 1) * tile(t3)
    q_ref[...] = (q * MLA_SCALE).astype(q_ref.dtype)

    ckv = _rms(kvlat_ref[...], gkv_ref[...])
    ckv_ref[...] = ckv
    kv = _dot(ckv, wkv_ref[...])
    sm = sm_ref[...]
    k_rope = sm[:, SM_KROPE:SM_KROPE + ROPE_DIM]
    k_rot = sm[:, SM_KROT:SM_KROT + ROPE_DIM]
    kr_ref[...] = k_rope
    cs = slice(NOPE_DIM, NOPE_DIM + ROPE_DIM)
    kr = k_rope * t1[:, cs] + k_rot * t2[:, cs]
    k = _assemble_keys(kv[:, :MLA_HEADS * NOPE_DIM], kr, jnp.zeros_like(kr))
    k_ref[...] = k.astype(k_ref.dtype)
    v_ref[...] = kv[:, MLA_HEADS * NOPE_DIM:].astype(v_ref.dtype)


def _mla_prep(proj, tabs, gq, wq_ext, gkv, wkv_ext, tiles_per_seq, tm):
    t = proj.shape[0]
    t1, t2, t3 = tabs
    per_pos = t1.shape[0] > tm
    tab_map = (lambda i: (i % tiles_per_seq, 0)) if per_pos else (lambda i: (0, 0))
    qk_w = MLA_HEADS * MLA_QK_PAD
    return pl.pallas_call(
        _mla_prep_kernel,
        grid=(t // tm,),
        in_specs=[pl.BlockSpec((tm, Q_LORA), lambda i: (i, COL_QLAT * LANES // Q_LORA)),
                  pl.BlockSpec((tm, KV_LORA), lambda i: (i, COL_KVLAT)),
                  pl.BlockSpec((tm, LANES), lambda i: (i, COL_SMALL)),
                  pl.BlockSpec((tm, LANES), tab_map),
                  pl.BlockSpec((tm, LANES), tab_map),
                  pl.BlockSpec((tm, LANES), tab_map),
                  pl.BlockSpec((1, Q_LORA), lambda i: (0, 0)),
                  pl.BlockSpec((Q_LORA, qk_w), lambda i: (0, 0)),
                  pl.BlockSpec((1, KV_LORA), lambda i: (0, 0)),
                  pl.BlockSpec((KV_LORA, 2 * MLA_W), lambda i: (0, 0))],
        out_specs=[pl.BlockSpec((tm, qk_w), lambda i: (i, 0)),
                   pl.BlockSpec((tm, qk_w), lambda i: (i, 0)),
                   pl.BlockSpec((tm, MLA_W), lambda i: (i, 0)),
                   pl.BlockSpec((tm, KV_LORA), lambda i: (i, 0)),
                   pl.BlockSpec((tm, ROPE_DIM), lambda i: (i, 0))],
        out_shape=[jax.ShapeDtypeStruct((t, qk_w), BF16),
                   jax.ShapeDtypeStruct((t, qk_w), BF16),
                   jax.ShapeDtypeStruct((t, MLA_W), BF16),
                   jax.ShapeDtypeStruct((t, KV_LORA), F32),
                   jax.ShapeDtypeStruct((t, ROPE_DIM), F32)],
        compiler_params=_cparams(("arbitrary",)),
    )(proj, proj, proj, t1, t2, t3, gq, wq_ext, gkv, wkv_ext)


def _ctxkeys_kernel(ckv_ref, kr_ref, wkv_ref, k_ref, v_ref):
    kv = _dot(ckv_ref[...], wkv_ref[...])
    kr = kr_ref[...]
    k = _assemble_keys(kv[:, :MLA_HEADS * NOPE_DIM], jnp.zeros_like(kr), kr)
    k_ref[...] = k.astype(k_ref.dtype)
    v_ref[...] = kv[:, MLA_HEADS * NOPE_DIM:].astype(v_ref.dtype)


def _ctx_keys(ckv_c, kr_c, wkv_ext):
    t = ckv_c.shape[0]
    qk_w = MLA_HEADS * MLA_QK_PAD
    return pl.pallas_call(
        _ctxkeys_kernel,
        out_shape=[jax.ShapeDtypeStruct((t, qk_w), BF16), jax.ShapeDtypeStruct((t, MLA_W), BF16)],
        compiler_params=_cparams(),
    )(ckv_c, kr_c, wkv_ext)


def _attn_kernel(*refs, has_ctx):
    if has_ctx:
        q_ref, k_ref, v_ref, kc_ref, vc_ref, o_ref = refs
    else:
        q_ref, k_ref, v_ref, o_ref = refs
    outs = []
    for h in range(MLA_HEADS):
        qs = slice(h * MLA_QK_PAD, (h + 1) * MLA_QK_PAD)
        vs = slice(h * MLA_V_DIM, (h + 1) * MLA_V_DIM)
        q = q_ref[0, :, qs]
        s = _dot_nt(q, k_ref[0, :, qs])
        m = jnp.max(s, axis=-1, keepdims=True)
        if has_ctx:
            sc = _dot_nt(q, kc_ref[0, :, qs])
            m = jnp.maximum(m, jnp.max(sc, axis=-1, keepdims=True))
        p = jnp.exp(s - m)
        den = jnp.sum(p, axis=-1, keepdims=True)
        acc = _dot(p, v_ref[0, :, vs])
        if has_ctx:
            pc = jnp.exp(sc - m)
            den = den + jnp.sum(pc, axis=-1, keepdims=True)
            acc = acc + _dot(pc, vc_ref[0, :, vs])
        outs.append(acc / den)
    o_ref[0] = jnp.concatenate(outs, axis=-1)


def _attention(q, k, v, kc, vc, tq):
    b, n, qk_w = q.shape
    has_ctx = kc is not None
    in_specs = [pl.BlockSpec((1, tq, qk_w), lambda bi, i: (bi, i, 0)),
                pl.BlockSpec((1, n, qk_w), lambda bi, i: (bi, 0, 0)),
                pl.BlockSpec((1, n, MLA_W), lambda bi, i: (bi, 0, 0))]
    args = [q, k, v]
    if has_ctx:
        lc = kc.shape[1]
        in_specs += [pl.BlockSpec((1, lc, qk_w), lambda bi, i: (bi, 0, 0)),
                     pl.BlockSpec((1, lc, MLA_W), lambda bi, i: (bi, 0, 0))]
        args += [kc, vc]
    return pl.pallas_call(
        functools.partial(_attn_kernel, has_ctx=has_ctx),
        grid=(b, n // tq),
        in_specs=in_specs,
        out_specs=pl.BlockSpec((1, tq, MLA_W), lambda bi, i: (bi, i, 0)),
        out_shape=jax.ShapeDtypeStruct((b, n, MLA_W), F32),
        compiler_params=_cparams(("arbitrary", "arbitrary")),
    )(*args)


def _tri_masks():
    r = _iota2((CHUNK, CHUNK), 0)
    c = _iota2((CHUNK, CHUNK), 1)
    return r, c


def _out_norm_gate(o_f_ref, o_b_ref, z_ref, gn_ref, o_ref, n):
    bd = _head_block_ones(LANES) * (1.0 / HEAD_W)
    rows = min(n, 512)

    def blk(i, carry):
        r0 = pl.multiple_of(i * rows, rows)
        o = o_f_ref[pl.ds(r0, rows), :] + o_b_ref[pl.ds(r0, rows), :]
        ms = _dot_sel(o * o, bd)
        z = z_ref[0, pl.ds(r0, rows), :]
        o_ref[0, pl.ds(r0, rows), :] = o * lax.rsqrt(ms + EPS) * gn_ref[...] * _silu(z)
        return carry

    lax.fori_loop(0, n // rows, blk, 0)


def _gla_kernel(q_ref, k_ref, v_ref, z_ref, sm_ref, wg_ref, bg_ref, gn_ref, s0_ref,
                o_ref, st_ref, of_scr, ob_scr, st_scr, *, n):
    nc = n // CHUNK
    unroll = 4
    r, c = _tri_masks()
    tri = ((r >= c).astype(F32), (r <= c).astype(F32))
    lo_lanes = _iota2((CHUNK, LANES), 1) < HEAD_W
    r2 = _iota2((LANES, LANES), 0)
    c2 = _iota2((LANES, LANES), 1)
    same_head = (r2 // CHUNK) == (c2 // CHUNK)
    keep2 = (same_head & (r2 >= c2), same_head & (r2 <= c2))

    def stack2(x):
        return jnp.concatenate([jnp.where(lo_lanes, x, 0.0), jnp.where(lo_lanes, 0.0, x)], axis=0)

    def fold2(x):
        return x[:CHUNK] + x[CHUNK:]

    zeros = jnp.zeros((GLA_DV, GLA_DK), F32)
    for d in range(2):
        st_scr[d] = jnp.concatenate([jnp.concatenate([s0_ref[0, d, 0], zeros], axis=1),
                                     jnp.concatenate([zeros, s0_ref[0, d, 1]], axis=1)], axis=0)

    def body(it, carry):
        ch = []
        for j in range(unroll):
            for d in range(2):
                ci = it * unroll + j
                cc = ci if d == 0 else nc - 1 - ci
                rows = pl.ds(pl.multiple_of(cc * CHUNK, CHUNK), CHUNK)
                ch.append(dict(d=d, rows=rows, sm=sm_ref[0, rows, :], q=q_ref[0, rows, :], k=k_ref[0, rows, :],
                               v=v_ref[0, rows, :]))
        logits = [_dot(k["sm"], wg_ref[0, :, k["d"] * LANES:(k["d"] + 1) * LANES])
                  + bg_ref[0, :, k["d"] * LANES:(k["d"] + 1) * LANES] for k in ch]
        gs = [_sel_dot(tri[k["d"]], _log_sigmoid(lg) * (1.0 / GLA_TAU)) for k, lg in zip(ch, logits)]
        for k, g in zip(ch, gs):
            g_last = g[CHUNK - 1:CHUNK] if k["d"] == 0 else g[0:1]
            k["qd"] = (k["q"] * (GLA_DK ** -0.5) * jnp.exp(g)).astype(BF16)
            k["k_inv"] = k["k"] * jnp.exp(-g)
            k["k_end"] = k["k"] * jnp.exp(g_last - g)
            k["decay"] = jnp.exp(g_last)
        a_s = [jnp.where(keep2[k["d"]], _dot_nt(stack2(k["qd"].astype(F32)), stack2(k["k_inv"])), 0.0) for k in ch]
        intra = [fold2(_dot(a, stack2(k["v"]))) for k, a in zip(ch, a_s)]
        kvs = [jnp.where(same_head, _dot_tn(k["v"], k["k_end"]), 0.0) for k in ch]
        st = [st_scr[0], st_scr[1]]
        inter = []
        for k, kv in zip(ch, kvs):
            inter.append(_dot_nt(k["qd"], st[k["d"]]))
            st[k["d"]] = st[k["d"]] * k["decay"] + kv
        for k, oi, ox in zip(ch, intra, inter):
            if k["d"] == 0:
                of_scr[k["rows"], :] = oi + ox
            else:
                ob_scr[k["rows"], :] = oi + ox
        st_scr[0] = st[0]
        st_scr[1] = st[1]
        return carry

    lax.fori_loop(0, nc // unroll, body, 0)
    for d in range(2):
        st = st_scr[d]
        st_ref[0, d, 0] = st[:GLA_DV, :GLA_DK]
        st_ref[0, d, 1] = st[GLA_DV:, GLA_DK:]
    _out_norm_gate(of_scr, ob_scr, z_ref, gn_ref, o_ref, n)


def _gla(proj3, wg_p, bg_p, gn, s0t):
    b, n, _ = proj3.shape
    npair = GLA_HEADS // 2
    col = lambda base: (lambda bi, p: (bi, 0, base + p))
    s0r = _pair_major(s0t).reshape(b * npair, 2, 2, GLA_DV, GLA_DK)
    st_spec = pl.BlockSpec((1, 2, 2, GLA_DV, GLA_DK), lambda bi, p: (bi * npair + p, 0, 0, 0, 0))
    o, st = pl.pallas_call(
        functools.partial(_gla_kernel, n=n),
        grid=(b, npair),
        in_specs=[pl.BlockSpec((1, n, LANES), col(COL_GLA_Q)),
                  pl.BlockSpec((1, n, LANES), col(COL_GLA_K)),
                  pl.BlockSpec((1, n, LANES), col(COL_GLA_V)),
                  pl.BlockSpec((1, n, LANES), col(COL_GLA_Z)),
                  pl.BlockSpec((1, n, LANES), lambda bi, p: (bi, 0, COL_SMALL)),
                  pl.BlockSpec((1, LANES, 2 * LANES), lambda bi, p: (p, 0, 0)),
                  pl.BlockSpec((1, 1, 2 * LANES), lambda bi, p: (p, 0, 0)),
                  pl.BlockSpec((1, LANES), lambda bi, p: (0, 0)),
                  st_spec],
        out_specs=[pl.BlockSpec((1, n, LANES), lambda bi, p: (bi, 0, p)), st_spec],
        out_shape=[jax.ShapeDtypeStruct((b, n, GLA_W), F32),
                   jax.ShapeDtypeStruct(s0r.shape, F32)],
        scratch_shapes=[pltpu.VMEM((n, LANES), F32), pltpu.VMEM((n, LANES), F32),
                        pltpu.VMEM((2, 2 * GLA_DV, 2 * GLA_DK), F32)],
        compiler_params=_cparams(("arbitrary", "arbitrary")),
    )(proj3, proj3, proj3, proj3, proj3, wg_p, bg_p, gn, s0r)
    return o, st


def _gdn_kernel(q_ref, k_ref, v_ref, z_ref, sm_ref, cq_ref, ck_ref, cv_ref, alog_ref, dtb_ref,
                ex_ref, gn_ref, s0_ref,
                o_ref, st_ref, a_scr, b_scr, qe_scr, dec_scr, of_scr, ob_scr, st_scr, *, n):
    nc = n // CHUNK
    unroll = 4
    o_scr = (of_scr, ob_scr)
    r, c = _tri_masks()
    tri = ((r >= c).astype(F32), (r <= c).astype(F32))
    row_id = _iota2((CHUNK, LANES), 0)
    lane_id = _iota2((CHUNK, LANES), 1)
    lo_lanes = lane_id < HEAD_W
    head_ones = _head_block_ones(LANES)
    r2 = _iota2((LANES, LANES), 0)
    c2 = _iota2((LANES, LANES), 1)
    same_head = (r2 // CHUNK) == (c2 // CHUNK)
    strict2 = (same_head & (r2 > c2), same_head & (r2 < c2))
    diag2 = r2 == c2
    blk16 = (r2 // 16) == (c2 // 16)

    def conv_silu(x_ref, w_ref, r0, first, last):
        cur = x_ref[0, pl.ds(r0, CHUNK), :]
        prev_row = x_ref[0, pl.ds(jnp.maximum(r0 - 1, 0), 1), :] * jnp.where(first, 0.0, 1.0)
        next_row = x_ref[0, pl.ds(jnp.minimum(r0 + CHUNK, n - 1), 1), :] * jnp.where(last, 0.0, 1.0)
        prev = jnp.where(row_id == 0, prev_row, pltpu.roll(cur, 1, 0))
        nxt = jnp.where(row_id == CHUNK - 1, next_row, pltpu.roll(cur, CHUNK - 1, 0))
        return _silu(prev * w_ref[0:1, :] + cur * w_ref[1:2, :] + nxt * w_ref[2:3, :])

    def stack2(x):
        return jnp.concatenate([jnp.where(lo_lanes, x, 0.0), jnp.where(lo_lanes, 0.0, x)], axis=0)

    def fold2(x):
        return x[:CHUNK] + x[CHUNK:]

    def bcast2(x):
        sw = pltpu.roll(x, HEAD_W, 1)
        return jnp.concatenate([jnp.where(lo_lanes, x, sw), jnp.where(lo_lanes, sw, x)], axis=0)

    def prep(it, carry):
        chunks = []
        for j in range(unroll):
            cc = it * unroll + j
            r0 = pl.multiple_of(cc * CHUNK, CHUNK)
            first = cc == 0
            last = cc == nc - 1
            chunks.append(dict(cc=cc, rows=pl.ds(r0, CHUNK), sm=sm_ref[0, pl.ds(r0, CHUNK), :],
                               qc=conv_silu(q_ref, cq_ref, r0, first, last),
                               kc=conv_silu(k_ref, ck_ref, r0, first, last),
                               vc=conv_silu(v_ref, cv_ref, r0, first, last)))
        qss = [_dot_sel(c["qc"] * c["qc"], head_ones) for c in chunks]
        kss = [_dot_sel(c["kc"] * c["kc"], head_ones) for c in chunks]
        for c, qs, ks in zip(chunks, qss, kss):
            c["qn"] = c["qc"] * lax.rsqrt(qs + EPS) * (GDN_DK ** -0.5)
            c["kn"] = c["kc"] * lax.rsqrt(ks + EPS)
            c["k2"] = stack2(c["kn"]).astype(BF16)
            la = -jnp.exp(alog_ref[...]) * _softplus(c["sm"] + dtb_ref[...])
            c["gates"] = jnp.where(lane_id >= SM_A, la, jax.nn.sigmoid(c["sm"]))
        kks = [_dot_nt(c["k2"], c["k2"]) for c in chunks]
        qk0s = [_dot_nt(stack2(c["qn"]), c["k2"]) for c in chunks]
        bls = [_dot_sel(c["gates"], ex_ref[0]) for c in chunks]
        for c, kk, qk0, bl in zip(chunks, kks, qk0s, bls):
            c["kk"], c["qk0"], c["bl"] = kk, qk0, bl

        ch = [dict(c=c, d=d, beta_x=c["bl"][:, 2 * d * LANES:(2 * d + 1) * LANES])
              for c in chunks for d in range(2)]
        g_xs = [_sel_dot(tri[k["d"]], k["c"]["bl"][:, (2 * k["d"] + 1) * LANES:(2 * k["d"] + 2) * LANES])
                for k in ch]
        ones8 = jnp.ones((8, LANES), F32)
        g_rows = [_sel_dot_nt(ones8, jnp.concatenate([jnp.where(lane_id == 0, g, 0.0),
                                                      jnp.where(lane_id == HEAD_W, g, 0.0)], axis=0))[0:1]
                  for g in g_xs]
        for k, g_x, g_row in zip(ch, g_xs, g_rows):
            d, c = k["d"], k["c"]
            k["g_x"] = g_x
            k["g_last"] = g_x[CHUNK - 1:CHUNK] if d == 0 else g_x[0:1]
            diff = bcast2(g_x) - g_row
            k["e_strict"] = jnp.where(strict2[d], jnp.exp(jnp.where(strict2[d], diff, 0.0)), 0.0)
            m = bcast2(k["beta_x"]) * c["kk"] * k["e_strict"]
            k["md"] = jnp.where(blk16, m, 0.0)
            k["lo"] = m - k["md"]
            k["eg"] = jnp.exp(g_x)
            k["rhs"] = jnp.concatenate([stack2(c["vc"] * k["beta_x"]),
                                        stack2(c["kn"] * k["beta_x"] * k["eg"])], axis=1)
        ps = [-k["md"] for k in ch]
        nds = list(ps)
        for _ in range(3):
            ps = [_dot(p, p) for p in ps]
            ts = [_dot(nd, p) for nd, p in zip(nds, ps)]
            nds = [nd + p + t for nd, p, t in zip(nds, ps, ts)]
        ts = [_dot(nd, k["rhs"]) for nd, k in zip(nds, ch)]
        xs = [k["rhs"] + t for k, t in zip(ch, ts)]
        for _ in range(CHUNK // 16 - 1):
            ts = [_dot(k["lo"], x) for k, x in zip(ch, xs)]
            ys = [k["rhs"] - t for k, t in zip(ch, ts)]
            ts = [_dot(nd, y) for nd, y in zip(nds, ys)]
            xs = [y + t for y, t in zip(ys, ts)]
        uws = [jnp.concatenate([fold2(x[:, :LANES]), fold2(x[:, LANES:])], axis=1) for x in xs]
        k_ends = [k["c"]["kn"] * jnp.exp(k["g_last"] - k["g_x"]) for k in ch]
        abs_ = [_dot_tn(uw, ke) for uw, ke in zip(uws, k_ends)]
        qus = [_dot(k["c"]["qk0"] * jnp.where(diag2, 1.0, k["e_strict"]), x) for k, x in zip(ch, xs)]
        for k, ab, qu in zip(ch, abs_, qus):
            d, c = k["d"], k["c"]
            blk = pl.ds(pl.multiple_of(c["cc"] * LANES, LANES), LANES)
            b_scr[d, blk, :] = jnp.where(same_head, ab[:LANES], 0.0)
            a_scr[d, blk, :] = jnp.where(same_head, ab[LANES:], 0.0).astype(BF16)
            qe_scr[d, c["rows"], :] = (c["qn"] * k["eg"] - fold2(qu[:, LANES:])).astype(BF16)
            o_scr[d][c["rows"], :] = fold2(qu[:, :LANES])
            dec_scr[d, pl.ds(pl.multiple_of(c["cc"] * 8, 8), 8), :] = jnp.broadcast_to(
                jnp.exp(k["g_last"]), (8, LANES))
        return carry

    lax.fori_loop(0, nc // unroll, prep, 0)

    zeros = jnp.zeros((GDN_DV, GDN_DK), F32)
    for d in range(2):
        st_scr[d] = jnp.concatenate([jnp.concatenate([s0_ref[0, d, 0], zeros], axis=1),
                                     jnp.concatenate([zeros, s0_ref[0, d, 1]], axis=1)], axis=0)

    def step(ci, carry):
        loaded = []
        for d in range(2):
            cc = ci if d == 0 else nc - 1 - ci
            rows = pl.ds(pl.multiple_of(cc * CHUNK, CHUNK), CHUNK)
            blk = pl.ds(pl.multiple_of(cc * LANES, LANES), LANES)
            loaded.append((rows, st_scr[d], a_scr[d, blk, :], b_scr[d, blk, :], qe_scr[d, rows, :],
                           o_scr[d][rows, :], dec_scr[d, pl.ds(pl.multiple_of(cc * 8, 8), 1), :]))
        ts = [_dot(st, a) for _, st, a, _, _, _, _ in loaded]
        qs = [_dot_nt(qe, st) for _, st, _, _, qe, _, _ in loaded]
        for d, ((rows, st, _, b, _, o_part, decay), t, q) in enumerate(zip(loaded, ts, qs)):
            st_scr[d] = st * decay - t + b
            o_scr[d][rows, :] = o_part + q
        return carry

    lax.fori_loop(0, nc, step, 0)
    for d in range(2):
        st = st_scr[d]
        st_ref[0, d, 0] = st[:GDN_DV, :GDN_DK]
        st_ref[0, d, 1] = st[GDN_DV:, GDN_DK:]
    _out_norm_gate(of_scr, ob_scr, z_ref, gn_ref, o_ref, n)


def _gdn_select_constants():
    npair = GDN_HEADS // 2
    ex = np.zeros((npair, LANES, 4 * LANES), np.float32)
    for p in range(npair):
        for d in range(2):
            for h in range(2):
                hh = 2 * p + h
                lanes_b = slice(2 * d * LANES + h * HEAD_W, 2 * d * LANES + (h + 1) * HEAD_W)
                lanes_g = slice((2 * d + 1) * LANES + h * HEAD_W, (2 * d + 1) * LANES + (h + 1) * HEAD_W)
                ex[p, SM_BETA + d * GDN_HEADS + hh, lanes_b] = 1.0
                ex[p, SM_A + d * GDN_HEADS + hh, lanes_g] = 1.0
    return jnp.asarray(ex)


def _gdn(proj3, conv_w, alog_row, dtb_row, gn, s0t):
    b, n, _ = proj3.shape
    npair = GDN_HEADS // 2
    ex = _gdn_select_constants()
    nc = n // CHUNK
    col = lambda base: (lambda bi, p: (bi, 0, base + p))
    cw = GDN_HEADS * GDN_DK
    s0r = _pair_major(s0t).reshape(b * npair, 2, 2, GDN_DV, GDN_DK)
    st_spec = pl.BlockSpec((1, 2, 2, GDN_DV, GDN_DK), lambda bi, p: (bi * npair + p, 0, 0, 0, 0))
    pair_spec = lambda shp: pl.BlockSpec((1,) + shp, lambda bi, p: (p, 0, 0))
    o, st = pl.pallas_call(
        functools.partial(_gdn_kernel, n=n),
        grid=(b, npair),
        in_specs=[pl.BlockSpec((1, n, LANES), col(COL_GDN_Q)),
                  pl.BlockSpec((1, n, LANES), col(COL_GDN_K)),
                  pl.BlockSpec((1, n, LANES), col(COL_GDN_V)),
                  pl.BlockSpec((1, n, LANES), col(COL_GDN_Z)),
                  pl.BlockSpec((1, n, LANES), lambda bi, p: (bi, 0, COL_SMALL)),
                  pl.BlockSpec((3, LANES), lambda bi, p: (0, p)),
                  pl.BlockSpec((3, LANES), lambda bi, p: (0, cw // LANES + p)),
                  pl.BlockSpec((3, LANES), lambda bi, p: (0, 2 * cw // LANES + p)),
                  pl.BlockSpec((1, LANES), lambda bi, p: (0, 0)),
                  pl.BlockSpec((1, LANES), lambda bi, p: (0, 0)),
                  pair_spec((LANES, 4 * LANES)),
                  pl.BlockSpec((1, LANES), lambda bi, p: (0, 0)),
                  st_spec],
        out_specs=[pl.BlockSpec((1, n, LANES), lambda bi, p: (bi, 0, p)), st_spec],
        out_shape=[jax.ShapeDtypeStruct((b, n, GDN_W), F32),
                   jax.ShapeDtypeStruct(s0r.shape, F32)],
        scratch_shapes=[pltpu.VMEM((2, nc * LANES, LANES), BF16),
                        pltpu.VMEM((2, nc * LANES, LANES), F32),
                        pltpu.VMEM((2, n, LANES), BF16),
                        pltpu.VMEM((2, nc * 8, LANES), F32),
                        pltpu.VMEM((n, LANES), F32), pltpu.VMEM((n, LANES), F32),
                        pltpu.VMEM((2, 2 * GDN_DV, 2 * GDN_DK), F32)],
        compiler_params=_cparams(("arbitrary", "arbitrary")),
    )(proj3, proj3, proj3, proj3, proj3, conv_w, conv_w, conv_w, alog_row, dtb_row, ex, gn, s0r)
    return o, st


def _outproj_kernel(omla_ref, ogla_ref, ogdn_ref, x_ref, mod_ref, nrm_ref, wout_ref, wr_ref,
                    x1_ref, xm_ref, afft_ref):
    y = (_dot(omla_ref[...], wout_ref[0:MLA_W, :])
         + _dot(ogla_ref[...], wout_ref[MLA_W:MLA_W + GLA_W, :])
         + _dot(ogdn_ref[...], wout_ref[MLA_W + GLA_W:, :]))
    m = mod_ref[0]
    x1 = x_ref[...] + m[2:3] * _rms(y, nrm_ref[1:2, :])
    x1_ref[...] = x1
    xm2 = _rms(x1, nrm_ref[2:3, :]) * (1.0 + m[4:5]) + m[3:4]
    d = xm2.shape[1]
    xm_ref[:, :d] = xm2
    logits = _dot_x3(xm2, wr_ref[...])
    lane = _iota2(logits.shape, 1)
    logits = jnp.where(lane < N_EXPERTS, logits, -jnp.inf)
    e = jnp.exp(logits - jnp.max(logits, axis=-1, keepdims=True))
    aff = e / jnp.sum(e, axis=-1, keepdims=True)
    xm_ref[:, d:] = aff
    sel = (_iota2((N_EXPERTS, LANES), 0) == _iota2((N_EXPERTS, LANES), 1)).astype(F32)
    afft_ref[...] = _sel_dot_nt(sel, aff, pieces=3)


def _out_projection(o_mla, o_gla, o_gdn, x2d, modg, nrm, w_out, w_router_p, tiles_per_batch, tm):
    t, d = x2d.shape
    per_batch = modg.shape[0] > 1
    return pl.pallas_call(
        _outproj_kernel,
        grid=(t // tm,),
        in_specs=[pl.BlockSpec((tm, MLA_W), lambda i: (i, 0)),
                  pl.BlockSpec((tm, GLA_W), lambda i: (i, 0)),
                  pl.BlockSpec((tm, GDN_W), lambda i: (i, 0)),
                  pl.BlockSpec((tm, d), lambda i: (i, 0)),
                  pl.BlockSpec((1, 6, d), (lambda i: (i // tiles_per_batch, 0, 0)) if per_batch
                               else (lambda i: (0, 0, 0))),
                  pl.BlockSpec((4, d), lambda i: (0, 0)),
                  pl.BlockSpec((d, d), lambda i: (0, 0)),
                  pl.BlockSpec((d, LANES), lambda i: (0, 0))],
        out_specs=[pl.BlockSpec((tm, d), lambda i: (i, 0)),
                   pl.BlockSpec((tm, d + LANES), lambda i: (i, 0)),
                   pl.BlockSpec((N_EXPERTS, tm), lambda i: (0, i))],
        out_shape=[jax.ShapeDtypeStruct((t, d), F32),
                   jax.ShapeDtypeStruct((t, d + LANES), F32),
                   jax.ShapeDtypeStruct((N_EXPERTS, t), F32)],
        compiler_params=_cparams(("arbitrary",)),
    )(o_mla, o_gla, o_gdn, x2d, modg, nrm, w_out, w_router_p)


def _route_kernel(afft_ref, rank_ref, pos_ref, *, cap):
    a = afft_ref[...]
    t = a.shape[1]

    def search(i, cur):
        cand = cur | jnp.left_shift(jnp.int32(1), 30 - i)
        cnt = jnp.sum((a >= lax.bitcast_convert_type(cand, F32)).astype(F32), axis=1, keepdims=True)
        return jnp.where(cnt >= cap, cand, cur)

    kth = lax.bitcast_convert_type(lax.fori_loop(0, 31, search, jnp.zeros((a.shape[0], 1), I32)), F32)
    gt = (a > kth).astype(F32)
    eq = (a == kth).astype(F32)
    need = cap - jnp.sum(gt, axis=1, keepdims=True)
    before = (_iota2((LANES, LANES), 0) < _iota2((LANES, LANES), 1)).astype(BF16)
    nb = t // LANES
    carry_eq = jnp.zeros((a.shape[0], 1), F32)
    carry = jnp.zeros((a.shape[0], 1), F32)
    for b in range(nb):
        sl = slice(b * LANES, (b + 1) * LANES)
        eqb = eq[:, sl]
        pre = jnp.dot(eqb.astype(BF16), before, preferred_element_type=F32) + carry_eq
        selb = jnp.maximum(gt[:, sl], jnp.where(pre < need, eqb, 0.0))
        carry_eq = carry_eq + jnp.sum(eqb, axis=1, keepdims=True)
        local = jnp.dot(selb.astype(BF16), before, preferred_element_type=F32)
        rank_ref[:, sl] = jnp.where(selb > 0.0, local, -1.0)
        pos_ref[:, sl] = local + carry
        carry = carry + jnp.sum(selb, axis=1, keepdims=True)


def _route(afft, cap):
    e, t = afft.shape
    return pl.pallas_call(
        functools.partial(_route_kernel, cap=cap),
        out_shape=[jax.ShapeDtypeStruct((e, t), F32), jax.ShapeDtypeStruct((e, t), F32)],
        compiler_params=_cparams(),
    )(afft)


def _compact_kernel(off_ref, rank_ref, idx_ref, *, nb):
    e = pl.program_id(0)
    idx_ref[...] = jnp.zeros(idx_ref.shape, I32)
    slot = _iota2((LANES, LANES), 0)
    tok = _iota2((LANES, LANES), 0).astype(BF16)
    group = 4

    def blks(g, carry):
        bs = [g * group + j for j in range(group)]
        onehots = [(rank_ref[0, pl.ds(b, 1), :].astype(I32) == slot).astype(BF16) for b in bs]
        ids = [jnp.dot(oh, tok, preferred_element_type=F32).astype(I32) + b * LANES for oh, b in zip(onehots, bs)]
        for b, v in zip(bs, ids):
            idx_ref[0, pl.ds(off_ref[e, b], LANES), :] = v
        return carry

    lax.fori_loop(0, nb // group, blks, 0)


def _compact(off, rank, cap):
    e, t = rank.shape
    nb = t // LANES
    rows = cap + LANES
    idx = pl.pallas_call(
        functools.partial(_compact_kernel, nb=nb),
        grid_spec=pltpu.PrefetchScalarGridSpec(
            num_scalar_prefetch=1,
            grid=(e,),
            in_specs=[pl.BlockSpec((1, nb, LANES), lambda ei, off_r: (ei, 0, 0))],
            out_specs=pl.BlockSpec((1, rows, LANES), lambda ei, off_r: (ei, 0, 0))),
        out_shape=jax.ShapeDtypeStruct((e, rows, LANES), I32),
        compiler_params=_cparams(("arbitrary",)),
    )(off, rank.reshape(e, nb, LANES))
    return idx[:, :cap, 0]


def _gather_copy(x_hbm, xe_scr, sem, slot, src_row, dst_row, nrows):
    return pltpu.make_async_copy(x_hbm.at[pl.ds(src_row, nrows)], xe_scr.at[slot, pl.ds(dst_row, nrows)],
                                 sem.at[slot])


def _expert_kernel(idx_ref, x_hbm, w1_ref, w3_ref, w2_ref, ye_ref, xe_scr, xb_scr, sem, *, cap):
    e = pl.program_id(0)
    ne = pl.num_programs(0)

    slot = e % 2
    nxt = (e + 1) % ne
    nxt_slot = 1 - slot
    d = xb_scr.shape[1]

    @pl.when(e == 0)
    def _():
        def row(s, carry):
            _gather_copy(x_hbm, xe_scr, sem, 0, idx_ref[0, s], s, 1).start()
            return carry
        lax.fori_loop(0, cap, row, 0)

    _gather_copy(x_hbm, xe_scr, sem, slot, 0, 0, cap).wait()
    xb_scr[...] = xe_scr[slot, :, :d].astype(BF16)
    ge = xe_scr[slot, :, d:]
    gate = jnp.sum(jnp.where(_iota2(ge.shape, 1) == e, ge, 0.0), axis=-1, keepdims=True)

    for s in range(cap):
        _gather_copy(x_hbm, xe_scr, sem, nxt_slot, idx_ref[nxt, s], s, 1).start()

    ff = w1_ref.shape[2]
    fchunk = 512
    y = jnp.zeros((cap, w2_ref.shape[2]), F32)
    for j in range(ff // fchunk):
        fs = slice(j * fchunk, (j + 1) * fchunk)
        h1 = _dot(xb_scr[...], w1_ref[0, :, fs])
        h3 = _dot(xb_scr[...], w3_ref[0, :, fs])
        y = y + _dot(_silu(h1) * h3, w2_ref[0, fs, :])
    ye_ref[0] = y * gate

    @pl.when(e == ne - 1)
    def _():
        _gather_copy(x_hbm, xe_scr, sem, nxt_slot, 0, 0, cap).wait()


def _expert_ffn(idx, xm, w1, w3, w2, cap):
    e, d, ff = w1.shape
    wspec = lambda shp: pl.BlockSpec((1,) + shp, lambda ei, idx_r: (ei, 0, 0))
    return pl.pallas_call(
        functools.partial(_expert_kernel, cap=cap),
        grid_spec=pltpu.PrefetchScalarGridSpec(
            num_scalar_prefetch=1,
            grid=(e,),
            in_specs=[pl.BlockSpec(memory_space=pl.ANY),
                      wspec((d, ff)), wspec((d, ff)), wspec((ff, d))],
            out_specs=pl.BlockSpec((1, cap, d), lambda ei, idx_r: (ei, 0, 0)),
            scratch_shapes=[pltpu.VMEM((2, cap, xm.shape[1]), F32), pltpu.VMEM((cap, d), BF16),
                            pltpu.SemaphoreType.DMA((2,))]),
        out_shape=jax.ShapeDtypeStruct((e, cap, d), F32),
        compiler_params=_cparams(("arbitrary",)),
    )(idx, xm, w1, w3, w2)


def _scatter_kernel(idx_ref, lo_ref, ye_ref, x1_ref, mod_ref, nrm_ref, o_ref, *, part_rows):
    part = pl.program_id(0)
    e = pl.program_id(1)

    @pl.when(e == 0)
    def _():
        o_ref[...] = jnp.zeros(o_ref.shape, F32)

    base = part * part_rows

    lo = lo_ref[e, part]
    hi = lo_ref[e, part + 1]
    group = 8
    ngroups = (hi - lo) // group

    def add_group(g, carry):
        s0 = lo + g * group
        ts = [idx_ref[e, s0 + j] - base for j in range(group)]
        new = [o_ref[pl.ds(t, 1), :] + ye_ref[0, pl.ds(s0 + j, 1), :] for j, t in enumerate(ts)]
        for t, v in zip(ts, new):
            o_ref[pl.ds(t, 1), :] = v
        return carry

    lax.fori_loop(0, ngroups, add_group, 0)

    def add(s, carry):
        t = idx_ref[e, s] - base
        o_ref[pl.ds(t, 1), :] = o_ref[pl.ds(t, 1), :] + ye_ref[0, pl.ds(s, 1), :]
        return carry

    lax.fori_loop(lo + ngroups * group, hi, add, 0)

    @pl.when(e == pl.num_programs(1) - 1)
    def _():
        m = mod_ref[0]
        rows = 512

        def blk(i, carry):
            rs = pl.ds(pl.multiple_of(i * rows, rows), rows)
            o_ref[rs, :] = x1_ref[rs, :] + m[5:6] * _rms(o_ref[rs, :], nrm_ref[3:4, :])
            return carry

        lax.fori_loop(0, part_rows // rows, blk, 0)


def _scatter_residual(idx, bounds, ye, x1, modg, nrm, parts_per_batch, nparts):
    e, cap, d = ye.shape
    t = x1.shape[0]
    part_rows = t // nparts
    per_batch = modg.shape[0] > 1
    return pl.pallas_call(
        functools.partial(_scatter_kernel, part_rows=part_rows),
        grid_spec=pltpu.PrefetchScalarGridSpec(
            num_scalar_prefetch=2,
            grid=(nparts, e),
            in_specs=[pl.BlockSpec((1, cap, d), lambda pi, ei, a, b: (ei, 0, 0)),
                      pl.BlockSpec((part_rows, d), lambda pi, ei, a, b: (pi, 0)),
                      pl.BlockSpec((1, 6, d), (lambda pi, ei, a, b: (pi // parts_per_batch, 0, 0)) if per_batch
                                   else (lambda pi, ei, a, b: (0, 0, 0))),
                      pl.BlockSpec((4, d), lambda pi, ei, a, b: (0, 0))],
            out_specs=pl.BlockSpec((part_rows, d), lambda pi, ei, a, b: (pi, 0))),
        out_shape=jax.ShapeDtypeStruct((t, d), F32),
        compiler_params=_cparams(("arbitrary", "arbitrary")),
    )(idx, bounds, ye, x1, modg, nrm)


def _rot_cols(w):
    s = w.shape[:-1]
    w2 = w.reshape(s + (ROPE_DIM // 2, 2))
    return jnp.stack([-w2[..., 1], w2[..., 0]], axis=-1).reshape(s + (ROPE_DIM,))


def _layer_weights(l, w_in, mla_q_norm, mla_wq_b, mla_kv_norm, mla_wkv_b, gla_wg, gla_bg, gla_out_norm,
                   gdn_conv, gdn_a_log, gdn_dt_bias, gdn_out_norm, w_out, sandwich_norms, w_router,
                   w_e1, w_e3, w_e2):
    wi = w_in[l]
    o = 0
    parts = {}
    for name, sz in (("q_lat", Q_LORA), ("kv_lat", KV_LORA), ("k_rope", ROPE_DIM),
                     ("gla_q", GLA_W), ("gla_k", GLA_W), ("gla_v", GLA_W), ("gla_glr", 2 * GLA_GATE_RANK),
                     ("gla_z", GLA_W), ("gdn_q", GDN_W), ("gdn_k", GDN_W), ("gdn_v", GDN_W), ("gdn_z", GDN_W),
                     ("gdn_b", 2 * GDN_HEADS), ("gdn_a", 2 * GDN_HEADS)):
        parts[name] = wi[:, o:o + sz]
        o += sz
    w_in_p = jnp.concatenate(
        [parts[k] for k in ("q_lat", "kv_lat", "gla_q", "gla_k", "gla_v", "gla_z",
                            "gdn_q", "gdn_k", "gdn_v", "gdn_z")]
        + [parts["k_rope"], _rot_cols(parts["k_rope"]), parts["gla_glr"], parts["gdn_b"], parts["gdn_a"]],
        axis=1).astype(BF16)

    wq = mla_wq_b[l].reshape(Q_LORA, MLA_HEADS, NOPE_DIM + ROPE_DIM)
    wq_ext = jnp.concatenate([wq, _rot_cols(wq[..., NOPE_DIM:])], axis=-1).reshape(Q_LORA, -1).astype(BF16)
    wkv = mla_wkv_b[l].reshape(KV_LORA, MLA_HEADS, NOPE_DIM + MLA_V_DIM)
    wkv_ext = jnp.concatenate([wkv[..., :NOPE_DIM].reshape(KV_LORA, -1),
                               wkv[..., NOPE_DIM:].reshape(KV_LORA, -1)], axis=1).astype(BF16)

    npair = GLA_HEADS // 2
    wg_p = jnp.zeros((npair, LANES, 2 * LANES), F32)
    bg_p = jnp.zeros((npair, 1, 2 * LANES), F32)
    for p in range(npair):
        for d in range(2):
            rows = slice(SM_GLR + d * GLA_GATE_RANK, SM_GLR + (d + 1) * GLA_GATE_RANK)
            wg_p = wg_p.at[p, rows, d * LANES:(d + 1) * LANES].set(gla_wg[l, d][:, p * LANES:(p + 1) * LANES])
            bg_p = bg_p.at[p, 0, d * LANES:(d + 1) * LANES].set(gla_bg[l, d][p * LANES:(p + 1) * LANES])

    alog_row = jnp.zeros((1, LANES), F32).at[0, SM_A:].set(gdn_a_log[l].reshape(-1))
    dtb_row = jnp.zeros((1, LANES), F32).at[0, SM_A:].set(gdn_dt_bias[l].reshape(-1))
    w_router_p = jnp.zeros((D_MODEL, LANES), F32).at[:, :N_EXPERTS].set(w_router[l])
    return dict(
        w_in=w_in_p, gq=mla_q_norm[l][None], wq=wq_ext, gkv=mla_kv_norm[l][None], wkv=wkv_ext,
        wg=wg_p.astype(BF16), bg=bg_p, gla_gn=jnp.tile(gla_out_norm[l], 2)[None],
        conv=gdn_conv[l], alog=alog_row, dtb=dtb_row, gdn_gn=jnp.tile(gdn_out_norm[l], 2)[None],
        w_out=w_out[l].astype(BF16), nrm=sandwich_norms[l], w_router=w_router_p,
        w1=w_e1[l], w3=w_e3[l], w2=w_e2[l])


def _rope_tables(n, rotate):
    ones = np.ones((1, NOPE_DIM), np.float32)
    if not rotate:
        t1 = np.concatenate([ones, np.ones((1, ROPE_DIM), np.float32), np.zeros((1, ROPE_DIM), np.float32)], 1)
        z = np.zeros_like(t1)
        return tuple(jnp.asarray(np.broadcast_to(t, (n, LANES)).copy()) for t in (t1, z, z))
    rows = n // GRID_W
    row = jnp.repeat(jnp.arange(rows), GRID_W).astype(F32)
    col = jnp.tile(jnp.arange(GRID_W), rows).astype(F32)
    half = ROPE_DIM // 2
    freqs = ROPE_BASE ** (-jnp.arange(0, half, 2, dtype=F32) / half)
    ang = jnp.concatenate([row[:, None] * freqs, col[:, None] * freqs], axis=-1)
    cos = jnp.repeat(jnp.cos(ang), 2, axis=-1)
    sin = jnp.repeat(jnp.sin(ang), 2, axis=-1)
    z32 = jnp.zeros((n, ROPE_DIM), F32)
    t1 = jnp.concatenate([jnp.ones((n, NOPE_DIM), F32), cos, z32], axis=1)
    t2 = jnp.concatenate([jnp.zeros((n, NOPE_DIM), F32), sin, z32], axis=1)
    t3 = jnp.concatenate([jnp.zeros((n, NOPE_DIM), F32), z32, jnp.ones((n, ROPE_DIM), F32)], axis=1)
    return t1, t2, t3


def _trunk_layer(x, lw, modg, ctx_cache, tabs):
    b, n, d = x.shape
    t = b * n
    tm = min(512, n)
    tiles_per_batch = n // tm
    x2d = x.reshape(t, d)

    proj = _in_projection(x2d, modg, lw["nrm"][0:1], lw["w_in"], tiles_per_batch, tm)
    proj3 = proj.reshape(b, n, PROJ_W)

    q, k, v, ckv, k_rope = _mla_prep(proj, tabs, lw["gq"], lw["wq"], lw["gkv"], lw["wkv"], tiles_per_batch, tm)
    qk_w = MLA_HEADS * MLA_QK_PAD
    if ctx_cache is None:
        kc = vc = None
        s0_gla = jnp.zeros((b, 2, GLA_HEADS, GLA_DV, GLA_DK), F32)
        s0_gdn = jnp.zeros((b, 2, GDN_HEADS, GDN_DV, GDN_DK), F32)
    else:
        ckv_c, kr_c, s0_gla, s0_gdn = ctx_cache
        lc = ckv_c.shape[1]
        kc, vc = _ctx_keys(ckv_c.reshape(b * lc, KV_LORA), kr_c.reshape(b * lc, ROPE_DIM), lw["wkv"])
        kc = kc.reshape(b, lc, qk_w)
        vc = vc.reshape(b, lc, MLA_W)
        s0_gla = jnp.swapaxes(s0_gla, -1, -2)
        s0_gdn = jnp.swapaxes(s0_gdn, -1, -2)
    o_mla = _attention(q.reshape(b, n, qk_w), k.reshape(b, n, qk_w), v.reshape(b, n, MLA_W), kc, vc,
                       tq=min(256, n))

    o_gla, st_gla = _gla(proj3, lw["wg"], lw["bg"], lw["gla_gn"], s0_gla)
    o_gdn, st_gdn = _gdn(proj3, lw["conv"], lw["alog"], lw["dtb"], lw["gdn_gn"], s0_gdn)

    x1, xm2, afft = _out_projection(o_mla.reshape(t, MLA_W), o_gla.reshape(t, GLA_W),
                                        o_gdn.reshape(t, GDN_W), x2d, modg, lw["nrm"], lw["w_out"],
                                        lw["w_router"], tiles_per_batch, tm)

    cap = EC_CAPACITY * t // N_EXPERTS
    rank, pos = _route(afft, cap)
    off = pos[:, ::LANES].astype(I32)
    idx = _compact(off, rank, cap)
    ye = _expert_ffn(idx, xm2, lw["w1"], lw["w3"], lw["w2"], cap)
    nparts = 4 if t >= 4 * 512 else 1
    part_rows = t // nparts
    bounds = jnp.concatenate([off[:, ::part_rows // LANES], jnp.full((N_EXPERTS, 1), cap, I32)], axis=1)
    x2 = _scatter_residual(idx, bounds, ye, x1, modg, lw["nrm"], max(n // part_rows, 1), nparts)

    new = None
    if ctx_cache is None:
        st_gla = jnp.swapaxes(st_gla.reshape(b, GLA_HEADS // 2, 2, 2, GLA_DV, GLA_DK), 1, 2)
        st_gdn = jnp.swapaxes(st_gdn.reshape(b, GDN_HEADS // 2, 2, 2, GDN_DV, GDN_DK), 1, 2)
        new = (ckv.reshape(b, n, KV_LORA), k_rope.reshape(b, n, ROPE_DIM),
               jnp.swapaxes(st_gla.reshape(b, 2, GLA_HEADS, GLA_DV, GLA_DK), -1, -2),
               jnp.swapaxes(st_gdn.reshape(b, 2, GDN_HEADS, GDN_DV, GDN_DK), -1, -2))
    return x2.reshape(b, n, d), new


def _pair_major(s0):
    b, two, h = s0.shape[:3]
    return jnp.swapaxes(s0.reshape(b, 2, h // 2, 2, s0.shape[-2], s0.shape[-1]), 1, 2)


def kernel(x_prompt, x_sample, cache_ckv, cache_krope, state_gla, state_gdn, c, c_ctx, w_in, mla_q_norm,
           mla_wq_b, mla_kv_norm, mla_wkv_b, gla_wg, gla_bg, gla_out_norm, gdn_conv, gdn_a_log, gdn_dt_bias,
           gdn_out_norm, w_out, w_mod, b_mod, sandwich_norms, w_router, w_e1, w_e3, w_e2):
    depth = w_in.shape[0]
    nb_s = x_sample.shape[0]
    cond = jnp.zeros((8, D_MODEL), F32).at[0].set(c_ctx).at[1:1 + nb_s].set(c)
    mod = _modulation(cond, w_mod, b_mod)

    tabs_ctx = _rope_tables(8, rotate=False)
    tabs_smp = _rope_tables(x_sample.shape[1], rotate=True)
    tabs_ctx = tuple(jnp.broadcast_to(t[:1], (min(512, x_prompt.shape[1]), LANES)) for t in tabs_ctx)

    lws = [_layer_weights(l, w_in, mla_q_norm, mla_wq_b, mla_kv_norm, mla_wkv_b, gla_wg, gla_bg, gla_out_norm,
                          gdn_conv, gdn_a_log, gdn_dt_bias, gdn_out_norm, w_out, sandwich_norms, w_router,
                          w_e1, w_e3, w_e2) for l in range(depth)]

    xp = x_prompt
    ckv_l, kr_l, sg_l, sd_l = [], [], [], []
    for l in range(depth):
        modg = mod[l, 0:1].reshape(1, 6, D_MODEL)
        xp, (ckv, kr, sg, sd) = _trunk_layer(xp, lws[l], modg, None, tabs_ctx)
        ckv_l.append(ckv)
        kr_l.append(kr)
        sg_l.append(sg)
        sd_l.append(sd)

    xs = x_sample
    for l in range(depth):
        modg = mod[l, 1:1 + nb_s].reshape(nb_s, 6, D_MODEL)
        xs, _ = _trunk_layer(xs, lws[l], modg,
                             (cache_ckv[:, l], cache_krope[:, l], state_gla[:, l], state_gdn[:, l]), tabs_smp)

    return (xp, xs, jnp.stack(ckv_l, axis=1), jnp.stack(kr_l, axis=1),
            jnp.stack(sg_l, axis=1), jnp.stack(sd_l, axis=1))
```

```python
import functools

import numpy as np
import jax
import jax.numpy as jnp
from jax import lax
from jax.experimental import pallas as pl
from jax.experimental.pallas import tpu as pltpu

F32 = jnp.float32
BF16 = jnp.bfloat16
I32 = jnp.int32

D_MODEL = 1024
DEPTH = 2
GRID_W = 64
EPS = 1e-6
CHUNK = 64

MLA_HEADS = 4
Q_LORA = 256
KV_LORA = 128
NOPE_DIM = 64
ROPE_DIM = 32
MLA_V_DIM = 64
ROPE_BASE = 10000.0
MLA_SCALE = (NOPE_DIM + ROPE_DIM) ** -0.5
MLA_QK_PAD = 128

GLA_HEADS = 4
GLA_DK = 64
GLA_DV = 64
GLA_GATE_RANK = 16
GLA_TAU = 16.0

GDN_HEADS = 8
GDN_DK = 64
GDN_DV = 64

N_EXPERTS = 16
EXPERT_FF = 1024
EC_CAPACITY = 2

LANES = 128
HEAD_W = 64
MLA_W = MLA_HEADS * MLA_V_DIM
GLA_W = GLA_HEADS * GLA_DV
GDN_W = GDN_HEADS * GDN_DV

COL_QLAT = 0
COL_KVLAT = 2
COL_GLA_Q = 3
COL_GLA_K = 5
COL_GLA_V = 7
COL_GLA_Z = 9
COL_GDN_Q = 11
COL_GDN_K = 15
COL_GDN_V = 19
COL_GDN_Z = 23
COL_SMALL = 27
PROJ_W = 28 * LANES
SM_KROPE = 0
SM_KROT = 32
SM_GLR = 64
SM_BETA = 96
SM_A = 112

VMEM_LIMIT = 56 * 1024 * 1024


def _cparams(sem=None, **kw):
    return pltpu.CompilerParams(dimension_semantics=sem, vmem_limit_bytes=VMEM_LIMIT, **kw)


def _dot(a, b):
    return jnp.dot(a.astype(BF16), b.astype(BF16), preferred_element_type=F32)


def _dot_nt(a, b):
    return lax.dot_general(a.astype(BF16), b.astype(BF16), (((1,), (1,)), ((), ())),
                           preferred_element_type=F32)


def _dot_tn(a, b):
    return lax.dot_general(a.astype(BF16), b.astype(BF16), (((0,), (0,)), ((), ())),
                           preferred_element_type=F32)


def _dot_x3(a, b):
    a1 = a.astype(BF16)
    a2 = (a - a1.astype(F32)).astype(BF16)
    b1 = b.astype(BF16)
    b2 = (b - b1.astype(F32)).astype(BF16)
    return (jnp.dot(a1, b1, preferred_element_type=F32) + jnp.dot(a1, b2, preferred_element_type=F32)
            + jnp.dot(a2, b1, preferred_element_type=F32))


def _split(x, pieces):
    out = []
    for _ in range(pieces):
        p = x.astype(BF16)
        out.append(p)
        x = x - p.astype(F32)
    return out


def _dot_sel(x, sel, pieces=2):
    s = sel.astype(BF16)
    return sum(jnp.dot(p, s, preferred_element_type=F32) for p in _split(x, pieces))


def _sel_dot(sel, x, pieces=2):
    s = sel.astype(BF16)
    return sum(jnp.dot(s, p, preferred_element_type=F32) for p in _split(x, pieces))


def _sel_dot_nt(sel, x, pieces=2):
    s = sel.astype(BF16)
    dn = (((1,), (1,)), ((), ()))
    return sum(lax.dot_general(s, p, dn, preferred_element_type=F32) for p in _split(x, pieces))


def _silu(x):
    return x * jax.nn.sigmoid(x)


def _softplus(x):
    return jnp.maximum(x, 0.0) + jnp.log(1.0 + jnp.exp(-jnp.abs(x)))


def _log_sigmoid(x):
    return -_softplus(-x)


def _rms(x, g):
    return x * lax.rsqrt(jnp.mean(x * x, axis=-1, keepdims=True) + EPS) * g


def _iota2(shape, dim):
    return lax.broadcasted_iota(I32, shape, dim)


def _head_block_ones(width):
    r = _iota2((width, width), 0) // HEAD_W
    c = _iota2((width, width), 1) // HEAD_W
    return (r == c).astype(F32)


def _mod_kernel(cond_ref, w_ref, b_ref, o_ref):
    s = _silu(cond_ref[...])
    o_ref[0] = _dot(s, w_ref[0]) + b_ref[0]


def _modulation(cond, w_mod, b_mod):
    depth, d, d6 = w_mod.shape
    nj = d6 // d
    return pl.pallas_call(
        _mod_kernel,
        grid=(depth, nj),
        in_specs=[pl.BlockSpec((8, d), lambda l, j: (0, 0)),
                  pl.BlockSpec((1, d, d), lambda l, j: (l, 0, j)),
                  pl.BlockSpec((1, 1, d), lambda l, j: (l, 0, j))],
        out_specs=pl.BlockSpec((1, 8, d), lambda l, j: (l, 0, j)),
        out_shape=jax.ShapeDtypeStruct((depth, 8, d6), F32),
        compiler_params=_cparams(("arbitrary", "arbitrary")),
    )(cond, w_mod, b_mod.reshape(depth, 1, d6))


def _inproj_kernel(x_ref, mod_ref, nrm_ref, w_ref, o_ref, *, col_chunk):
    x = x_ref[...]
    m = mod_ref[0]
    xm = (_rms(x, nrm_ref[...]) * (1.0 + m[1:2]) + m[0:1]).astype(BF16)
    for j in range(o_ref.shape[1] // col_chunk):
        sl = slice(j * col_chunk, (j + 1) * col_chunk)
        o_ref[:, sl] = jnp.dot(xm, w_ref[:, sl], preferred_element_type=F32)


def _in_projection(x2d, modg, nrm0, w_in_p, tiles_per_batch, tm):
    t, d = x2d.shape
    pw = w_in_p.shape[1]
    per_batch = modg.shape[0] > 1
    return pl.pallas_call(
        functools.partial(_inproj_kernel, col_chunk=512),
        grid=(t // tm,),
        in_specs=[pl.BlockSpec((tm, d), lambda i: (i, 0)),
                  pl.BlockSpec((1, 6, d), (lambda i: (i // tiles_per_batch, 0, 0)) if per_batch
                               else (lambda i: (0, 0, 0))),
                  pl.BlockSpec((1, d), lambda i: (0, 0)),
                  pl.BlockSpec((d, pw), lambda i: (0, 0))],
        out_specs=pl.BlockSpec((tm, pw), lambda i: (i, 0)),
        out_shape=jax.ShapeDtypeStruct((t, pw), F32),
        compiler_params=_cparams(("arbitrary",)),
    )(x2d, modg, nrm0, w_in_p)


def _assemble_keys(k_nope, k_mid, k_last):
    parts = []
    for h in range(MLA_HEADS):
        parts += [k_nope[:, h * NOPE_DIM:(h + 1) * NOPE_DIM], k_mid, k_last]
    return jnp.concatenate(parts, axis=-1)


def _mla_prep_kernel(qlat_ref, kvlat_ref, sm_ref, t1_ref, t2_ref, t3_ref, gq_ref, wq_ref, gkv_ref, wk_ref,
                     wvt_ref, q_ref, k_ref, vt_ref, ckv_ref, kr_ref):
    t1 = t1_ref[...]
    t2 = t2_ref[...]
    t3 = t3_ref[...]
    qn = _rms(qlat_ref[...], gq_ref[...])
    raw = _dot(qn, wq_ref[...])
    w = raw.shape[1]
    tile = lambda t: jnp.concatenate([t] * MLA_HEADS, axis=-1)
    q = raw * tile(t1) + pltpu.roll(raw, w - ROPE_DIM, 1) * tile(t2) + pltpu.roll(raw, ROPE_DIM, 1) * tile(t3)
    q_ref[...] = (q * MLA_SCALE).astype(q_ref.dtype)

    ckv = _rms(kvlat_ref[...], gkv_ref[...])
    ckv_ref[...] = ckv
    k_nope = _dot(ckv, wk_ref[...])
    sm = sm_ref[...]
    k_rope = sm[:, SM_KROPE:SM_KROPE + ROPE_DIM]
    k_rot = sm[:, SM_KROT:SM_KROT + ROPE_DIM]
    kr_ref[...] = k_rope
    cs = slice(NOPE_DIM, NOPE_DIM + ROPE_DIM)
    kr = k_rope * t1[:, cs] + k_rot * t2[:, cs]
    k = _assemble_keys(k_nope, kr, jnp.zeros_like(kr))
    k_ref[...] = k.astype(k_ref.dtype)
    vt_ref[...] = _dot_nt(wvt_ref[...], ckv).astype(vt_ref.dtype)


def _mla_prep(proj, tabs, gq, wq_ext, gkv, wk, wvt, tiles_per_seq, tm):
    t = proj.shape[0]
    t1, t2, t3 = tabs
    per_pos = t1.shape[0] > tm
    tab_map = (lambda i: (i % tiles_per_seq, 0)) if per_pos else (lambda i: (0, 0))
    qk_w = MLA_HEADS * MLA_QK_PAD
    return pl.pallas_call(
        _mla_prep_kernel,
        grid=(t // tm,),
        in_specs=[pl.BlockSpec((tm, Q_LORA), lambda i: (i, COL_QLAT * LANES // Q_LORA)),
                  pl.BlockSpec((tm, KV_LORA), lambda i: (i, COL_KVLAT)),
                  pl.BlockSpec((tm, LANES), lambda i: (i, COL_SMALL)),
                  pl.BlockSpec((tm, LANES), tab_map),
                  pl.BlockSpec((tm, LANES), tab_map),
                  pl.BlockSpec((tm, LANES), tab_map),
                  pl.BlockSpec((1, Q_LORA), lambda i: (0, 0)),
                  pl.BlockSpec((Q_LORA, qk_w), lambda i: (0, 0)),
                  pl.BlockSpec((1, KV_LORA), lambda i: (0, 0)),
                  pl.BlockSpec((KV_LORA, MLA_HEADS * NOPE_DIM), lambda i: (0, 0)),
                  pl.BlockSpec((MLA_W, KV_LORA), lambda i: (0, 0))],
        out_specs=[pl.BlockSpec((tm, qk_w), lambda i: (i, 0)),
                   pl.BlockSpec((tm, qk_w), lambda i: (i, 0)),
                   pl.BlockSpec((MLA_W, tm), lambda i: (0, i)),
                   pl.BlockSpec((tm, KV_LORA), lambda i: (i, 0)),
                   pl.BlockSpec((tm, ROPE_DIM), lambda i: (i, 0))],
        out_shape=[jax.ShapeDtypeStruct((t, qk_w), BF16),
                   jax.ShapeDtypeStruct((t, qk_w), BF16),
                   jax.ShapeDtypeStruct((MLA_W, t), BF16),
                   jax.ShapeDtypeStruct((t, KV_LORA), F32),
                   jax.ShapeDtypeStruct((t, ROPE_DIM), F32)],
        compiler_params=_cparams(("arbitrary",)),
    )(proj, proj, proj, t1, t2, t3, gq, wq_ext, gkv, wk, wvt)


def _ctxkeys_kernel(ckv_ref, kr_ref, wk_ref, wvt_ref, k_ref, vt_ref):
    ckv = ckv_ref[...]
    kr = kr_ref[...]
    k = _assemble_keys(_dot(ckv, wk_ref[...]), jnp.zeros_like(kr), kr)
    k_ref[...] = k.astype(k_ref.dtype)
    vt_ref[...] = _dot_nt(wvt_ref[...], ckv).astype(vt_ref.dtype)


def _ctx_keys(ckv_c, kr_c, wk, wvt):
    t = ckv_c.shape[0]
    qk_w = MLA_HEADS * MLA_QK_PAD
    return pl.pallas_call(
        _ctxkeys_kernel,
        out_shape=[jax.ShapeDtypeStruct((t, qk_w), BF16), jax.ShapeDtypeStruct((MLA_W, t), BF16)],
        compiler_params=_cparams(),
    )(ckv_c, kr_c, wk, wvt)


def _attn_kernel(*refs, has_ctx):
    if has_ctx:
        q_ref, k_ref, vt_ref, kc_ref, vct_ref, o_ref = refs
    else:
        q_ref, k_ref, vt_ref, o_ref = refs
    heads = [slice(h * MLA_QK_PAD, (h + 1) * MLA_QK_PAD) for h in range(MLA_HEADS)]
    sts = [_dot_nt(k_ref[0, :, qs], q_ref[0, :, qs]) for qs in heads]
    if has_ctx:
        scts = [_dot_nt(kc_ref[0, :, qs], q_ref[0, :, qs]) for qs in heads]
    outs = []
    for h in range(MLA_HEADS):
        vs = slice(h * MLA_V_DIM, (h + 1) * MLA_V_DIM)
        st = sts[h]
        m = jnp.max(st, axis=0, keepdims=True)
        if has_ctx:
            sct = scts[h]
            m = jnp.maximum(m, jnp.max(sct, axis=0, keepdims=True))
        p = jnp.exp(st - m)
        den = jnp.sum(p, axis=0, keepdims=True)
        acc = _dot(vt_ref[vs, :], p)
        if has_ctx:
            pc = jnp.exp(sct - m)
            den = den + jnp.sum(pc, axis=0, keepdims=True)
            acc = acc + _dot(vct_ref[vs, :], pc)
        outs.append(acc / den)
    o_ref[0] = jnp.concatenate(outs, axis=0).T


def _attention(q, k, vt, kc, vct, tq):
    b, n, qk_w = q.shape
    has_ctx = kc is not None
    in_specs = [pl.BlockSpec((1, tq, qk_w), lambda bi, i: (bi, i, 0)),
                pl.BlockSpec((1, n, qk_w), lambda bi, i: (bi, 0, 0)),
                pl.BlockSpec((MLA_W, n), lambda bi, i: (0, bi))]
    args = [q, k, vt]
    if has_ctx:
        lc = kc.shape[1]
        in_specs += [pl.BlockSpec((1, lc, qk_w), lambda bi, i: (bi, 0, 0)),
                     pl.BlockSpec((MLA_W, lc), lambda bi, i: (0, bi))]
        args += [kc, vct]
    return pl.pallas_call(
        functools.partial(_attn_kernel, has_ctx=has_ctx),
        grid=(b, n // tq),
        in_specs=in_specs,
        out_specs=pl.BlockSpec((1, tq, MLA_W), lambda bi, i: (bi, i, 0)),
        out_shape=jax.ShapeDtypeStruct((b, n, MLA_W), F32),
        compiler_params=_cparams(("arbitrary", "arbitrary")),
    )(*args)


def _tri_masks():
    r = _iota2((CHUNK, CHUNK), 0)
    c = _iota2((CHUNK, CHUNK), 1)
    return r, c


def _out_norm_gate(o_f_ref, o_b_ref, z_ref, gn_ref, o_ref, n):
    bd = _head_block_ones(LANES) * (1.0 / HEAD_W)
    rows = min(n, 512)

    def blk(i, carry):
        r0 = pl.multiple_of(i * rows, rows)
        o = o_f_ref[pl.ds(r0, rows), :] + o_b_ref[pl.ds(r0, rows), :]
        ms = _dot_sel(o * o, bd)
        z = z_ref[0, pl.ds(r0, rows), :]
        o_ref[0, pl.ds(r0, rows), :] = o * lax.rsqrt(ms + EPS) * gn_ref[...] * _silu(z)
        return carry

    lax.fori_loop(0, n // rows, blk, 0)


def _gla_kernel(q_ref, k_ref, v_ref, z_ref, sm_ref, wg_ref, bg_ref, gn_ref, s0_ref,
                o_ref, st_ref, of_scr, ob_scr, st_scr, *, n):
    nc = n // CHUNK
    unroll = 4
    r, c = _tri_masks()
    tri = ((r >= c).astype(F32), (r <= c).astype(F32))
    lo_lanes = _iota2((CHUNK, LANES), 1) < HEAD_W
    r2 = _iota2((LANES, LANES), 0)
    c2 = _iota2((LANES, LANES), 1)
    same_head = (r2 // CHUNK) == (c2 // CHUNK)
    keep2 = (same_head & (r2 >= c2), same_head & (r2 <= c2))

    def stack2(x):
        return jnp.concatenate([jnp.where(lo_lanes, x, 0.0), jnp.where(lo_lanes, 0.0, x)], axis=0)

    def fold2(x):
        return x[:CHUNK] + x[CHUNK:]

    zeros = jnp.zeros((GLA_DV, GLA_DK), F32)
    for d in range(2):
        st_scr[d] = jnp.concatenate([jnp.concatenate([s0_ref[0, d, 0], zeros], axis=1),
                                     jnp.concatenate([zeros, s0_ref[0, d, 1]], axis=1)], axis=0)

    def body(it, carry):
        ch = []
        for j in range(unroll):
            for d in range(2):
                ci = it * unroll + j
                cc = ci if d == 0 else nc - 1 - ci
                rows = pl.ds(pl.multiple_of(cc * CHUNK, CHUNK), CHUNK)
                ch.append(dict(d=d, rows=rows, sm=sm_ref[0, rows, :], q=q_ref[0, rows, :], k=k_ref[0, rows, :],
                               v=v_ref[0, rows, :]))
        logits = [_dot(k["sm"], wg_ref[0, :, k["d"] * LANES:(k["d"] + 1) * LANES])
                  + bg_ref[0, :, k["d"] * LANES:(k["d"] + 1) * LANES] for k in ch]
        gs = [_sel_dot(tri[k["d"]], _log_sigmoid(lg) * (1.0 / GLA_TAU)) for k, lg in zip(ch, logits)]
        for k, g in zip(ch, gs):
            g_last = g[CHUNK - 1:CHUNK] if k["d"] == 0 else g[0:1]
            k["qd"] = (k["q"] * (GLA_DK ** -0.5) * jnp.exp(g)).astype(BF16)
            k["k_inv"] = k["k"] * jnp.exp(-g)
            k["k_end"] = k["k"] * jnp.exp(g_last - g)
            k["decay"] = jnp.exp(g_last)
        a_s = [jnp.where(keep2[k["d"]], _dot_nt(stack2(k["qd"].astype(F32)), stack2(k["k_inv"])), 0.0) for k in ch]
        intra = [fold2(_dot(a, stack2(k["v"]))) for k, a in zip(ch, a_s)]
        kvs = [jnp.where(same_head, _dot_tn(k["v"], k["k_end"]), 0.0) for k in ch]
        st = [st_scr[0], st_scr[1]]
        inter = []
        for k, kv in zip(ch, kvs):
            inter.append(_dot_nt(k["qd"], st[k["d"]]))
            st[k["d"]] = st[k["d"]] * k["decay"] + kv
        for k, oi, ox in zip(ch, intra, inter):
            if k["d"] == 0:
                of_scr[k["rows"], :] = oi + ox
            else:
                ob_scr[k["rows"], :] = oi + ox
        st_scr[0] = st[0]
        st_scr[1] = st[1]
        return carry

    lax.fori_loop(0, nc // unroll, body, 0)
    for d in range(2):
        st = st_scr[d]
        st_ref[0, d, 0] = st[:GLA_DV, :GLA_DK]
        st_ref[0, d, 1] = st[GLA_DV:, GLA_DK:]
    _out_norm_gate(of_scr, ob_scr, z_ref, gn_ref, o_ref, n)


def _gla(proj3, wg_p, bg_p, gn, s0t):
    b, n, _ = proj3.shape
    npair = GLA_HEADS // 2
    col = lambda base: (lambda bi, p: (bi, 0, base + p))
    s0r = _pair_major(s0t).reshape(b * npair, 2, 2, GLA_DV, GLA_DK)
    st_spec = pl.BlockSpec((1, 2, 2, GLA_DV, GLA_DK), lambda bi, p: (bi * npair + p, 0, 0, 0, 0))
    o, st = pl.pallas_call(
        functools.partial(_gla_kernel, n=n),
        grid=(b, npair),
        in_specs=[pl.BlockSpec((1, n, LANES), col(COL_GLA_Q)),
                  pl.BlockSpec((1, n, LANES), col(COL_GLA_K)),
                  pl.BlockSpec((1, n, LANES), col(COL_GLA_V)),
                  pl.BlockSpec((1, n, LANES), col(COL_GLA_Z)),
                  pl.BlockSpec((1, n, LANES), lambda bi, p: (bi, 0, COL_SMALL)),
                  pl.BlockSpec((1, LANES, 2 * LANES), lambda bi, p: (p, 0, 0)),
                  pl.BlockSpec((1, 1, 2 * LANES), lambda bi, p: (p, 0, 0)),
                  pl.BlockSpec((1, LANES), lambda bi, p: (0, 0)),
                  st_spec],
        out_specs=[pl.BlockSpec((1, n, LANES), lambda bi, p: (bi, 0, p)), st_spec],
        out_shape=[jax.ShapeDtypeStruct((b, n, GLA_W), F32),
                   jax.ShapeDtypeStruct(s0r.shape, F32)],
        scratch_shapes=[pltpu.VMEM((n, LANES), F32), pltpu.VMEM((n, LANES), F32),
                        pltpu.VMEM((2, 2 * GLA_DV, 2 * GLA_DK), F32)],
        compiler_params=_cparams(("arbitrary", "arbitrary")),
    )(proj3, proj3, proj3, proj3, proj3, wg_p, bg_p, gn, s0r)
    return o, st


def _gdn_kernel(q_ref, k_ref, v_ref, z_ref, sm_ref, cq_ref, ck_ref, cv_ref, alog_ref, dtb_ref,
                ex_ref, gn_ref, s0_ref,
                o_ref, st_ref, a_scr, b_scr, qe_scr, dec_scr, of_scr, ob_scr, st_scr, *, n):
    nc = n // CHUNK
    unroll = 4
    o_scr = (of_scr, ob_scr)
    r, c = _tri_masks()
    tri = ((r >= c).astype(F32), (r <= c).astype(F32))
    row_id = _iota2((CHUNK, LANES), 0)
    lane_id = _iota2((CHUNK, LANES), 1)
    lo_lanes = lane_id < HEAD_W
    head_ones = _head_block_ones(LANES)
    r2 = _iota2((LANES, LANES), 0)
    c2 = _iota2((LANES, LANES), 1)
    same_head = (r2 // CHUNK) == (c2 // CHUNK)
    strict2 = (same_head & (r2 > c2), same_head & (r2 < c2))
    diag2 = r2 == c2
    blk16 = (r2 // 16) == (c2 // 16)

    def conv_silu(x_ref, w_ref, r0, first, last):
        cur = x_ref[0, pl.ds(r0, CHUNK), :]
        prev_row = x_ref[0, pl.ds(jnp.maximum(r0 - 1, 0), 1), :] * jnp.where(first, 0.0, 1.0)
        next_row = x_ref[0, pl.ds(jnp.minimum(r0 + CHUNK, n - 1), 1), :] * jnp.where(last, 0.0, 1.0)
        prev = jnp.where(row_id == 0, prev_row, pltpu.roll(cur, 1, 0))
        nxt = jnp.where(row_id == CHUNK - 1, next_row, pltpu.roll(cur, CHUNK - 1, 0))
        return _silu(prev * w_ref[0:1, :] + cur * w_ref[1:2, :] + nxt * w_ref[2:3, :])

    def stack2(x):
        return jnp.concatenate([jnp.where(lo_lanes, x, 0.0), jnp.where(lo_lanes, 0.0, x)], axis=0)

    def fold2(x):
        return x[:CHUNK] + x[CHUNK:]

    def bcast2(x):
        sw = pltpu.roll(x, HEAD_W, 1)
        return jnp.concatenate([jnp.where(lo_lanes, x, sw), jnp.where(lo_lanes, sw, x)], axis=0)

    def prep(it, carry):
        chunks = []
        for j in range(unroll):
            cc = it * unroll + j
            r0 = pl.multiple_of(cc * CHUNK, CHUNK)
            first = cc == 0
            last = cc == nc - 1
            chunks.append(dict(cc=cc, rows=pl.ds(r0, CHUNK), sm=sm_ref[0, pl.ds(r0, CHUNK), :],
                               qc=conv_silu(q_ref, cq_ref, r0, first, last),
                               kc=conv_silu(k_ref, ck_ref, r0, first, last),
                               vc=conv_silu(v_ref, cv_ref, r0, first, last)))
        qss = [_dot_sel(c["qc"] * c["qc"], head_ones) for c in chunks]
        kss = [_dot_sel(c["kc"] * c["kc"], head_ones) for c in chunks]
        for c, qs, ks in zip(chunks, qss, kss):
            c["qn"] = c["qc"] * lax.rsqrt(qs + EPS) * (GDN_DK ** -0.5)
            c["kn"] = c["kc"] * lax.rsqrt(ks + EPS)
            c["k2"] = stack2(c["kn"]).astype(BF16)
            la = -jnp.exp(alog_ref[...]) * _softplus(c["sm"] + dtb_ref[...])
            c["gates"] = jnp.where(lane_id >= SM_A, la, jax.nn.sigmoid(c["sm"]))
        kks = [_dot_nt(c["k2"], c["k2"]) for c in chunks]
        qk0s = [_dot_nt(stack2(c["qn"]), c["k2"]) for c in chunks]
        bls = [_dot_sel(c["gates"], ex_ref[0]) for c in chunks]
        for c, kk, qk0, bl in zip(chunks, kks, qk0s, bls):
            c["kk"], c["qk0"], c["bl"] = kk, qk0, bl

        ch = [dict(c=c, d=d, beta_x=c["bl"][:, 2 * d * LANES:(2 * d + 1) * LANES])
              for c in chunks for d in range(2)]
        g_xs = [_sel_dot(tri[k["d"]], k["c"]["bl"][:, (2 * k["d"] + 1) * LANES:(2 * k["d"] + 2) * LANES])
                for k in ch]
        ones8 = jnp.ones((8, LANES), F32)
        g_rows = [_sel_dot_nt(ones8, jnp.concatenate([jnp.where(lane_id == 0, g, 0.0),
                                                      jnp.where(lane_id == HEAD_W, g, 0.0)], axis=0))[0:1]
                  for g in g_xs]
        for k, g_x, g_row in zip(ch, g_xs, g_rows):
            d, c = k["d"], k["c"]
            k["g_x"] = g_x
            k["g_last"] = g_x[CHUNK - 1:CHUNK] if d == 0 else g_x[0:1]
            diff = bcast2(g_x) - g_row
            k["e_strict"] = jnp.where(strict2[d], jnp.exp(jnp.where(strict2[d], diff, 0.0)), 0.0)
            m = bcast2(k["beta_x"]) * c["kk"] * k["e_strict"]
            k["md"] = jnp.where(blk16, m, 0.0)
            k["lo"] = m - k["md"]
            k["eg"] = jnp.exp(g_x)
            k["rhs"] = jnp.concatenate([stack2(c["vc"] * k["beta_x"]),
                                        stack2(c["kn"] * k["beta_x"] * k["eg"])], axis=1)
        ps = [-k["md"] for k in ch]
        nds = list(ps)
        for _ in range(3):
            ps = [_dot(p, p) for p in ps]
            ts = [_dot(nd, p) for nd, p in zip(nds, ps)]
            nds = [nd + p + t for nd, p, t in zip(nds, ps, ts)]
        ts = [_dot(nd, k["rhs"]) for nd, k in zip(nds, ch)]
        xs = [k["rhs"] + t for k, t in zip(ch, ts)]
        for _ in range(CHUNK // 16 - 1):
            ts = [_dot(k["lo"], x) for k, x in zip(ch, xs)]
            ys = [k["rhs"] - t for k, t in zip(ch, ts)]
            ts = [_dot(nd, y) for nd, y in zip(nds, ys)]
            xs = [y + t for y, t in zip(ys, ts)]
        uws = [jnp.concatenate([fold2(x[:, :LANES]), fold2(x[:, LANES:])], axis=1) for x in xs]
        k_ends = [k["c"]["kn"] * jnp.exp(k["g_last"] - k["g_x"]) for k in ch]
        abs_ = [_dot_tn(uw, ke) for uw, ke in zip(uws, k_ends)]
        qus = [_dot(k["c"]["qk0"] * jnp.where(diag2, 1.0, k["e_strict"]), x) for k, x in zip(ch, xs)]
        for k, ab, qu in zip(ch, abs_, qus):
            d, c = k["d"], k["c"]
            blk = pl.ds(pl.multiple_of(c["cc"] * LANES, LANES), LANES)
            b_scr[d, blk, :] = jnp.where(same_head, ab[:LANES], 0.0)
            a_scr[d, blk, :] = jnp.where(same_head, ab[LANES:], 0.0).astype(BF16)
            qe_scr[d, c["rows"], :] = (c["qn"] * k["eg"] - fold2(qu[:, LANES:])).astype(BF16)
            o_scr[d][c["rows"], :] = fold2(qu[:, :LANES])
            dec_scr[d, pl.ds(pl.multiple_of(c["cc"] * 8, 8), 8), :] = jnp.broadcast_to(
                jnp.exp(k["g_last"]), (8, LANES))
        return carry

    lax.fori_loop(0, nc // unroll, prep, 0)

    zeros = jnp.zeros((GDN_DV, GDN_DK), F32)
    for d in range(2):
        st_scr[d] = jnp.concatenate([jnp.concatenate([s0_ref[0, d, 0], zeros], axis=1),
                                     jnp.concatenate([zeros, s0_ref[0, d, 1]], axis=1)], axis=0)

    def step(ci, carry):
        loaded = []
        for d in range(2):
            cc = ci if d == 0 else nc - 1 - ci
            rows = pl.ds(pl.multiple_of(cc * CHUNK, CHUNK), CHUNK)
            blk = pl.ds(pl.multiple_of(cc * LANES, LANES), LANES)
            loaded.append((rows, st_scr[d], a_scr[d, blk, :], b_scr[d, blk, :], qe_scr[d, rows, :],
                           o_scr[d][rows, :], dec_scr[d, pl.ds(pl.multiple_of(cc * 8, 8), 1), :]))
        ts = [_dot(st, a) for _, st, a, _, _, _, _ in loaded]
        qs = [_dot_nt(qe, st) for _, st, _, _, qe, _, _ in loaded]
        for d, ((rows, st, _, b, _, o_part, decay), t, q) in enumerate(zip(loaded, ts, qs)):
            st_scr[d] = st * decay - t + b
            o_scr[d][rows, :] = o_part + q
        return carry

    lax.fori_loop(0, nc, step, 0)
    for d in range(2):
        st = st_scr[d]
        st_ref[0, d, 0] = st[:GDN_DV, :GDN_DK]
        st_ref[0, d, 1] = st[GDN_DV:, GDN_DK:]
    _out_norm_gate(of_scr, ob_scr, z_ref, gn_ref, o_ref, n)


def _gdn_select_constants():
    npair = GDN_HEADS // 2
    ex = np.zeros((npair, LANES, 4 * LANES), np.float32)
    for p in range(npair):
        for d in range(2):
            for h in range(2):
                hh = 2 * p + h
                lanes_b = slice(2 * d * LANES + h * HEAD_W, 2 * d * LANES + (h + 1) * HEAD_W)
                lanes_g = slice((2 * d + 1) * LANES + h * HEAD_W, (2 * d + 1) * LANES + (h + 1) * HEAD_W)
                ex[p, SM_BETA + d * GDN_HEADS + hh, lanes_b] = 1.0
                ex[p, SM_A + d * GDN_HEADS + hh, lanes_g] = 1.0
    return jnp.asarray(ex)


def _gdn(proj3, conv_w, alog_row, dtb_row, gn, s0t):
    b, n, _ = proj3.shape
    npair = GDN_HEADS // 2
    ex = _gdn_select_constants()
    nc = n // CHUNK
    col = lambda base: (lambda bi, p: (bi, 0, base + p))
    cw = GDN_HEADS * GDN_DK
    s0r = _pair_major(s0t).reshape(b * npair, 2, 2, GDN_DV, GDN_DK)
    st_spec = pl.BlockSpec((1, 2, 2, GDN_DV, GDN_DK), lambda bi, p: (bi * npair + p, 0, 0, 0, 0))
    pair_spec = lambda shp: pl.BlockSpec((1,) + shp, lambda bi, p: (p, 0, 0))
    o, st = pl.pallas_call(
        functools.partial(_gdn_kernel, n=n),
        grid=(b, npair),
        in_specs=[pl.BlockSpec((1, n, LANES), col(COL_GDN_Q)),
                  pl.BlockSpec((1, n, LANES), col(COL_GDN_K)),
                  pl.BlockSpec((1, n, LANES), col(COL_GDN_V)),
                  pl.BlockSpec((1, n, LANES), col(COL_GDN_Z)),
                  pl.BlockSpec((1, n, LANES), lambda bi, p: (bi, 0, COL_SMALL)),
                  pl.BlockSpec((3, LANES), lambda bi, p: (0, p)),
                  pl.BlockSpec((3, LANES), lambda bi, p: (0, cw // LANES + p)),
                  pl.BlockSpec((3, LANES), lambda bi, p: (0, 2 * cw // LANES + p)),
                  pl.BlockSpec((1, LANES), lambda bi, p: (0, 0)),
                  pl.BlockSpec((1, LANES), lambda bi, p: (0, 0)),
                  pair_spec((LANES, 4 * LANES)),
                  pl.BlockSpec((1, LANES), lambda bi, p: (0, 0)),
                  st_spec],
        out_specs=[pl.BlockSpec((1, n, LANES), lambda bi, p: (bi, 0, p)), st_spec],
        out_shape=[jax.ShapeDtypeStruct((b, n, GDN_W), F32),
                   jax.ShapeDtypeStruct(s0r.shape, F32)],
        scratch_shapes=[pltpu.VMEM((2, nc * LANES, LANES), BF16),
                        pltpu.VMEM((2, nc * LANES, LANES), F32),
                        pltpu.VMEM((2, n, LANES), BF16),
                        pltpu.VMEM((2, nc * 8, LANES), F32),
                        pltpu.VMEM((n, LANES), F32), pltpu.VMEM((n, LANES), F32),
                        pltpu.VMEM((2, 2 * GDN_DV, 2 * GDN_DK), F32)],
        compiler_params=_cparams(("arbitrary", "arbitrary")),
    )(proj3, proj3, proj3, proj3, proj3, conv_w, conv_w, conv_w, alog_row, dtb_row, ex, gn, s0r)
    return o, st


def _outproj_kernel(omla_ref, ogla_ref, ogdn_ref, x_ref, mod_ref, nrm_ref, wout_ref, wr_ref,
                    x1_ref, xm_ref, afft_ref):
    y = (_dot(omla_ref[...], wout_ref[0:MLA_W, :])
         + _dot(ogla_ref[...], wout_ref[MLA_W:MLA_W + GLA_W, :])
         + _dot(ogdn_ref[...], wout_ref[MLA_W + GLA_W:, :]))
    m = mod_ref[0]
    x1 = x_ref[...] + m[2:3] * _rms(y, nrm_ref[1:2, :])
    x1_ref[...] = x1
    xm2 = _rms(x1, nrm_ref[2:3, :]) * (1.0 + m[4:5]) + m[3:4]
    d = xm2.shape[1]
    xm_ref[:, :d] = xm2
    logits = _dot_x3(xm2, wr_ref[...])
    lane = _iota2(logits.shape, 1)
    logits = jnp.where(lane < N_EXPERTS, logits, -jnp.inf)
    e = jnp.exp(logits - jnp.max(logits, axis=-1, keepdims=True))
    aff = e / jnp.sum(e, axis=-1, keepdims=True)
    xm_ref[:, d:] = aff
    sel = (_iota2((N_EXPERTS, LANES), 0) == _iota2((N_EXPERTS, LANES), 1)).astype(F32)
    afft_ref[...] = _sel_dot_nt(sel, aff, pieces=3)


def _out_projection(o_mla, o_gla, o_gdn, x2d, modg, nrm, w_out, w_router_p, tiles_per_batch, tm):
    t, d = x2d.shape
    per_batch = modg.shape[0] > 1
    return pl.pallas_call(
        _outproj_kernel,
        grid=(t // tm,),
        in_specs=[pl.BlockSpec((tm, MLA_W), lambda i: (i, 0)),
                  pl.BlockSpec((tm, GLA_W), lambda i: (i, 0)),
                  pl.BlockSpec((tm, GDN_W), lambda i: (i, 0)),
                  pl.BlockSpec((tm, d), lambda i: (i, 0)),
                  pl.BlockSpec((1, 6, d), (lambda i: (i // tiles_per_batch, 0, 0)) if per_batch
                               else (lambda i: (0, 0, 0))),
                  pl.BlockSpec((4, d), lambda i: (0, 0)),
                  pl.BlockSpec((d, d), lambda i: (0, 0)),
                  pl.BlockSpec((d, LANES), lambda i: (0, 0))],
        out_specs=[pl.BlockSpec((tm, d), lambda i: (i, 0)),
                   pl.BlockSpec((tm, d + LANES), lambda i: (i, 0)),
                   pl.BlockSpec((N_EXPERTS, tm), lambda i: (0, i))],
        out_shape=[jax.ShapeDtypeStruct((t, d), F32),
                   jax.ShapeDtypeStruct((t, d + LANES), F32),
                   jax.ShapeDtypeStruct((N_EXPERTS, t), F32)],
        compiler_params=_cparams(("arbitrary",)),
    )(o_mla, o_gla, o_gdn, x2d, modg, nrm, w_out, w_router_p)


def _route_kernel(afft_ref, rank_ref, pos_ref, *, cap):
    a = afft_ref[...]
    t = a.shape[1]

    def search(i, cur):
        cand = cur | jnp.left_shift(jnp.int32(1), 30 - i)
        cnt = jnp.sum((a >= lax.bitcast_convert_type(cand, F32)).astype(F32), axis=1, keepdims=True)
        return jnp.where(cnt >= cap, cand, cur)

    kth = lax.bitcast_convert_type(lax.fori_loop(0, 31, search, jnp.zeros((a.shape[0], 1), I32)), F32)
    gt = (a > kth).astype(F32)
    eq = (a == kth).astype(F32)
    need = cap - jnp.sum(gt, axis=1, keepdims=True)
    before = (_iota2((LANES, LANES), 0) < _iota2((LANES, LANES), 1)).astype(BF16)
    nb = t // LANES
    carry_eq = jnp.zeros((a.shape[0], 1), F32)
    carry = jnp.zeros((a.shape[0], 1), F32)
    for b in range(nb):
        sl = slice(b * LANES, (b + 1) * LANES)
        eqb = eq[:, sl]
        pre = jnp.dot(eqb.astype(BF16), before, preferred_element_type=F32) + carry_eq
        selb = jnp.maximum(gt[:, sl], jnp.where(pre < need, eqb, 0.0))
        carry_eq = carry_eq + jnp.sum(eqb, axis=1, keepdims=True)
        local = jnp.dot(selb.astype(BF16), before, preferred_element_type=F32)
        rank_ref[:, sl] = jnp.where(selb > 0.0, local, -1.0)
        pos_ref[:, sl] = local + carry
        carry = carry + jnp.sum(selb, axis=1, keepdims=True)


def _route(afft, cap):
    e, t = afft.shape
    return pl.pallas_call(
        functools.partial(_route_kernel, cap=cap),
        out_shape=[jax.ShapeDtypeStruct((e, t), F32), jax.ShapeDtypeStruct((e, t), F32)],
        compiler_params=_cparams(),
    )(afft)


def _compact_kernel(off_ref, rank_ref, idx_ref, *, nb):
    e = pl.program_id(0)
    idx_ref[...] = jnp.zeros(idx_ref.shape, I32)
    slot = _iota2((LANES, LANES), 0)
    tok = _iota2((LANES, LANES), 0).astype(BF16)
    group = 4

    def blks(g, carry):
        bs = [g * group + j for j in range(group)]
        onehots = [(rank_ref[0, pl.ds(b, 1), :].astype(I32) == slot).astype(BF16) for b in bs]
        ids = [jnp.dot(oh, tok, preferred_element_type=F32).astype(I32) + b * LANES for oh, b in zip(onehots, bs)]
        for b, v in zip(bs, ids):
            idx_ref[0, pl.ds(off_ref[e, b], LANES), :] = v
        return carry

    lax.fori_loop(0, nb // group, blks, 0)


def _compact(off, rank, cap):
    e, t = rank.shape
    nb = t // LANES
    rows = cap + LANES
    idx = pl.pallas_call(
        functools.partial(_compact_kernel, nb=nb),
        grid_spec=pltpu.PrefetchScalarGridSpec(
            num_scalar_prefetch=1,
            grid=(e,),
            in_specs=[pl.BlockSpec((1, nb, LANES), lambda ei, off_r: (ei, 0, 0))],
            out_specs=pl.BlockSpec((1, rows, LANES), lambda ei, off_r: (ei, 0, 0))),
        out_shape=jax.ShapeDtypeStruct((e, rows, LANES), I32),
        compiler_params=_cparams(("arbitrary",)),
    )(off, rank.reshape(e, nb, LANES))
    return idx[:, :cap, 0]


def _gather_copy(x_hbm, xe_scr, sem, slot, src_row, dst_row, nrows):
    return pltpu.make_async_copy(x_hbm.at[pl.ds(src_row, nrows)], xe_scr.at[slot, pl.ds(dst_row, nrows)],
                                 sem.at[slot])


def _expert_kernel(idx_ref, x_hbm, w1_ref, w3_ref, w2_ref, ye_ref, xe_scr, xb_scr, sem, *, cap):
    e = pl.program_id(0)
    ne = pl.num_programs(0)

    slot = e % 2
    nxt = (e + 1) % ne
    nxt_slot = 1 - slot
    d = xb_scr.shape[1]

    @pl.when(e == 0)
    def _():
        def row(s, carry):
            _gather_copy(x_hbm, xe_scr, sem, 0, idx_ref[0, s], s, 1).start()
            return carry
        lax.fori_loop(0, cap, row, 0)

    _gather_copy(x_hbm, xe_scr, sem, slot, 0, 0, cap).wait()
    xb_scr[...] = xe_scr[slot, :, :d].astype(BF16)
    ge = xe_scr[slot, :, d:]
    gate = jnp.sum(jnp.where(_iota2(ge.shape, 1) == e, ge, 0.0), axis=-1, keepdims=True)

    for s in range(cap):
        _gather_copy(x_hbm, xe_scr, sem, nxt_slot, idx_ref[nxt, s], s, 1).start()

    ff = w1_ref.shape[2]
    fchunk = 512
    y = jnp.zeros((cap, w2_ref.shape[2]), F32)
    for j in range(ff // fchunk):
        fs = slice(j * fchunk, (j + 1) * fchunk)
        h1 = _dot(xb_scr[...], w1_ref[0, :, fs])
        h3 = _dot(xb_scr[...], w3_ref[0, :, fs])
        y = y + _dot(_silu(h1) * h3, w2_ref[0, fs, :])
    ye_ref[0] = y * gate

    @pl.when(e == ne - 1)
    def _():
        _gather_copy(x_hbm, xe_scr, sem, nxt_slot, 0, 0, cap).wait()


def _expert_ffn(idx, xm, w1, w3, w2, layer, cap):
    _, e, d, ff = w1.shape
    wspec = lambda shp: pl.BlockSpec((None, 1) + shp, lambda ei, idx_r: (layer, ei, 0, 0))
    return pl.pallas_call(
        functools.partial(_expert_kernel, cap=cap),
        grid_spec=pltpu.PrefetchScalarGridSpec(
            num_scalar_prefetch=1,
            grid=(e,),
            in_specs=[pl.BlockSpec(memory_space=pl.ANY),
                      wspec((d, ff)), wspec((d, ff)), wspec((ff, d))],
            out_specs=pl.BlockSpec((1, cap, d), lambda ei, idx_r: (ei, 0, 0)),
            scratch_shapes=[pltpu.VMEM((2, cap, xm.shape[1]), F32), pltpu.VMEM((cap, d), BF16),
                            pltpu.SemaphoreType.DMA((2,))]),
        out_shape=jax.ShapeDtypeStruct((e, cap, d), F32),
        compiler_params=_cparams(("arbitrary",)),
    )(idx, xm, w1, w3, w2)


def _scatter_kernel(idx_ref, lo_ref, ye_ref, x1_ref, mod_ref, nrm_ref, o_ref, *, part_rows):
    part = pl.program_id(0)
    e = pl.program_id(1)

    @pl.when(e == 0)
    def _():
        o_ref[...] = jnp.zeros(o_ref.shape, F32)

    base = part * part_rows

    lo = lo_ref[e, part]
    hi = lo_ref[e, part + 1]
    group = 8
    ngroups = (hi - lo) // group

    def add_group(g, carry):
        s0 = lo + g * group
        ts = [idx_ref[e, s0 + j] - base for j in range(group)]
        new = [o_ref[pl.ds(t, 1), :] + ye_ref[0, pl.ds(s0 + j, 1), :] for j, t in enumerate(ts)]
        for t, v in zip(ts, new):
            o_ref[pl.ds(t, 1), :] = v
        return carry

    lax.fori_loop(0, ngroups, add_group, 0)

    def add(s, carry):
        t = idx_ref[e, s] - base
        o_ref[pl.ds(t, 1), :] = o_ref[pl.ds(t, 1), :] + ye_ref[0, pl.ds(s, 1), :]
        return carry

    lax.fori_loop(lo + ngroups * group, hi, add, 0)

    @pl.when(e == pl.num_programs(1) - 1)
    def _():
        m = mod_ref[0]
        rows = 512

        def blk(i, carry):
            rs = pl.ds(pl.multiple_of(i * rows, rows), rows)
            o_ref[rs, :] = x1_ref[rs, :] + m[5:6] * _rms(o_ref[rs, :], nrm_ref[3:4, :])
            return carry

        lax.fori_loop(0, part_rows // rows, blk, 0)


def _scatter_residual(idx, bounds, ye, x1, modg, nrm, parts_per_batch, nparts):
    e, cap, d = ye.shape
    t = x1.shape[0]
    part_rows = t // nparts
    per_batch = modg.shape[0] > 1
    return pl.pallas_call(
        functools.partial(_scatter_kernel, part_rows=part_rows),
        grid_spec=pltpu.PrefetchScalarGridSpec(
            num_scalar_prefetch=2,
            grid=(nparts, e),
            in_specs=[pl.BlockSpec((1, cap, d), lambda pi, ei, a, b: (ei, 0, 0)),
                      pl.BlockSpec((part_rows, d), lambda pi, ei, a, b: (pi, 0)),
                      pl.BlockSpec((1, 6, d), (lambda pi, ei, a, b: (pi // parts_per_batch, 0, 0)) if per_batch
                                   else (lambda pi, ei, a, b: (0, 0, 0))),
                      pl.BlockSpec((4, d), lambda pi, ei, a, b: (0, 0))],
            out_specs=pl.BlockSpec((part_rows, d), lambda pi, ei, a, b: (pi, 0))),
        out_shape=jax.ShapeDtypeStruct((t, d), F32),
        compiler_params=_cparams(("arbitrary", "arbitrary")),
    )(idx, bounds, ye, x1, modg, nrm)


def _rot_cols(w):
    s = w.shape[:-1]
    w2 = w.reshape(s + (ROPE_DIM // 2, 2))
    return jnp.stack([-w2[..., 1], w2[..., 0]], axis=-1).reshape(s + (ROPE_DIM,))


def _layer_weights(l, w_in, mla_q_norm, mla_wq_b, mla_kv_norm, mla_wkv_b, gla_wg, gla_bg, gla_out_norm,
                   gdn_conv, gdn_a_log, gdn_dt_bias, gdn_out_norm, w_out, sandwich_norms, w_router,
                   w_e1, w_e3, w_e2):
    wi = w_in[l]
    o = 0
    parts = {}
    for name, sz in (("q_lat", Q_LORA), ("kv_lat", KV_LORA), ("k_rope", ROPE_DIM),
                     ("gla_q", GLA_W), ("gla_k", GLA_W), ("gla_v", GLA_W), ("gla_glr", 2 * GLA_GATE_RANK),
                     ("gla_z", GLA_W), ("gdn_q", GDN_W), ("gdn_k", GDN_W), ("gdn_v", GDN_W), ("gdn_z", GDN_W),
                     ("gdn_b", 2 * GDN_HEADS), ("gdn_a", 2 * GDN_HEADS)):
        parts[name] = wi[:, o:o + sz]
        o += sz
    w_in_p = jnp.concatenate(
        [parts[k] for k in ("q_lat", "kv_lat", "gla_q", "gla_k", "gla_v", "gla_z",
                            "gdn_q", "gdn_k", "gdn_v", "gdn_z")]
        + [parts["k_rope"], _rot_cols(parts["k_rope"]), parts["gla_glr"], parts["gdn_b"], parts["gdn_a"]],
        axis=1).astype(BF16)

    wq = mla_wq_b[l].reshape(Q_LORA, MLA_HEADS, NOPE_DIM + ROPE_DIM)
    wq_ext = jnp.concatenate([wq, _rot_cols(wq[..., NOPE_DIM:])], axis=-1).reshape(Q_LORA, -1).astype(BF16)
    wkv = mla_wkv_b[l].reshape(KV_LORA, MLA_HEADS, NOPE_DIM + MLA_V_DIM)
    wk = wkv[..., :NOPE_DIM].reshape(KV_LORA, -1).astype(BF16)
    wvt = wkv[..., NOPE_DIM:].reshape(KV_LORA, -1).T.astype(BF16)

    npair = GLA_HEADS // 2
    wg_p = jnp.zeros((npair, LANES, 2 * LANES), F32)
    bg_p = jnp.zeros((npair, 1, 2 * LANES), F32)
    for p in range(npair):
        for d in range(2):
            rows = slice(SM_GLR + d * GLA_GATE_RANK, SM_GLR + (d + 1) * GLA_GATE_RANK)
            wg_p = wg_p.at[p, rows, d * LANES:(d + 1) * LANES].set(gla_wg[l, d][:, p * LANES:(p + 1) * LANES])
            bg_p = bg_p.at[p, 0, d * LANES:(d + 1) * LANES].set(gla_bg[l, d][p * LANES:(p + 1) * LANES])

    alog_row = jnp.zeros((1, LANES), F32).at[0, SM_A:].set(gdn_a_log[l].reshape(-1))
    dtb_row = jnp.zeros((1, LANES), F32).at[0, SM_A:].set(gdn_dt_bias[l].reshape(-1))
    w_router_p = jnp.zeros((D_MODEL, LANES), F32).at[:, :N_EXPERTS].set(w_router[l])
    return dict(
        w_in=w_in_p, gq=mla_q_norm[l][None], wq=wq_ext, gkv=mla_kv_norm[l][None], wk=wk, wvt=wvt, layer=l,
        wg=wg_p.astype(BF16), bg=bg_p, gla_gn=jnp.tile(gla_out_norm[l], 2)[None],
        conv=gdn_conv[l], alog=alog_row, dtb=dtb_row, gdn_gn=jnp.tile(gdn_out_norm[l], 2)[None],
        w_out=w_out[l].astype(BF16), nrm=sandwich_norms[l], w_router=w_router_p,
        w1=w_e1, w3=w_e3, w2=w_e2)


def _rope_tables(n, rotate):
    ones = np.ones((1, NOPE_DIM), np.float32)
    if not rotate:
        t1 = np.concatenate([ones, np.ones((1, ROPE_DIM), np.float32), np.zeros((1, ROPE_DIM), np.float32)], 1)
        z = np.zeros_like(t1)
        return tuple(jnp.asarray(np.broadcast_to(t, (n, LANES)).copy()) for t in (t1, z, z))
    rows = n // GRID_W
    row = jnp.repeat(jnp.arange(rows), GRID_W).astype(F32)
    col = jnp.tile(jnp.arange(GRID_W), rows).astype(F32)
    half = ROPE_DIM // 2
    freqs = ROPE_BASE ** (-jnp.arange(0, half, 2, dtype=F32) / half)
    ang = jnp.concatenate([row[:, None] * freqs, col[:, None] * freqs], axis=-1)
    cos = jnp.repeat(jnp.cos(ang), 2, axis=-1)
    sin = jnp.repeat(jnp.sin(ang), 2, axis=-1)
    z32 = jnp.zeros((n, ROPE_DIM), F32)
    t1 = jnp.concatenate([jnp.ones((n, NOPE_DIM), F32), cos, z32], axis=1)
    t2 = jnp.concatenate([jnp.zeros((n, NOPE_DIM), F32), sin, z32], axis=1)
    t3 = jnp.concatenate([jnp.zeros((n, NOPE_DIM), F32), z32, jnp.ones((n, ROPE_DIM), F32)], axis=1)
    return t1, t2, t3


def _trunk_layer(x, lw, modg, ctx_cache, tabs):
    b, n, d = x.shape
    t = b * n
    tm = min(512, n)
    tiles_per_batch = n // tm
    x2d = x.reshape(t, d)

    proj = _in_projection(x2d, modg, lw["nrm"][0:1], lw["w_in"], tiles_per_batch, tm)
    proj3 = proj.reshape(b, n, PROJ_W)

    q, k, vt, ckv, k_rope = _mla_prep(proj, tabs, lw["gq"], lw["wq"], lw["gkv"], lw["wk"], lw["wvt"],
                                      tiles_per_batch, tm)
    qk_w = MLA_HEADS * MLA_QK_PAD
    if ctx_cache is None:
        kc = vct = None
        s0_gla = jnp.zeros((b, 2, GLA_HEADS, GLA_DV, GLA_DK), F32)
        s0_gdn = jnp.zeros((b, 2, GDN_HEADS, GDN_DV, GDN_DK), F32)
    else:
        ckv_c, kr_c, s0_gla, s0_gdn = ctx_cache
        lc = ckv_c.shape[1]
        kc, vct = _ctx_keys(ckv_c.reshape(b * lc, KV_LORA), kr_c.reshape(b * lc, ROPE_DIM), lw["wk"], lw["wvt"])
        kc = kc.reshape(b, lc, qk_w)
        s0_gla = jnp.swapaxes(s0_gla, -1, -2)
        s0_gdn = jnp.swapaxes(s0_gdn, -1, -2)
    o_mla = _attention(q.reshape(b, n, qk_w), k.reshape(b, n, qk_w), vt, kc, vct, tq=min(256, n))

    o_gla, st_gla = _gla(proj3, lw["wg"], lw["bg"], lw["gla_gn"], s0_gla)
    o_gdn, st_gdn = _gdn(proj3, lw["conv"], lw["alog"], lw["dtb"], lw["gdn_gn"], s0_gdn)

    x1, xm2, afft = _out_projection(o_mla.reshape(t, MLA_W), o_gla.reshape(t, GLA_W),
                                        o_gdn.reshape(t, GDN_W), x2d, modg, lw["nrm"], lw["w_out"],
                                        lw["w_router"], tiles_per_batch, tm)

    cap = EC_CAPACITY * t // N_EXPERTS
    rank, pos = _route(afft, cap)
    off = pos[:, ::LANES].astype(I32)
    idx = _compact(off, rank, cap)
    ye = _expert_ffn(idx, xm2, lw["w1"], lw["w3"], lw["w2"], lw["layer"], cap)
    nparts = 4 if t >= 4 * 512 else 1
    part_rows = t // nparts
    bounds = jnp.concatenate([off[:, ::part_rows // LANES], jnp.full((N_EXPERTS, 1), cap, I32)], axis=1)
    x2 = _scatter_residual(idx, bounds, ye, x1, modg, lw["nrm"], max(n // part_rows, 1), nparts)

    new = None
    if ctx_cache is None:
        st_gla = jnp.swapaxes(st_gla.reshape(b, GLA_HEADS // 2, 2, 2, GLA_DV, GLA_DK), 1, 2)
        st_gdn = jnp.swapaxes(st_gdn.reshape(b, GDN_HEADS // 2, 2, 2, GDN_DV, GDN_DK), 1, 2)
        new = (ckv.reshape(b, n, KV_LORA), k_rope.reshape(b, n, ROPE_DIM),
               jnp.swapaxes(st_gla.reshape(b, 2, GLA_HEADS, GLA_DV, GLA_DK), -1, -2),
               jnp.swapaxes(st_gdn.reshape(b, 2, GDN_HEADS, GDN_DV, GDN_DK), -1, -2))
    return x2.reshape(b, n, d), new


def _pair_major(s0):
    b, two, h = s0.shape[:3]
    return jnp.swapaxes(s0.reshape(b, 2, h // 2, 2, s0.shape[-2], s0.shape[-1]), 1, 2)


def kernel(x_prompt, x_sample, cache_ckv, cache_krope, state_gla, state_gdn, c, c_ctx, w_in, mla_q_norm,
           mla_wq_b, mla_kv_norm, mla_wkv_b, gla_wg, gla_bg, gla_out_norm, gdn_conv, gdn_a_log, gdn_dt_bias,
           gdn_out_norm, w_out, w_mod, b_mod, sandwich_norms, w_router, w_e1, w_e3, w_e2):
    depth = w_in.shape[0]
    nb_s = x_sample.shape[0]
    cond = jnp.zeros((8, D_MODEL), F32).at[0].set(c_ctx).at[1:1 + nb_s].set(c)
    mod = _modulation(cond, w_mod, b_mod)

    tabs_ctx = _rope_tables(8, rotate=False)
    tabs_smp = _rope_tables(x_sample.shape[1], rotate=True)
    tabs_ctx = tuple(jnp.broadcast_to(t[:1], (min(512, x_prompt.shape[1]), LANES)) for t in tabs_ctx)

    lws = [_layer_weights(l, w_in, mla_q_norm, mla_wq_b, mla_kv_norm, mla_wkv_b, gla_wg, gla_bg, gla_out_norm,
                          gdn_conv, gdn_a_log, gdn_dt_bias, gdn_out_norm, w_out, sandwich_norms, w_router,
                          w_e1, w_e3, w_e2) for l in range(depth)]

    xp = x_prompt
    ckv_l, kr_l, sg_l, sd_l = [], [], [], []
    for l in range(depth):
        modg = mod[l, 0:1].reshape(1, 6, D_MODEL)
        xp, (ckv, kr, sg, sd) = _trunk_layer(xp, lws[l], modg, None, tabs_ctx)
        ckv_l.append(ckv)
        kr_l.append(kr)
        sg_l.append(sg)
        sd_l.append(sd)

    xs = x_sample
    for l in range(depth):
        modg = mod[l, 1:1 + nb_s].reshape(nb_s, 6, D_MODEL)
        xs, _ = _trunk_layer(xs, lws[l], modg,
                             (cache_ckv[:, l], cache_krope[:, l], state_gla[:, l], state_gdn[:, l]), tabs_smp)

    return (xp, xs, jnp.stack(ckv_l, axis=1), jnp.stack(kr_l, axis=1),
            jnp.stack(sg_l, axis=1), jnp.stack(sd_l, axis=1))
```

```python
import functools

import numpy as np
import jax
import jax.numpy as jnp
from jax import lax
from jax.experimental import pallas as pl
from jax.experimental.pallas import tpu as pltpu

F32 = jnp.float32
BF16 = jnp.bfloat16
I32 = jnp.int32

D_MODEL = 1024
DEPTH = 2
GRID_W = 64
EPS = 1e-6
CHUNK = 64

MLA_HEADS = 4
Q_LORA = 256
KV_LORA = 128
NOPE_DIM = 64
ROPE_DIM = 32
MLA_V_DIM = 64
ROPE_BASE = 10000.0
MLA_SCALE = (NOPE_DIM + ROPE_DIM) ** -0.5
MLA_QK_PAD = 128

GLA_HEADS = 4
GLA_DK = 64
GLA_DV = 64
GLA_GATE_RANK = 16
GLA_TAU = 16.0

GDN_HEADS = 8
GDN_DK = 64
GDN_DV = 64

N_EXPERTS = 16
EXPERT_FF = 1024
EC_CAPACITY = 2

LANES = 128
HEAD_W = 64
MLA_W = MLA_HEADS * MLA_V_DIM
GLA_W = GLA_HEADS * GLA_DV
GDN_W = GDN_HEADS * GDN_DV

COL_QLAT = 0
COL_KVLAT = 2
COL_GLA_Q = 3
COL_GLA_K = 5
COL_GLA_V = 7
COL_GLA_Z = 9
COL_GDN_Q = 11
COL_GDN_K = 15
COL_GDN_V = 19
COL_GDN_Z = 23
COL_SMALL = 27
PROJ_W = 28 * LANES
SM_KROPE = 0
SM_KROT = 32
SM_GLR = 64
SM_BETA = 96
SM_A = 112

VMEM_LIMIT = 56 * 1024 * 1024


def _cparams(sem=None, **kw):
    return pltpu.CompilerParams(dimension_semantics=sem, vmem_limit_bytes=VMEM_LIMIT, **kw)


def _dot(a, b):
    return jnp.dot(a.astype(BF16), b.astype(BF16), preferred_element_type=F32)


def _dot_nt(a, b):
    return lax.dot_general(a.astype(BF16), b.astype(BF16), (((1,), (1,)), ((), ())),
                           preferred_element_type=F32)


def _dot_tn(a, b):
    return lax.dot_general(a.astype(BF16), b.astype(BF16), (((0,), (0,)), ((), ())),
                           preferred_element_type=F32)


def _dot_x3(a, b):
    a1 = a.astype(BF16)
    a2 = (a - a1.astype(F32)).astype(BF16)
    b1 = b.astype(BF16)
    b2 = (b - b1.astype(F32)).astype(BF16)
    return (jnp.dot(a1, b1, preferred_element_type=F32) + jnp.dot(a1, b2, preferred_element_type=F32)
            + jnp.dot(a2, b1, preferred_element_type=F32))


def _split(x, pieces):
    out = []
    for _ in range(pieces):
        p = x.astype(BF16)
        out.append(p)
        x = x - p.astype(F32)
    return out


def _dot_sel(x, sel, pieces=2):
    s = sel.astype(BF16)
    return sum(jnp.dot(p, s, preferred_element_type=F32) for p in _split(x, pieces))


def _sel_dot(sel, x, pieces=2):
    s = sel.astype(BF16)
    return sum(jnp.dot(s, p, preferred_element_type=F32) for p in _split(x, pieces))


def _sel_dot_nt(sel, x, pieces=2):
    s = sel.astype(BF16)
    dn = (((1,), (1,)), ((), ()))
    return sum(lax.dot_general(s, p, dn, preferred_element_type=F32) for p in _split(x, pieces))


def _silu(x):
    return x * jax.nn.sigmoid(x)


def _softplus(x):
    return jnp.maximum(x, 0.0) + jnp.log(1.0 + jnp.exp(-jnp.abs(x)))


def _log_sigmoid(x):
    return -_softplus(-x)


def _rms(x, g):
    return x * lax.rsqrt(jnp.mean(x * x, axis=-1, keepdims=True) + EPS) * g


def _iota2(shape, dim):
    return lax.broadcasted_iota(I32, shape, dim)


def _head_block_ones(width):
    r = _iota2((width, width), 0) // HEAD_W
    c = _iota2((width, width), 1) // HEAD_W
    return (r == c).astype(F32)


def _mod_kernel(cond_ref, w_ref, b_ref, o_ref):
    s = _silu(cond_ref[...])
    o_ref[0] = _dot(s, w_ref[0]) + b_ref[0]


def _modulation(cond, w_mod, b_mod):
    depth, d, d6 = w_mod.shape
    nj = d6 // d
    return pl.pallas_call(
        _mod_kernel,
        grid=(depth, nj),
        in_specs=[pl.BlockSpec((8, d), lambda l, j: (0, 0)),
                  pl.BlockSpec((1, d, d), lambda l, j: (l, 0, j)),
                  pl.BlockSpec((1, 1, d), lambda l, j: (l, 0, j))],
        out_specs=pl.BlockSpec((1, 8, d), lambda l, j: (l, 0, j)),
        out_shape=jax.ShapeDtypeStruct((depth, 8, d6), F32),
        compiler_params=_cparams(("arbitrary", "arbitrary")),
    )(cond, w_mod, b_mod.reshape(depth, 1, d6))


def _inproj_kernel(x_ref, mod_ref, nrm_ref, w_ref, o_ref, *, col_chunk):
    x = x_ref[...]
    m = mod_ref[0]
    xm = (_rms(x, nrm_ref[...]) * (1.0 + m[1:2]) + m[0:1]).astype(BF16)
    for j in range(o_ref.shape[1] // col_chunk):
        sl = slice(j * col_chunk, (j + 1) * col_chunk)
        o_ref[:, sl] = jnp.dot(xm, w_ref[:, sl], preferred_element_type=F32)


def _in_projection(x2d, modg, nrm0, w_in_p, tiles_per_batch, tm):
    t, d = x2d.shape
    pw = w_in_p.shape[1]
    per_batch = modg.shape[0] > 1
    return pl.pallas_call(
        functools.partial(_inproj_kernel, col_chunk=512),
        grid=(t // tm,),
        in_specs=[pl.BlockSpec((tm, d), lambda i: (i, 0)),
                  pl.BlockSpec((1, 6, d), (lambda i: (i // tiles_per_batch, 0, 0)) if per_batch
                               else (lambda i: (0, 0, 0))),
                  pl.BlockSpec((1, d), lambda i: (0, 0)),
                  pl.BlockSpec((d, pw), lambda i: (0, 0))],
        out_specs=pl.BlockSpec((tm, pw), lambda i: (i, 0)),
        out_shape=jax.ShapeDtypeStruct((t, pw), F32),
        compiler_params=_cparams(("arbitrary",)),
    )(x2d, modg, nrm0, w_in_p)


def _assemble_keys(k_nope, k_mid, k_last):
    parts = []
    for h in range(MLA_HEADS):
        parts += [k_nope[:, h * NOPE_DIM:(h + 1) * NOPE_DIM], k_mid, k_last]
    return jnp.concatenate(parts, axis=-1)


def _mla_prep_kernel(qlat_ref, kvlat_ref, sm_ref, t1_ref, t2_ref, t3_ref, gq_ref, wq_ref, gkv_ref, wk_ref,
                     wvt_ref, q_ref, k_ref, vt_ref, ckv_ref, kr_ref):
    t1 = t1_ref[...]
    t2 = t2_ref[...]
    t3 = t3_ref[...]
    qn = _rms(qlat_ref[...], gq_ref[...])
    raw = _dot(qn, wq_ref[...])
    w = raw.shape[1]
    tile = lambda t: jnp.concatenate([t] * MLA_HEADS, axis=-1)
    q = raw * tile(t1) + pltpu.roll(raw, w - ROPE_DIM, 1) * tile(t2) + pltpu.roll(raw, ROPE_DIM, 1) * tile(t3)
    q_ref[...] = (q * MLA_SCALE).astype(q_ref.dtype)

    ckv = _rms(kvlat_ref[...], gkv_ref[...])
    ckv_ref[...] = ckv
    k_nope = _dot(ckv, wk_ref[...])
    sm = sm_ref[...]
    k_rope = sm[:, SM_KROPE:SM_KROPE + ROPE_DIM]
    k_rot = sm[:, SM_KROT:SM_KROT + ROPE_DIM]
    kr_ref[...] = k_rope
    cs = slice(NOPE_DIM, NOPE_DIM + ROPE_DIM)
    kr = k_rope * t1[:, cs] + k_rot * t2[:, cs]
    k = _assemble_keys(k_nope, kr, jnp.zeros_like(kr))
    k_ref[...] = k.astype(k_ref.dtype)
    vt_ref[...] = _dot_nt(wvt_ref[...], ckv).astype(vt_ref.dtype)


def _mla_prep(proj, tabs, gq, wq_ext, gkv, wk, wvt, tiles_per_seq, tm):
    t = proj.shape[0]
    t1, t2, t3 = tabs
    per_pos = t1.shape[0] > tm
    tab_map = (lambda i: (i % tiles_per_seq, 0)) if per_pos else (lambda i: (0, 0))
    qk_w = MLA_HEADS * MLA_QK_PAD
    return pl.pallas_call(
        _mla_prep_kernel,
        grid=(t // tm,),
        in_specs=[pl.BlockSpec((tm, Q_LORA), lambda i: (i, COL_QLAT * LANES // Q_LORA)),
                  pl.BlockSpec((tm, KV_LORA), lambda i: (i, COL_KVLAT)),
                  pl.BlockSpec((tm, LANES), lambda i: (i, COL_SMALL)),
                  pl.BlockSpec((tm, LANES), tab_map),
                  pl.BlockSpec((tm, LANES), tab_map),
                  pl.BlockSpec((tm, LANES), tab_map),
                  pl.BlockSpec((1, Q_LORA), lambda i: (0, 0)),
                  pl.BlockSpec((Q_LORA, qk_w), lambda i: (0, 0)),
                  pl.BlockSpec((1, KV_LORA), lambda i: (0, 0)),
                  pl.BlockSpec((KV_LORA, MLA_HEADS * NOPE_DIM), lambda i: (0, 0)),
                  pl.BlockSpec((MLA_W, KV_LORA), lambda i: (0, 0))],
        out_specs=[pl.BlockSpec((tm, qk_w), lambda i: (i, 0)),
                   pl.BlockSpec((tm, qk_w), lambda i: (i, 0)),
                   pl.BlockSpec((MLA_W, tm), lambda i: (0, i)),
                   pl.BlockSpec((tm, KV_LORA), lambda i: (i, 0)),
                   pl.BlockSpec((tm, ROPE_DIM), lambda i: (i, 0))],
        out_shape=[jax.ShapeDtypeStruct((t, qk_w), BF16),
                   jax.ShapeDtypeStruct((t, qk_w), BF16),
                   jax.ShapeDtypeStruct((MLA_W, t), BF16),
                   jax.ShapeDtypeStruct((t, KV_LORA), F32),
                   jax.ShapeDtypeStruct((t, ROPE_DIM), F32)],
        compiler_params=_cparams(("arbitrary",)),
    )(proj, proj, proj, t1, t2, t3, gq, wq_ext, gkv, wk, wvt)


def _ctxkeys_kernel(ckv_ref, kr_ref, wk_ref, wvt_ref, k_ref, vt_ref):
    ckv = ckv_ref[...]
    kr = kr_ref[...]
    k = _assemble_keys(_dot(ckv, wk_ref[...]), jnp.zeros_like(kr), kr)
    k_ref[...] = k.astype(k_ref.dtype)
    vt_ref[...] = _dot_nt(wvt_ref[...], ckv).astype(vt_ref.dtype)


def _ctx_keys(ckv_c, kr_c, wk, wvt):
    t = ckv_c.shape[0]
    qk_w = MLA_HEADS * MLA_QK_PAD
    return pl.pallas_call(
        _ctxkeys_kernel,
        out_shape=[jax.ShapeDtypeStruct((t, qk_w), BF16), jax.ShapeDtypeStruct((MLA_W, t), BF16)],
        compiler_params=_cparams(),
    )(ckv_c, kr_c, wk, wvt)


def _attn_kernel(*refs, has_ctx):
    if has_ctx:
        q_ref, k_ref, vt_ref, kc_ref, vct_ref, o_ref = refs
    else:
        q_ref, k_ref, vt_ref, o_ref = refs
    heads = [slice(h * MLA_QK_PAD, (h + 1) * MLA_QK_PAD) for h in range(MLA_HEADS)]
    sts = [_dot_nt(k_ref[0, :, qs], q_ref[0, :, qs]) for qs in heads]
    if has_ctx:
        scts = [_dot_nt(kc_ref[0, :, qs], q_ref[0, :, qs]) for qs in heads]
    outs = []
    for h in range(MLA_HEADS):
        vs = slice(h * MLA_V_DIM, (h + 1) * MLA_V_DIM)
        st = sts[h]
        m = jnp.max(st, axis=0, keepdims=True)
        if has_ctx:
            sct = scts[h]
            m = jnp.maximum(m, jnp.max(sct, axis=0, keepdims=True))
        p = jnp.exp(st - m)
        den = jnp.sum(p, axis=0, keepdims=True)
        acc = _dot(vt_ref[vs, :], p)
        if has_ctx:
            pc = jnp.exp(sct - m)
            den = den + jnp.sum(pc, axis=0, keepdims=True)
            acc = acc + _dot(vct_ref[vs, :], pc)
        outs.append(acc / den)
    o_ref[0] = jnp.concatenate(outs, axis=0).T


def _attention(q, k, vt, kc, vct, tq):
    b, n, qk_w = q.shape
    has_ctx = kc is not None
    in_specs = [pl.BlockSpec((1, tq, qk_w), lambda bi, i: (bi, i, 0)),
                pl.BlockSpec((1, n, qk_w), lambda bi, i: (bi, 0, 0)),
                pl.BlockSpec((MLA_W, n), lambda bi, i: (0, bi))]
    args = [q, k, vt]
    if has_ctx:
        lc = kc.shape[1]
        in_specs += [pl.BlockSpec((1, lc, qk_w), lambda bi, i: (bi, 0, 0)),
                     pl.BlockSpec((MLA_W, lc), lambda bi, i: (0, bi))]
        args += [kc, vct]
    return pl.pallas_call(
        functools.partial(_attn_kernel, has_ctx=has_ctx),
        grid=(b, n // tq),
        in_specs=in_specs,
        out_specs=pl.BlockSpec((1, tq, MLA_W), lambda bi, i: (bi, i, 0)),
        out_shape=jax.ShapeDtypeStruct((b, n, MLA_W), F32),
        compiler_params=_cparams(("arbitrary", "arbitrary")),
    )(*args)


def _tri_masks():
    r = _iota2((CHUNK, CHUNK), 0)
    c = _iota2((CHUNK, CHUNK), 1)
    return r, c


def _out_norm_gate(o_f_ref, o_b_ref, z_ref, gn_ref, o_ref, n):
    bd = _head_block_ones(LANES) * (1.0 / HEAD_W)
    rows = min(n, 512)

    def blk(i, carry):
        r0 = pl.multiple_of(i * rows, rows)
        o = o_f_ref[pl.ds(r0, rows), :] + o_b_ref[pl.ds(r0, rows), :]
        ms = _dot_sel(o * o, bd)
        z = z_ref[0, pl.ds(r0, rows), :]
        o_ref[0, pl.ds(r0, rows), :] = o * lax.rsqrt(ms + EPS) * gn_ref[...] * _silu(z)
        return carry

    lax.fori_loop(0, n // rows, blk, 0)


def _gla_kernel(q_ref, k_ref, v_ref, z_ref, sm_ref, wg_ref, bg_ref, gn_ref, s0_ref,
                o_ref, st_ref, of_scr, ob_scr, st_scr, *, n):
    nc = n // CHUNK
    unroll = 4
    r, c = _tri_masks()
    tri = ((r >= c).astype(F32), (r <= c).astype(F32))
    lo_lanes = _iota2((CHUNK, LANES), 1) < HEAD_W
    r2 = _iota2((LANES, LANES), 0)
    c2 = _iota2((LANES, LANES), 1)
    same_head = (r2 // CHUNK) == (c2 // CHUNK)
    keep2 = (same_head & (r2 >= c2), same_head & (r2 <= c2))

    def stack2(x):
        return jnp.concatenate([jnp.where(lo_lanes, x, 0.0), jnp.where(lo_lanes, 0.0, x)], axis=0)

    def fold2(x):
        return x[:CHUNK] + x[CHUNK:]

    zeros = jnp.zeros((GLA_DV, GLA_DK), F32)
    for d in range(2):
        st_scr[d] = jnp.concatenate([jnp.concatenate([s0_ref[0, d, 0], zeros], axis=1),
                                     jnp.concatenate([zeros, s0_ref[0, d, 1]], axis=1)], axis=0)

    def body(it, carry):
        ch = []
        for j in range(unroll):
            for d in range(2):
                ci = it * unroll + j
                cc = ci if d == 0 else nc - 1 - ci
                rows = pl.ds(pl.multiple_of(cc * CHUNK, CHUNK), CHUNK)
                ch.append(dict(d=d, rows=rows, sm=sm_ref[0, rows, :], q=q_ref[0, rows, :], k=k_ref[0, rows, :],
                               v=v_ref[0, rows, :]))
        logits = [_dot(k["sm"], wg_ref[0, :, k["d"] * LANES:(k["d"] + 1) * LANES])
                  + bg_ref[0, :, k["d"] * LANES:(k["d"] + 1) * LANES] for k in ch]
        gs = [_sel_dot(tri[k["d"]], _log_sigmoid(lg) * (1.0 / GLA_TAU)) for k, lg in zip(ch, logits)]
        for k, g in zip(ch, gs):
            g_last = g[CHUNK - 1:CHUNK] if k["d"] == 0 else g[0:1]
            k["qd"] = (k["q"] * (GLA_DK ** -0.5) * jnp.exp(g)).astype(BF16)
            k["k_inv"] = k["k"] * jnp.exp(-g)
            k["k_end"] = k["k"] * jnp.exp(g_last - g)
            k["decay"] = jnp.exp(g_last)
        a_s = [jnp.where(keep2[k["d"]], _dot_nt(stack2(k["qd"].astype(F32)), stack2(k["k_inv"])), 0.0) for k in ch]
        intra = [fold2(_dot(a, stack2(k["v"]))) for k, a in zip(ch, a_s)]
        kvs = [jnp.where(same_head, _dot_tn(k["v"], k["k_end"]), 0.0) for k in ch]
        st = [st_scr[0], st_scr[1]]
        inter = []
        for k, kv in zip(ch, kvs):
            inter.append(_dot_nt(k["qd"], st[k["d"]]))
            st[k["d"]] = st[k["d"]] * k["decay"] + kv
        for k, oi, ox in zip(ch, intra, inter):
            if k["d"] == 0:
                of_scr[k["rows"], :] = oi + ox
            else:
                ob_scr[k["rows"], :] = oi + ox
        st_scr[0] = st[0]
        st_scr[1] = st[1]
        return carry

    lax.fori_loop(0, nc // unroll, body, 0)
    for d in range(2):
        st = st_scr[d]
        st_ref[0, d, 0] = st[:GLA_DV, :GLA_DK]
        st_ref[0, d, 1] = st[GLA_DV:, GLA_DK:]
    _out_norm_gate(of_scr, ob_scr, z_ref, gn_ref, o_ref, n)


def _gla(proj3, wg_p, bg_p, gn, s0t):
    b, n, _ = proj3.shape
    npair = GLA_HEADS // 2
    col = lambda base: (lambda bi, p: (bi, 0, base + p))
    s0r = _pair_major(s0t).reshape(b * npair, 2, 2, GLA_DV, GLA_DK)
    st_spec = pl.BlockSpec((1, 2, 2, GLA_DV, GLA_DK), lambda bi, p: (bi * npair + p, 0, 0, 0, 0))
    o, st = pl.pallas_call(
        functools.partial(_gla_kernel, n=n),
        grid=(b, npair),
        in_specs=[pl.BlockSpec((1, n, LANES), col(COL_GLA_Q)),
                  pl.BlockSpec((1, n, LANES), col(COL_GLA_K)),
                  pl.BlockSpec((1, n, LANES), col(COL_GLA_V)),
                  pl.BlockSpec((1, n, LANES), col(COL_GLA_Z)),
                  pl.BlockSpec((1, n, LANES), lambda bi, p: (bi, 0, COL_SMALL)),
                  pl.BlockSpec((1, LANES, 2 * LANES), lambda bi, p: (p, 0, 0)),
                  pl.BlockSpec((1, 1, 2 * LANES), lambda bi, p: (p, 0, 0)),
                  pl.BlockSpec((1, LANES), lambda bi, p: (0, 0)),
                  st_spec],
        out_specs=[pl.BlockSpec((1, n, LANES), lambda bi, p: (bi, 0, p)), st_spec],
        out_shape=[jax.ShapeDtypeStruct((b, n, GLA_W), F32),
                   jax.ShapeDtypeStruct(s0r.shape, F32)],
        scratch_shapes=[pltpu.VMEM((n, LANES), F32), pltpu.VMEM((n, LANES), F32),
                        pltpu.VMEM((2, 2 * GLA_DV, 2 * GLA_DK), F32)],
        compiler_params=_cparams(("arbitrary", "arbitrary")),
    )(proj3, proj3, proj3, proj3, proj3, wg_p, bg_p, gn, s0r)
    return o, st


def _gdn_kernel(q_ref, k_ref, v_ref, z_ref, sm_ref, cq_ref, ck_ref, cv_ref, alog_ref, dtb_ref,
                ex_ref, gn_ref, s0_ref,
                o_ref, st_ref, a_scr, b_scr, qe_scr, dec_scr, of_scr, ob_scr, st_scr, *, n):
    nc = n // CHUNK
    unroll = 4
    o_scr = (of_scr, ob_scr)
    r, c = _tri_masks()
    tri = ((r >= c).astype(F32), (r <= c).astype(F32))
    row_id = _iota2((CHUNK, LANES), 0)
    lane_id = _iota2((CHUNK, LANES), 1)
    lo_lanes = lane_id < HEAD_W
    head_ones = _head_block_ones(LANES)
    r2 = _iota2((LANES, LANES), 0)
    c2 = _iota2((LANES, LANES), 1)
    same_head = (r2 // CHUNK) == (c2 // CHUNK)
    strict2 = (same_head & (r2 > c2), same_head & (r2 < c2))
    diag2 = r2 == c2
    blk16 = (r2 // 16) == (c2 // 16)

    def conv_silu(x_ref, w_ref, r0, first, last):
        cur = x_ref[0, pl.ds(r0, CHUNK), :]
        prev_row = x_ref[0, pl.ds(jnp.maximum(r0 - 1, 0), 1), :] * jnp.where(first, 0.0, 1.0)
        next_row = x_ref[0, pl.ds(jnp.minimum(r0 + CHUNK, n - 1), 1), :] * jnp.where(last, 0.0, 1.0)
        prev = jnp.where(row_id == 0, prev_row, pltpu.roll(cur, 1, 0))
        nxt = jnp.where(row_id == CHUNK - 1, next_row, pltpu.roll(cur, CHUNK - 1, 0))
        return _silu(prev * w_ref[0:1, :] + cur * w_ref[1:2, :] + nxt * w_ref[2:3, :])

    def stack2(x):
        return jnp.concatenate([jnp.where(lo_lanes, x, 0.0), jnp.where(lo_lanes, 0.0, x)], axis=0)

    def fold2(x):
        return x[:CHUNK] + x[CHUNK:]

    def bcast2(x):
        sw = pltpu.roll(x, HEAD_W, 1)
        return jnp.concatenate([jnp.where(lo_lanes, x, sw), jnp.where(lo_lanes, sw, x)], axis=0)

    def prep(it, carry):
        chunks = []
        for j in range(unroll):
            cc = it * unroll + j
            r0 = pl.multiple_of(cc * CHUNK, CHUNK)
            first = cc == 0
            last = cc == nc - 1
            chunks.append(dict(cc=cc, rows=pl.ds(r0, CHUNK), sm=sm_ref[0, pl.ds(r0, CHUNK), :],
                               qc=conv_silu(q_ref, cq_ref, r0, first, last),
                               kc=conv_silu(k_ref, ck_ref, r0, first, last),
                               vc=conv_silu(v_ref, cv_ref, r0, first, last)))
        qss = [_dot_sel(c["qc"] * c["qc"], head_ones) for c in chunks]
        kss = [_dot_sel(c["kc"] * c["kc"], head_ones) for c in chunks]
        for c, qs, ks in zip(chunks, qss, kss):
            c["qn"] = c["qc"] * lax.rsqrt(qs + EPS) * (GDN_DK ** -0.5)
            c["kn"] = c["kc"] * lax.rsqrt(ks + EPS)
            c["k2"] = stack2(c["kn"]).astype(BF16)
            la = -jnp.exp(alog_ref[...]) * _softplus(c["sm"] + dtb_ref[...])
            c["gates"] = jnp.where(lane_id >= SM_A, la, jax.nn.sigmoid(c["sm"]))
        kks = [_dot_nt(c["k2"], c["k2"]) for c in chunks]
        qk0s = [_dot_nt(stack2(c["qn"]), c["k2"]) for c in chunks]
        bls = [_dot_sel(c["gates"], ex_ref[0]) for c in chunks]
        for c, kk, qk0, bl in zip(chunks, kks, qk0s, bls):
            c["kk"], c["qk0"], c["bl"] = kk, qk0, bl

        ch = [dict(c=c, d=d, beta_x=c["bl"][:, 2 * d * LANES:(2 * d + 1) * LANES])
              for c in chunks for d in range(2)]
        g_xs = [_sel_dot(tri[k["d"]], k["c"]["bl"][:, (2 * k["d"] + 1) * LANES:(2 * k["d"] + 2) * LANES])
                for k in ch]
        ones8 = jnp.ones((8, LANES), F32)
        g_rows = [_sel_dot_nt(ones8, jnp.concatenate([jnp.where(lane_id == 0, g, 0.0),
                                                      jnp.where(lane_id == HEAD_W, g, 0.0)], axis=0))[0:1]
                  for g in g_xs]
        for k, g_x, g_row in zip(ch, g_xs, g_rows):
            d, c = k["d"], k["c"]
            k["g_x"] = g_x
            k["g_last"] = g_x[CHUNK - 1:CHUNK] if d == 0 else g_x[0:1]
            diff = bcast2(g_x) - g_row
            k["e_strict"] = jnp.where(strict2[d], jnp.exp(jnp.where(strict2[d], diff, 0.0)), 0.0)
            m = bcast2(k["beta_x"]) * c["kk"] * k["e_strict"]
            k["md"] = jnp.where(blk16, m, 0.0)
            k["lo"] = m - k["md"]
            k["eg"] = jnp.exp(g_x)
            k["rhs"] = jnp.concatenate([stack2(c["vc"] * k["beta_x"]),
                                        stack2(c["kn"] * k["beta_x"] * k["eg"])], axis=1)
        ps = [-k["md"] for k in ch]
        nds = list(ps)
        for _ in range(3):
            ps = [_dot(p, p) for p in ps]
            ts = [_dot(nd, p) for nd, p in zip(nds, ps)]
            nds = [nd + p + t for nd, p, t in zip(nds, ps, ts)]
        ts = [_dot(nd, k["rhs"]) for nd, k in zip(nds, ch)]
        x0s = [k["rhs"] + t for k, t in zip(ch, ts)]
        ts = [_dot(nd, k["lo"]) for nd, k in zip(nds, ch)]
        zs = [(k["lo"] + t).astype(BF16) for k, t in zip(ch, ts)]
        xs = x0s
        for _ in range(CHUNK // 16 - 1):
            ts = [_dot(z, x) for z, x in zip(zs, xs)]
            xs = [x0 - t for x0, t in zip(x0s, ts)]
        uws = [jnp.concatenate([fold2(x[:, :LANES]), fold2(x[:, LANES:])], axis=1) for x in xs]
        k_ends = [k["c"]["kn"] * jnp.exp(k["g_last"] - k["g_x"]) for k in ch]
        abs_ = [_dot_tn(ke, uw) for uw, ke in zip(uws, k_ends)]
        qus = [_dot(k["c"]["qk0"] * jnp.where(diag2, 1.0, k["e_strict"]), x) for k, x in zip(ch, xs)]
        for k, ab, qu in zip(ch, abs_, qus):
            d, c = k["d"], k["c"]
            blk = pl.ds(pl.multiple_of(c["cc"] * LANES, LANES), LANES)
            b_scr[d, blk, :] = jnp.where(same_head, ab[:, :LANES], 0.0)
            a_scr[d, blk, :] = jnp.where(same_head, ab[:, LANES:], 0.0).astype(BF16)
            qe_scr[d, c["rows"], :] = (c["qn"] * k["eg"] - fold2(qu[:, LANES:])).astype(BF16)
            o_scr[d][c["rows"], :] = fold2(qu[:, :LANES])
            dec_scr[d, pl.ds(pl.multiple_of(c["cc"] * 8, 8), 8), :] = jnp.broadcast_to(
                jnp.exp(k["g_last"]), (8, LANES))
        return carry

    lax.fori_loop(0, nc // unroll, prep, 0)

    zeros = jnp.zeros((GDN_DV, GDN_DK), F32)
    for d in range(2):
        st_scr[d] = jnp.concatenate([jnp.concatenate([s0_ref[0, d, 0], zeros], axis=1),
                                     jnp.concatenate([zeros, s0_ref[0, d, 1]], axis=1)], axis=0)

    def step(ci, carry):
        loaded = []
        for d in range(2):
            cc = ci if d == 0 else nc - 1 - ci
            rows = pl.ds(pl.multiple_of(cc * CHUNK, CHUNK), CHUNK)
            blk = pl.ds(pl.multiple_of(cc * LANES, LANES), LANES)
            loaded.append((rows, st_scr[d], a_scr[d, blk, :], b_scr[d, blk, :], qe_scr[d, rows, :],
                           o_scr[d][rows, :], dec_scr[d, pl.ds(pl.multiple_of(cc * 8, 8), 1), :]))
        ts = [_dot(a, st) for _, st, a, _, _, _, _ in loaded]
        qs = [_dot(qe, st) for _, st, _, _, qe, _, _ in loaded]
        for d, ((rows, st, _, b, _, o_part, decay), t, q) in enumerate(zip(loaded, ts, qs)):
            decay_rows = jnp.concatenate([jnp.broadcast_to(decay[:, 0:1], (GDN_DK, LANES)),
                                          jnp.broadcast_to(decay[:, HEAD_W:HEAD_W + 1], (GDN_DK, LANES))], axis=0)
            st_scr[d] = st * decay_rows - t + b
            o_scr[d][rows, :] = o_part + q
        return carry

    lax.fori_loop(0, nc, step, 0)
    for d in range(2):
        st = st_scr[d]
        st_ref[0, d, 0] = st[:GDN_DV, :GDN_DK]
        st_ref[0, d, 1] = st[GDN_DV:, GDN_DK:]
    _out_norm_gate(of_scr, ob_scr, z_ref, gn_ref, o_ref, n)


def _gdn_select_constants():
    npair = GDN_HEADS // 2
    ex = np.zeros((npair, LANES, 4 * LANES), np.float32)
    for p in range(npair):
        for d in range(2):
            for h in range(2):
                hh = 2 * p + h
                lanes_b = slice(2 * d * LANES + h * HEAD_W, 2 * d * LANES + (h + 1) * HEAD_W)
                lanes_g = slice((2 * d + 1) * LANES + h * HEAD_W, (2 * d + 1) * LANES + (h + 1) * HEAD_W)
                ex[p, SM_BETA + d * GDN_HEADS + hh, lanes_b] = 1.0
                ex[p, SM_A + d * GDN_HEADS + hh, lanes_g] = 1.0
    return jnp.asarray(ex)


def _gdn(proj3, conv_w, alog_row, dtb_row, gn, s0t):
    b, n, _ = proj3.shape
    npair = GDN_HEADS // 2
    ex = _gdn_select_constants()
    nc = n // CHUNK
    col = lambda base: (lambda bi, p: (bi, 0, base + p))
    cw = GDN_HEADS * GDN_DK
    s0r = _pair_major(s0t).reshape(b * npair, 2, 2, GDN_DV, GDN_DK)
    st_spec = pl.BlockSpec((1, 2, 2, GDN_DV, GDN_DK), lambda bi, p: (bi * npair + p, 0, 0, 0, 0))
    pair_spec = lambda shp: pl.BlockSpec((1,) + shp, lambda bi, p: (p, 0, 0))
    o, st = pl.pallas_call(
        functools.partial(_gdn_kernel, n=n),
        grid=(b, npair),
        in_specs=[pl.BlockSpec((1, n, LANES), col(COL_GDN_Q)),
                  pl.BlockSpec((1, n, LANES), col(COL_GDN_K)),
                  pl.BlockSpec((1, n, LANES), col(COL_GDN_V)),
                  pl.BlockSpec((1, n, LANES), col(COL_GDN_Z)),
                  pl.BlockSpec((1, n, LANES), lambda bi, p: (bi, 0, COL_SMALL)),
                  pl.BlockSpec((3, LANES), lambda bi, p: (0, p)),
                  pl.BlockSpec((3, LANES), lambda bi, p: (0, cw // LANES + p)),
                  pl.BlockSpec((3, LANES), lambda bi, p: (0, 2 * cw // LANES + p)),
                  pl.BlockSpec((1, LANES), lambda bi, p: (0, 0)),
                  pl.BlockSpec((1, LANES), lambda bi, p: (0, 0)),
                  pair_spec((LANES, 4 * LANES)),
                  pl.BlockSpec((1, LANES), lambda bi, p: (0, 0)),
                  st_spec],
        out_specs=[pl.BlockSpec((1, n, LANES), lambda bi, p: (bi, 0, p)), st_spec],
        out_shape=[jax.ShapeDtypeStruct((b, n, GDN_W), F32),
                   jax.ShapeDtypeStruct(s0r.shape, F32)],
        scratch_shapes=[pltpu.VMEM((2, nc * LANES, LANES), BF16),
                        pltpu.VMEM((2, nc * LANES, LANES), F32),
                        pltpu.VMEM((2, n, LANES), BF16),
                        pltpu.VMEM((2, nc * 8, LANES), F32),
                        pltpu.VMEM((n, LANES), F32), pltpu.VMEM((n, LANES), F32),
                        pltpu.VMEM((2, 2 * GDN_DV, 2 * GDN_DK), F32)],
        compiler_params=_cparams(("arbitrary", "arbitrary")),
    )(proj3, proj3, proj3, proj3, proj3, conv_w, conv_w, conv_w, alog_row, dtb_row, ex, gn, s0r)
    return o, st


def _outproj_kernel(omla_ref, ogla_ref, ogdn_ref, x_ref, mod_ref, nrm_ref, wout_ref, wr_ref,
                    x1_ref, xm_ref, afft_ref):
    y = (_dot(omla_ref[...], wout_ref[0:MLA_W, :])
         + _dot(ogla_ref[...], wout_ref[MLA_W:MLA_W + GLA_W, :])
         + _dot(ogdn_ref[...], wout_ref[MLA_W + GLA_W:, :]))
    m = mod_ref[0]
    x1 = x_ref[...] + m[2:3] * _rms(y, nrm_ref[1:2, :])
    x1_ref[...] = x1
    xm2 = _rms(x1, nrm_ref[2:3, :]) * (1.0 + m[4:5]) + m[3:4]
    d = xm2.shape[1]
    xm_ref[:, :d] = xm2
    logits = _dot_x3(xm2, wr_ref[...])
    lane = _iota2(logits.shape, 1)
    logits = jnp.where(lane < N_EXPERTS, logits, -jnp.inf)
    e = jnp.exp(logits - jnp.max(logits, axis=-1, keepdims=True))
    aff = e / jnp.sum(e, axis=-1, keepdims=True)
    xm_ref[:, d:] = aff
    sel = (_iota2((N_EXPERTS, LANES), 0) == _iota2((N_EXPERTS, LANES), 1)).astype(F32)
    afft_ref[...] = _sel_dot_nt(sel, aff, pieces=3)


def _out_projection(o_mla, o_gla, o_gdn, x2d, modg, nrm, w_out, w_router_p, tiles_per_batch, tm):
    t, d = x2d.shape
    per_batch = modg.shape[0] > 1
    return pl.pallas_call(
        _outproj_kernel,
        grid=(t // tm,),
        in_specs=[pl.BlockSpec((tm, MLA_W), lambda i: (i, 0)),
                  pl.BlockSpec((tm, GLA_W), lambda i: (i, 0)),
                  pl.BlockSpec((tm, GDN_W), lambda i: (i, 0)),
                  pl.BlockSpec((tm, d), lambda i: (i, 0)),
                  pl.BlockSpec((1, 6, d), (lambda i: (i // tiles_per_batch, 0, 0)) if per_batch
                               else (lambda i: (0, 0, 0))),
                  pl.BlockSpec((4, d), lambda i: (0, 0)),
                  pl.BlockSpec((d, d), lambda i: (0, 0)),
                  pl.BlockSpec((d, LANES), lambda i: (0, 0))],
        out_specs=[pl.BlockSpec((tm, d), lambda i: (i, 0)),
                   pl.BlockSpec((tm, d + LANES), lambda i: (i, 0)),
                   pl.BlockSpec((N_EXPERTS, tm), lambda i: (0, i))],
        out_shape=[jax.ShapeDtypeStruct((t, d), F32),
                   jax.ShapeDtypeStruct((t, d + LANES), F32),
                   jax.ShapeDtypeStruct((N_EXPERTS, t), F32)],
        compiler_params=_cparams(("arbitrary",)),
    )(o_mla, o_gla, o_gdn, x2d, modg, nrm, w_out, w_router_p)


def _route_kernel(afft_ref, rank_ref, pos_ref, *, cap):
    a = afft_ref[...]
    t = a.shape[1]

    def search(i, cur):
        cand = cur | jnp.left_shift(jnp.int32(1), 30 - i)
        cnt = jnp.sum((a >= lax.bitcast_convert_type(cand, F32)).astype(F32), axis=1, keepdims=True)
        return jnp.where(cnt >= cap, cand, cur)

    kth = lax.bitcast_convert_type(lax.fori_loop(0, 31, search, jnp.zeros((a.shape[0], 1), I32)), F32)
    gt = (a > kth).astype(F32)
    eq = (a == kth).astype(F32)
    need = cap - jnp.sum(gt, axis=1, keepdims=True)
    before = (_iota2((LANES, LANES), 0) < _iota2((LANES, LANES), 1)).astype(BF16)
    nb = t // LANES
    carry_eq = jnp.zeros((a.shape[0], 1), F32)
    carry = jnp.zeros((a.shape[0], 1), F32)
    for b in range(nb):
        sl = slice(b * LANES, (b + 1) * LANES)
        eqb = eq[:, sl]
        pre = jnp.dot(eqb.astype(BF16), before, preferred_element_type=F32) + carry_eq
        selb = jnp.maximum(gt[:, sl], jnp.where(pre < need, eqb, 0.0))
        carry_eq = carry_eq + jnp.sum(eqb, axis=1, keepdims=True)
        local = jnp.dot(selb.astype(BF16), before, preferred_element_type=F32)
        rank_ref[:, sl] = jnp.where(selb > 0.0, local, -1.0)
        pos_ref[:, sl] = local + carry
        carry = carry + jnp.sum(selb, axis=1, keepdims=True)


def _route(afft, cap):
    e, t = afft.shape
    return pl.pallas_call(
        functools.partial(_route_kernel, cap=cap),
        out_shape=[jax.ShapeDtypeStruct((e, t), F32), jax.ShapeDtypeStruct((e, t), F32)],
        compiler_params=_cparams(),
    )(afft)


def _compact_kernel(off_ref, rank_ref, idx_ref, *, nb):
    e = pl.program_id(0)
    idx_ref[...] = jnp.zeros(idx_ref.shape, I32)
    slot = _iota2((LANES, LANES), 0)
    tok = _iota2((LANES, LANES), 0).astype(BF16)
    group = 4

    def blks(g, carry):
        bs = [g * group + j for j in range(group)]
        onehots = [(rank_ref[0, pl.ds(b, 1), :].astype(I32) == slot).astype(BF16) for b in bs]
        ids = [jnp.dot(oh, tok, preferred_element_type=F32).astype(I32) + b * LANES for oh, b in zip(onehots, bs)]
        for b, v in zip(bs, ids):
            idx_ref[0, pl.ds(off_ref[e, b], LANES), :] = v
        return carry

    lax.fori_loop(0, nb // group, blks, 0)


def _compact(off, rank, cap):
    e, t = rank.shape
    nb = t // LANES
    rows = cap + LANES
    idx = pl.pallas_call(
        functools.partial(_compact_kernel, nb=nb),
        grid_spec=pltpu.PrefetchScalarGridSpec(
            num_scalar_prefetch=1,
            grid=(e,),
            in_specs=[pl.BlockSpec((1, nb, LANES), lambda ei, off_r: (ei, 0, 0))],
            out_specs=pl.BlockSpec((1, rows, LANES), lambda ei, off_r: (ei, 0, 0))),
        out_shape=jax.ShapeDtypeStruct((e, rows, LANES), I32),
        compiler_params=_cparams(("arbitrary",)),
    )(off, rank.reshape(e, nb, LANES))
    return idx[:, :cap, 0]


def _gather_copy(x_hbm, xe_scr, sem, slot, src_row, dst_row, nrows):
    return pltpu.make_async_copy(x_hbm.at[pl.ds(src_row, nrows)], xe_scr.at[slot, pl.ds(dst_row, nrows)],
                                 sem.at[slot])


def _expert_kernel(idx_ref, x_hbm, w1_ref, w3_ref, w2_ref, ye_ref, xe_scr, xb_scr, sem, *, cap):
    e = pl.program_id(0)
    ne = pl.num_programs(0)

    slot = e % 2
    nxt = (e + 1) % ne
    nxt_slot = 1 - slot
    d = xb_scr.shape[1]

    @pl.when(e == 0)
    def _():
        def row(s, carry):
            _gather_copy(x_hbm, xe_scr, sem, 0, idx_ref[0, s], s, 1).start()
            return carry
        lax.fori_loop(0, cap, row, 0)

    _gather_copy(x_hbm, xe_scr, sem, slot, 0, 0, cap).wait()
    xb_scr[...] = xe_scr[slot, :, :d].astype(BF16)
    ge = xe_scr[slot, :, d:]
    gate = jnp.sum(jnp.where(_iota2(ge.shape, 1) == e, ge, 0.0), axis=-1, keepdims=True)

    for s in range(cap):
        _gather_copy(x_hbm, xe_scr, sem, nxt_slot, idx_ref[nxt, s], s, 1).start()

    ff = w1_ref.shape[2]
    fchunk = 512
    y = jnp.zeros((cap, w2_ref.shape[2]), F32)
    for j in range(ff // fchunk):
        fs = slice(j * fchunk, (j + 1) * fchunk)
        h1 = _dot(xb_scr[...], w1_ref[0, :, fs])
        h3 = _dot(xb_scr[...], w3_ref[0, :, fs])
        y = y + _dot(_silu(h1) * h3, w2_ref[0, fs, :])
    ye_ref[0] = y * gate

    @pl.when(e == ne - 1)
    def _():
        _gather_copy(x_hbm, xe_scr, sem, nxt_slot, 0, 0, cap).wait()


def _expert_ffn(idx, xm, w1, w3, w2, layer, cap):
    _, e, d, ff = w1.shape
    wspec = lambda shp: pl.BlockSpec((None, 1) + shp, lambda ei, idx_r: (layer, ei, 0, 0))
    return pl.pallas_call(
        functools.partial(_expert_kernel, cap=cap),
        grid_spec=pltpu.PrefetchScalarGridSpec(
            num_scalar_prefetch=1,
            grid=(e,),
            in_specs=[pl.BlockSpec(memory_space=pl.ANY),
                      wspec((d, ff)), wspec((d, ff)), wspec((ff, d))],
            out_specs=pl.BlockSpec((1, cap, d), lambda ei, idx_r: (ei, 0, 0)),
            scratch_shapes=[pltpu.VMEM((2, cap, xm.shape[1]), F32), pltpu.VMEM((cap, d), BF16),
                            pltpu.SemaphoreType.DMA((2,))]),
        out_shape=jax.ShapeDtypeStruct((e, cap, d), F32),
        compiler_params=_cparams(("arbitrary",)),
    )(idx, xm, w1, w3, w2)


def _scatter_kernel(idx_ref, lo_ref, ye_ref, x1_ref, mod_ref, nrm_ref, o_ref, *, part_rows):
    part = pl.program_id(0)
    e = pl.program_id(1)

    @pl.when(e == 0)
    def _():
        o_ref[...] = jnp.zeros(o_ref.shape, F32)

    base = part * part_rows

    lo = lo_ref[e, part]
    hi = lo_ref[e, part + 1]
    group = 8
    head = jnp.minimum(hi, (lo + group - 1) // group * group)
    ngroups = (hi - head) // group

    def add(s, carry):
        t = idx_ref[e, s] - base
        o_ref[pl.ds(t, 1), :] = o_ref[pl.ds(t, 1), :] + ye_ref[0, pl.ds(s, 1), :]
        return carry

    lax.fori_loop(lo, head, add, 0)

    def add_group(g, carry):
        s0 = pl.multiple_of(head + g * group, group)
        ts = [idx_ref[e, s0 + j] - base for j in range(group)]
        ys = ye_ref[0, pl.ds(s0, group), :]
        new = [o_ref[pl.ds(t, 1), :] + ys[j:j + 1] for j, t in enumerate(ts)]
        for t, v in zip(ts, new):
            o_ref[pl.ds(t, 1), :] = v
        return carry

    lax.fori_loop(0, ngroups, add_group, 0)
    lax.fori_loop(head + ngroups * group, hi, add, 0)

    @pl.when(e == pl.num_programs(1) - 1)
    def _():
        m = mod_ref[0]
        rows = 512

        def blk(i, carry):
            rs = pl.ds(pl.multiple_of(i * rows, rows), rows)
            o_ref[rs, :] = x1_ref[rs, :] + m[5:6] * _rms(o_ref[rs, :], nrm_ref[3:4, :])
            return carry

        lax.fori_loop(0, part_rows // rows, blk, 0)


def _scatter_residual(idx, bounds, ye, x1, modg, nrm, parts_per_batch, nparts):
    e, cap, d = ye.shape
    t = x1.shape[0]
    part_rows = t // nparts
    per_batch = modg.shape[0] > 1
    return pl.pallas_call(
        functools.partial(_scatter_kernel, part_rows=part_rows),
        grid_spec=pltpu.PrefetchScalarGridSpec(
            num_scalar_prefetch=2,
            grid=(nparts, e),
            in_specs=[pl.BlockSpec((1, cap, d), lambda pi, ei, a, b: (ei, 0, 0)),
                      pl.BlockSpec((part_rows, d), lambda pi, ei, a, b: (pi, 0)),
                      pl.BlockSpec((1, 6, d), (lambda pi, ei, a, b: (pi // parts_per_batch, 0, 0)) if per_batch
                                   else (lambda pi, ei, a, b: (0, 0, 0))),
                      pl.BlockSpec((4, d), lambda pi, ei, a, b: (0, 0))],
            out_specs=pl.BlockSpec((part_rows, d), lambda pi, ei, a, b: (pi, 0))),
        out_shape=jax.ShapeDtypeStruct((t, d), F32),
        compiler_params=_cparams(("arbitrary", "arbitrary")),
    )(idx, bounds, ye, x1, modg, nrm)


def _rot_cols(w):
    s = w.shape[:-1]
    w2 = w.reshape(s + (ROPE_DIM // 2, 2))
    return jnp.stack([-w2[..., 1], w2[..., 0]], axis=-1).reshape(s + (ROPE_DIM,))


def _layer_weights(l, w_in, mla_q_norm, mla_wq_b, mla_kv_norm, mla_wkv_b, gla_wg, gla_bg, gla_out_norm,
                   gdn_conv, gdn_a_log, gdn_dt_bias, gdn_out_norm, w_out, sandwich_norms, w_router,
                   w_e1, w_e3, w_e2):
    wi = w_in[l]
    o = 0
    parts = {}
    for name, sz in (("q_lat", Q_LORA), ("kv_lat", KV_LORA), ("k_rope", ROPE_DIM),
                     ("gla_q", GLA_W), ("gla_k", GLA_W), ("gla_v", GLA_W), ("gla_glr", 2 * GLA_GATE_RANK),
                     ("gla_z", GLA_W), ("gdn_q", GDN_W), ("gdn_k", GDN_W), ("gdn_v", GDN_W), ("gdn_z", GDN_W),
                     ("gdn_b", 2 * GDN_HEADS), ("gdn_a", 2 * GDN_HEADS)):
        parts[name] = wi[:, o:o + sz]
        o += sz
    w_in_p = jnp.concatenate(
        [parts[k] for k in ("q_lat", "kv_lat", "gla_q", "gla_k", "gla_v", "gla_z",
                            "gdn_q", "gdn_k", "gdn_v", "gdn_z")]
        + [parts["k_rope"], _rot_cols(parts["k_rope"]), parts["gla_glr"], parts["gdn_b"], parts["gdn_a"]],
        axis=1).astype(BF16)

    wq = mla_wq_b[l].reshape(Q_LORA, MLA_HEADS, NOPE_DIM + ROPE_DIM)
    wq_ext = jnp.concatenate([wq, _rot_cols(wq[..., NOPE_DIM:])], axis=-1).reshape(Q_LORA, -1).astype(BF16)
    wkv = mla_wkv_b[l].reshape(KV_LORA, MLA_HEADS, NOPE_DIM + MLA_V_DIM)
    wk = wkv[..., :NOPE_DIM].reshape(KV_LORA, -1).astype(BF16)
    wvt = wkv[..., NOPE_DIM:].reshape(KV_LORA, -1).T.astype(BF16)

    npair = GLA_HEADS // 2
    wg_p = jnp.zeros((npair, LANES, 2 * LANES), F32)
    bg_p = jnp.zeros((npair, 1, 2 * LANES), F32)
    for p in range(npair):
        for d in range(2):
            rows = slice(SM_GLR + d * GLA_GATE_RANK, SM_GLR + (d + 1) * GLA_GATE_RANK)
            wg_p = wg_p.at[p, rows, d * LANES:(d + 1) * LANES].set(gla_wg[l, d][:, p * LANES:(p + 1) * LANES])
            bg_p = bg_p.at[p, 0, d * LANES:(d + 1) * LANES].set(gla_bg[l, d][p * LANES:(p + 1) * LANES])

    alog_row = jnp.zeros((1, LANES), F32).at[0, SM_A:].set(gdn_a_log[l].reshape(-1))
    dtb_row = jnp.zeros((1, LANES), F32).at[0, SM_A:].set(gdn_dt_bias[l].reshape(-1))
    w_router_p = jnp.zeros((D_MODEL, LANES), F32).at[:, :N_EXPERTS].set(w_router[l])
    return dict(
        w_in=w_in_p, gq=mla_q_norm[l][None], wq=wq_ext, gkv=mla_kv_norm[l][None], wk=wk, wvt=wvt, layer=l,
        wg=wg_p.astype(BF16), bg=bg_p, gla_gn=jnp.tile(gla_out_norm[l], 2)[None],
        conv=gdn_conv[l], alog=alog_row, dtb=dtb_row, gdn_gn=jnp.tile(gdn_out_norm[l], 2)[None],
        w_out=w_out[l].astype(BF16), nrm=sandwich_norms[l], w_router=w_router_p,
        w1=w_e1, w3=w_e3, w2=w_e2)


def _rope_tables(n, rotate):
    ones = np.ones((1, NOPE_DIM), np.float32)
    if not rotate:
        t1 = np.concatenate([ones, np.ones((1, ROPE_DIM), np.float32), np.zeros((1, ROPE_DIM), np.float32)], 1)
        z = np.zeros_like(t1)
        return tuple(jnp.asarray(np.broadcast_to(t, (n, LANES)).copy()) for t in (t1, z, z))
    rows = n // GRID_W
    row = jnp.repeat(jnp.arange(rows), GRID_W).astype(F32)
    col = jnp.tile(jnp.arange(GRID_W), rows).astype(F32)
    half = ROPE_DIM // 2
    freqs = ROPE_BASE ** (-jnp.arange(0, half, 2, dtype=F32) / half)
    ang = jnp.concatenate([row[:, None] * freqs, col[:, None] * freqs], axis=-1)
    cos = jnp.repeat(jnp.cos(ang), 2, axis=-1)
    sin = jnp.repeat(jnp.sin(ang), 2, axis=-1)
    z32 = jnp.zeros((n, ROPE_DIM), F32)
    t1 = jnp.concatenate([jnp.ones((n, NOPE_DIM), F32), cos, z32], axis=1)
    t2 = jnp.concatenate([jnp.zeros((n, NOPE_DIM), F32), sin, z32], axis=1)
    t3 = jnp.concatenate([jnp.zeros((n, NOPE_DIM), F32), z32, jnp.ones((n, ROPE_DIM), F32)], axis=1)
    return t1, t2, t3


def _trunk_layer(x, lw, modg, ctx_cache, tabs):
    b, n, d = x.shape
    t = b * n
    tm = min(512, n)
    tiles_per_batch = n // tm
    x2d = x.reshape(t, d)

    proj = _in_projection(x2d, modg, lw["nrm"][0:1], lw["w_in"], tiles_per_batch, tm)
    proj3 = proj.reshape(b, n, PROJ_W)

    q, k, vt, ckv, k_rope = _mla_prep(proj, tabs, lw["gq"], lw["wq"], lw["gkv"], lw["wk"], lw["wvt"],
                                      tiles_per_batch, tm)
    qk_w = MLA_HEADS * MLA_QK_PAD
    if ctx_cache is None:
        kc = vct = None
        s0_gla = jnp.zeros((b, 2, GLA_HEADS, GLA_DV, GLA_DK), F32)
        s0_gdn = jnp.zeros((b, 2, GDN_HEADS, GDN_DV, GDN_DK), F32)
    else:
        ckv_c, kr_c, s0_gla, s0_gdn = ctx_cache
        lc = ckv_c.shape[1]
        kc, vct = _ctx_keys(ckv_c.reshape(b * lc, KV_LORA), kr_c.reshape(b * lc, ROPE_DIM), lw["wk"], lw["wvt"])
        kc = kc.reshape(b, lc, qk_w)
        s0_gla = jnp.swapaxes(s0_gla, -1, -2)
    o_mla = _attention(q.reshape(b, n, qk_w), k.reshape(b, n, qk_w), vt, kc, vct, tq=min(256, n))

    o_gla, st_gla = _gla(proj3, lw["wg"], lw["bg"], lw["gla_gn"], s0_gla)
    o_gdn, st_gdn = _gdn(proj3, lw["conv"], lw["alog"], lw["dtb"], lw["gdn_gn"], s0_gdn)

    x1, xm2, afft = _out_projection(o_mla.reshape(t, MLA_W), o_gla.reshape(t, GLA_W),
                                        o_gdn.reshape(t, GDN_W), x2d, modg, lw["nrm"], lw["w_out"],
                                        lw["w_router"], tiles_per_batch, tm)

    cap = EC_CAPACITY * t // N_EXPERTS
    rank, pos = _route(afft, cap)
    off = pos[:, ::LANES].astype(I32)
    idx = _compact(off, rank, cap)
    ye = _expert_ffn(idx, xm2, lw["w1"], lw["w3"], lw["w2"], lw["layer"], cap)
    nparts = 4 if t >= 4 * 512 else 1
    part_rows = t // nparts
    bounds = jnp.concatenate([off[:, ::part_rows // LANES], jnp.full((N_EXPERTS, 1), cap, I32)], axis=1)
    x2 = _scatter_residual(idx, bounds, ye, x1, modg, lw["nrm"], max(n // part_rows, 1), nparts)

    new = None
    if ctx_cache is None:
        st_gla = jnp.swapaxes(st_gla.reshape(b, GLA_HEADS // 2, 2, 2, GLA_DV, GLA_DK), 1, 2)
        st_gdn = jnp.swapaxes(st_gdn.reshape(b, GDN_HEADS // 2, 2, 2, GDN_DV, GDN_DK), 1, 2)
        new = (ckv.reshape(b, n, KV_LORA), k_rope.reshape(b, n, ROPE_DIM),
               jnp.swapaxes(st_gla.reshape(b, 2, GLA_HEADS, GLA_DV, GLA_DK), -1, -2),
               st_gdn.reshape(b, 2, GDN_HEADS, GDN_DK, GDN_DV))
    return x2.reshape(b, n, d), new


def _pair_major(s0):
    b, two, h = s0.shape[:3]
    return jnp.swapaxes(s0.reshape(b, 2, h // 2, 2, s0.shape[-2], s0.shape[-1]), 1, 2)


def kernel(x_prompt, x_sample, cache_ckv, cache_krope, state_gla, state_gdn, c, c_ctx, w_in, mla_q_norm,
           mla_wq_b, mla_kv_norm, mla_wkv_b, gla_wg, gla_bg, gla_out_norm, gdn_conv, gdn_a_log, gdn_dt_bias,
           gdn_out_norm, w_out, w_mod, b_mod, sandwich_norms, w_router, w_e1, w_e3, w_e2):
    depth = w_in.shape[0]
    nb_s = x_sample.shape[0]
    cond = jnp.zeros((8, D_MODEL), F32).at[0].set(c_ctx).at[1:1 + nb_s].set(c)
    mod = _modulation(cond, w_mod, b_mod)

    tabs_ctx = _rope_tables(8, rotate=False)
    tabs_smp = _rope_tables(x_sample.shape[1], rotate=True)
    tabs_ctx = tuple(jnp.broadcast_to(t[:1], (min(512, x_prompt.shape[1]), LANES)) for t in tabs_ctx)

    lws = [_layer_weights(l, w_in, mla_q_norm, mla_wq_b, mla_kv_norm, mla_wkv_b, gla_wg, gla_bg, gla_out_norm,
                          gdn_conv, gdn_a_log, gdn_dt_bias, gdn_out_norm, w_out, sandwich_norms, w_router,
                          w_e1, w_e3, w_e2) for l in range(depth)]

    xp = x_prompt
    ckv_l, kr_l, sg_l, sd_l = [], [], [], []
    for l in range(depth):
        modg = mod[l, 0:1].reshape(1, 6, D_MODEL)
        xp, (ckv, kr, sg, sd) = _trunk_layer(xp, lws[l], modg, None, tabs_ctx)
        ckv_l.append(ckv)
        kr_l.append(kr)
        sg_l.append(sg)
        sd_l.append(sd)

    xs = x_sample
    for l in range(depth):
        modg = mod[l, 1:1 + nb_s].reshape(nb_s, 6, D_MODEL)
        xs, _ = _trunk_layer(xs, lws[l], modg,
                             (cache_ckv[:, l], cache_krope[:, l], state_gla[:, l], state_gdn[:, l]), tabs_smp)

    return (xp, xs, jnp.stack(ckv_l, axis=1), jnp.stack(kr_l, axis=1),
            jnp.stack(sg_l, axis=1), jnp.stack(sd_l, axis=1))
```

```python
import functools

import numpy as np
import jax
import jax.numpy as jnp
from jax import lax
from jax.experimental import pallas as pl
from jax.experimental.pallas import tpu as pltpu

F32 = jnp.float32
BF16 = jnp.bfloat16
I32 = jnp.int32

D_MODEL = 1024
DEPTH = 2
GRID_W = 64
EPS = 1e-6
CHUNK = 64

MLA_HEADS = 4
Q_LORA = 256
KV_LORA = 128
NOPE_DIM = 64
ROPE_DIM = 32
MLA_V_DIM = 64
ROPE_BASE = 10000.0
MLA_SCALE = (NOPE_DIM + ROPE_DIM) ** -0.5
MLA_QK_PAD = 128

GLA_HEADS = 4
GLA_DK = 64
GLA_DV = 64
GLA_GATE_RANK = 16
GLA_TAU = 16.0

GDN_HEADS = 8
GDN_DK = 64
GDN_DV = 64

N_EXPERTS = 16
EXPERT_FF = 1024
EC_CAPACITY = 2

LANES = 128
HEAD_W = 64
MLA_W = MLA_HEADS * MLA_V_DIM
GLA_W = GLA_HEADS * GLA_DV
GDN_W = GDN_HEADS * GDN_DV

COL_QLAT = 0
COL_KVLAT = 2
COL_GLA_Q = 3
COL_GLA_K = 5
COL_GLA_V = 7
COL_GLA_Z = 9
COL_GDN_Q = 11
COL_GDN_K = 15
COL_GDN_V = 19
COL_GDN_Z = 23
COL_SMALL = 27
PROJ_W = 28 * LANES
SM_KROPE = 0
SM_KROT = 32
SM_GLR = 64
SM_BETA = 96
SM_A = 112

VMEM_LIMIT = 56 * 1024 * 1024


def _cparams(sem=None, **kw):
    return pltpu.CompilerParams(dimension_semantics=sem, vmem_limit_bytes=VMEM_LIMIT, **kw)


def _dot(a, b):
    return jnp.dot(a.astype(BF16), b.astype(BF16), preferred_element_type=F32)


def _dot_nt(a, b):
    return lax.dot_general(a.astype(BF16), b.astype(BF16), (((1,), (1,)), ((), ())),
                           preferred_element_type=F32)


def _dot_tn(a, b):
    return lax.dot_general(a.astype(BF16), b.astype(BF16), (((0,), (0,)), ((), ())),
                           preferred_element_type=F32)


def _dot_x3(a, b):
    a1 = a.astype(BF16)
    a2 = (a - a1.astype(F32)).astype(BF16)
    b1 = b.astype(BF16)
    b2 = (b - b1.astype(F32)).astype(BF16)
    return (jnp.dot(a1, b1, preferred_element_type=F32) + jnp.dot(a1, b2, preferred_element_type=F32)
            + jnp.dot(a2, b1, preferred_element_type=F32))


def _split(x, pieces):
    out = []
    for _ in range(pieces):
        p = x.astype(BF16)
        out.append(p)
        x = x - p.astype(F32)
    return out


def _dot_sel(x, sel, pieces=2):
    s = sel.astype(BF16)
    return sum(jnp.dot(p, s, preferred_element_type=F32) for p in _split(x, pieces))


def _sel_dot(sel, x, pieces=2):
    s = sel.astype(BF16)
    return sum(jnp.dot(s, p, preferred_element_type=F32) for p in _split(x, pieces))


def _sel_dot_nt(sel, x, pieces=2):
    s = sel.astype(BF16)
    dn = (((1,), (1,)), ((), ()))
    return sum(lax.dot_general(s, p, dn, preferred_element_type=F32) for p in _split(x, pieces))


def _silu(x):
    return x * jax.nn.sigmoid(x)


def _softplus(x):
    return jnp.maximum(x, 0.0) + jnp.log(1.0 + jnp.exp(-jnp.abs(x)))


def _log_sigmoid(x):
    return -_softplus(-x)


def _rms(x, g):
    return x * lax.rsqrt(jnp.mean(x * x, axis=-1, keepdims=True) + EPS) * g


def _iota2(shape, dim):
    return lax.broadcasted_iota(I32, shape, dim)


def _head_block_ones(width):
    r = _iota2((width, width), 0) // HEAD_W
    c = _iota2((width, width), 1) // HEAD_W
    return (r == c).astype(F32)


def _mod_kernel(cond_ref, w_ref, b_ref, o_ref):
    s = _silu(cond_ref[...])
    o_ref[0] = _dot(s, w_ref[0]) + b_ref[0]


def _modulation(cond, w_mod, b_mod):
    depth, d, d6 = w_mod.shape
    nj = d6 // d
    return pl.pallas_call(
        _mod_kernel,
        grid=(depth, nj),
        in_specs=[pl.BlockSpec((8, d), lambda l, j: (0, 0)),
                  pl.BlockSpec((1, d, d), lambda l, j: (l, 0, j)),
                  pl.BlockSpec((1, 1, d), lambda l, j: (l, 0, j))],
        out_specs=pl.BlockSpec((1, 8, d), lambda l, j: (l, 0, j)),
        out_shape=jax.ShapeDtypeStruct((depth, 8, d6), F32),
        compiler_params=_cparams(("arbitrary", "arbitrary")),
    )(cond, w_mod, b_mod.reshape(depth, 1, d6))


def _inproj_kernel(x_ref, mod_ref, nrm_ref, w_ref, o_ref, *, col_chunk):
    x = x_ref[...]
    m = mod_ref[0]
    xm = (_rms(x, nrm_ref[...]) * (1.0 + m[1:2]) + m[0:1]).astype(BF16)
    for j in range(o_ref.shape[1] // col_chunk):
        sl = slice(j * col_chunk, (j + 1) * col_chunk)
        o_ref[:, sl] = jnp.dot(xm, w_ref[:, sl], preferred_element_type=F32)


def _in_projection(x2d, modg, nrm0, w_in_p, tiles_per_batch, tm):
    t, d = x2d.shape
    pw = w_in_p.shape[1]
    per_batch = modg.shape[0] > 1
    return pl.pallas_call(
        functools.partial(_inproj_kernel, col_chunk=512),
        grid=(t // tm,),
        in_specs=[pl.BlockSpec((tm, d), lambda i: (i, 0)),
                  pl.BlockSpec((1, 6, d), (lambda i: (i // tiles_per_batch, 0, 0)) if per_batch
                               else (lambda i: (0, 0, 0))),
                  pl.BlockSpec((1, d), lambda i: (0, 0)),
                  pl.BlockSpec((d, pw), lambda i: (0, 0))],
        out_specs=pl.BlockSpec((tm, pw), lambda i: (i, 0)),
        out_shape=jax.ShapeDtypeStruct((t, pw), F32),
        compiler_params=_cparams(("arbitrary",)),
    )(x2d, modg, nrm0, w_in_p)


def _assemble_keys(k_nope, k_mid, k_last):
    parts = []
    for h in range(MLA_HEADS):
        parts += [k_nope[:, h * NOPE_DIM:(h + 1) * NOPE_DIM], k_mid, k_last]
    return jnp.concatenate(parts, axis=-1)


def _mla_prep_kernel(qlat_ref, kvlat_ref, sm_ref, t1_ref, t2_ref, t3_ref, gq_ref, wq_ref, gkv_ref, wk_ref,
                     wvt_ref, q_ref, k_ref, vt_ref, ckv_ref, kr_ref):
    t1 = t1_ref[...]
    t2 = t2_ref[...]
    t3 = t3_ref[...]
    qn = _rms(qlat_ref[...], gq_ref[...])
    raw = _dot(qn, wq_ref[...])
    w = raw.shape[1]
    tile = lambda t: jnp.concatenate([t] * MLA_HEADS, axis=-1)
    q = raw * tile(t1) + pltpu.roll(raw, w - ROPE_DIM, 1) * tile(t2) + pltpu.roll(raw, ROPE_DIM, 1) * tile(t3)
    q_ref[...] = (q * MLA_SCALE).astype(q_ref.dtype)

    ckv = _rms(kvlat_ref[...], gkv_ref[...])
    ckv_ref[...] = ckv
    k_nope = _dot(ckv, wk_ref[...])
    sm = sm_ref[...]
    k_rope = sm[:, SM_KROPE:SM_KROPE + ROPE_DIM]
    k_rot = sm[:, SM_KROT:SM_KROT + ROPE_DIM]
    kr_ref[...] = k_rope
    cs = slice(NOPE_DIM, NOPE_DIM + ROPE_DIM)
    kr = k_rope * t1[:, cs] + k_rot * t2[:, cs]
    k = _assemble_keys(k_nope, kr, jnp.zeros_like(kr))
    k_ref[...] = k.astype(k_ref.dtype)
    vt_ref[...] = _dot_nt(wvt_ref[...], ckv).astype(vt_ref.dtype)


def _mla_prep(proj, tabs, gq, wq_ext, gkv, wk, wvt, tiles_per_seq, tm):
    t = proj.shape[0]
    t1, t2, t3 = tabs
    per_pos = t1.shape[0] > tm
    tab_map = (lambda i: (i % tiles_per_seq, 0)) if per_pos else (lambda i: (0, 0))
    qk_w = MLA_HEADS * MLA_QK_PAD
    return pl.pallas_call(
        _mla_prep_kernel,
        grid=(t // tm,),
        in_specs=[pl.BlockSpec((tm, Q_LORA), lambda i: (i, COL_QLAT * LANES // Q_LORA)),
                  pl.BlockSpec((tm, KV_LORA), lambda i: (i, COL_KVLAT)),
                  pl.BlockSpec((tm, LANES), lambda i: (i, COL_SMALL)),
                  pl.BlockSpec((tm, LANES), tab_map),
                  pl.BlockSpec((tm, LANES), tab_map),
                  pl.BlockSpec((tm, LANES), tab_map),
                  pl.BlockSpec((1, Q_LORA), lambda i: (0, 0)),
                  pl.BlockSpec((Q_LORA, qk_w), lambda i: (0, 0)),
                  pl.BlockSpec((1, KV_LORA), lambda i: (0, 0)),
                  pl.BlockSpec((KV_LORA, MLA_HEADS * NOPE_DIM), lambda i: (0, 0)),
                  pl.BlockSpec((MLA_W, KV_LORA), lambda i: (0, 0))],
        out_specs=[pl.BlockSpec((tm, qk_w), lambda i: (i, 0)),
                   pl.BlockSpec((tm, qk_w), lambda i: (i, 0)),
                   pl.BlockSpec((MLA_W, tm), lambda i: (0, i)),
                   pl.BlockSpec((tm, KV_LORA), lambda i: (i, 0)),
                   pl.BlockSpec((tm, ROPE_DIM), lambda i: (i, 0))],
        out_shape=[jax.ShapeDtypeStruct((t, qk_w), BF16),
                   jax.ShapeDtypeStruct((t, qk_w), BF16),
                   jax.ShapeDtypeStruct((MLA_W, t), BF16),
                   jax.ShapeDtypeStruct((t, KV_LORA), F32),
                   jax.ShapeDtypeStruct((t, ROPE_DIM), F32)],
        compiler_params=_cparams(("arbitrary",)),
    )(proj, proj, proj, t1, t2, t3, gq, wq_ext, gkv, wk, wvt)


def _ctxkeys_kernel(ckv_ref, kr_ref, wk_ref, wvt_ref, k_ref, vt_ref):
    ckv = ckv_ref[...]
    kr = kr_ref[...]
    k = _assemble_keys(_dot(ckv, wk_ref[...]), jnp.zeros_like(kr), kr)
    k_ref[...] = k.astype(k_ref.dtype)
    vt_ref[...] = _dot_nt(wvt_ref[...], ckv).astype(vt_ref.dtype)


def _ctx_keys(ckv_c, kr_c, wk, wvt):
    t = ckv_c.shape[0]
    qk_w = MLA_HEADS * MLA_QK_PAD
    return pl.pallas_call(
        _ctxkeys_kernel,
        out_shape=[jax.ShapeDtypeStruct((t, qk_w), BF16), jax.ShapeDtypeStruct((MLA_W, t), BF16)],
        compiler_params=_cparams(),
    )(ckv_c, kr_c, wk, wvt)


def _attn_kernel(*refs, has_ctx):
    if has_ctx:
        q_ref, k_ref, vt_ref, kc_ref, vct_ref, o_ref = refs
    else:
        q_ref, k_ref, vt_ref, o_ref = refs
    heads = [slice(h * MLA_QK_PAD, (h + 1) * MLA_QK_PAD) for h in range(MLA_HEADS)]
    sts = [_dot_nt(k_ref[0, :, qs], q_ref[0, :, qs]) for qs in heads]
    if has_ctx:
        scts = [_dot_nt(kc_ref[0, :, qs], q_ref[0, :, qs]) for qs in heads]
    outs = []
    for h in range(MLA_HEADS):
        vs = slice(h * MLA_V_DIM, (h + 1) * MLA_V_DIM)
        st = sts[h]
        m = jnp.max(st, axis=0, keepdims=True)
        if has_ctx:
            sct = scts[h]
            m = jnp.maximum(m, jnp.max(sct, axis=0, keepdims=True))
        p = jnp.exp(st - m)
        den = jnp.sum(p, axis=0, keepdims=True)
        acc = _dot(vt_ref[vs, :], p)
        if has_ctx:
            pc = jnp.exp(sct - m)
            den = den + jnp.sum(pc, axis=0, keepdims=True)
            acc = acc + _dot(vct_ref[vs, :], pc)
        outs.append(acc / den)
    o_ref[0] = jnp.concatenate(outs, axis=0).T.astype(o_ref.dtype)


def _attention(q, k, vt, kc, vct, tq):
    b, n, qk_w = q.shape
    has_ctx = kc is not None
    in_specs = [pl.BlockSpec((1, tq, qk_w), lambda bi, i: (bi, i, 0)),
                pl.BlockSpec((1, n, qk_w), lambda bi, i: (bi, 0, 0)),
                pl.BlockSpec((MLA_W, n), lambda bi, i: (0, bi))]
    args = [q, k, vt]
    if has_ctx:
        lc = kc.shape[1]
        in_specs += [pl.BlockSpec((1, lc, qk_w), lambda bi, i: (bi, 0, 0)),
                     pl.BlockSpec((MLA_W, lc), lambda bi, i: (0, bi))]
        args += [kc, vct]
    return pl.pallas_call(
        functools.partial(_attn_kernel, has_ctx=has_ctx),
        grid=(b, n // tq),
        in_specs=in_specs,
        out_specs=pl.BlockSpec((1, tq, MLA_W), lambda bi, i: (bi, i, 0)),
        out_shape=jax.ShapeDtypeStruct((b, n, MLA_W), BF16),
        compiler_params=_cparams(("arbitrary", "arbitrary")),
    )(*args)


def _tri_masks():
    r = _iota2((CHUNK, CHUNK), 0)
    c = _iota2((CHUNK, CHUNK), 1)
    return r, c


def _out_norm_gate(o_f_ref, o_b_ref, z_ref, gn_ref, o_ref, n):
    bd = _head_block_ones(LANES) * (1.0 / HEAD_W)
    rows = min(n, 512)

    def blk(i, carry):
        r0 = pl.multiple_of(i * rows, rows)
        o = o_f_ref[pl.ds(r0, rows), :] + o_b_ref[pl.ds(r0, rows), :]
        ms = _dot_sel(o * o, bd)
        z = z_ref[0, pl.ds(r0, rows), :]
        o_ref[0, pl.ds(r0, rows), :] = (o * lax.rsqrt(ms + EPS) * gn_ref[...] * _silu(z)).astype(o_ref.dtype)
        return carry

    lax.fori_loop(0, n // rows, blk, 0)


def _gla_kernel(q_ref, k_ref, v_ref, z_ref, sm_ref, wg_ref, bg_ref, gn_ref, s0_ref,
                o_ref, st_ref, of_scr, ob_scr, st_scr, *, n):
    nc = n // CHUNK
    unroll = 4
    r, c = _tri_masks()
    tri = ((r >= c).astype(F32), (r <= c).astype(F32))
    lo_lanes = _iota2((CHUNK, LANES), 1) < HEAD_W
    r2 = _iota2((LANES, LANES), 0)
    c2 = _iota2((LANES, LANES), 1)
    same_head = (r2 // CHUNK) == (c2 // CHUNK)
    keep2 = (same_head & (r2 >= c2), same_head & (r2 <= c2))

    def stack2(x):
        return jnp.concatenate([jnp.where(lo_lanes, x, 0.0), jnp.where(lo_lanes, 0.0, x)], axis=0)

    def fold2(x):
        return x[:CHUNK] + x[CHUNK:]

    zeros = jnp.zeros((GLA_DV, GLA_DK), F32)
    for d in range(2):
        st_scr[d] = jnp.concatenate([jnp.concatenate([s0_ref[0, d, 0], zeros], axis=1),
                                     jnp.concatenate([zeros, s0_ref[0, d, 1]], axis=1)], axis=0)

    def body(it, carry):
        ch = []
        for j in range(unroll):
            for d in range(2):
                ci = it * unroll + j
                cc = ci if d == 0 else nc - 1 - ci
                rows = pl.ds(pl.multiple_of(cc * CHUNK, CHUNK), CHUNK)
                ch.append(dict(d=d, rows=rows, sm=sm_ref[0, rows, :], q=q_ref[0, rows, :], k=k_ref[0, rows, :],
                               v=v_ref[0, rows, :]))
        logits = [_dot(k["sm"], wg_ref[0, :, k["d"] * LANES:(k["d"] + 1) * LANES])
                  + bg_ref[0, :, k["d"] * LANES:(k["d"] + 1) * LANES] for k in ch]
        gs = [_sel_dot(tri[k["d"]], _log_sigmoid(lg) * (1.0 / GLA_TAU)) for k, lg in zip(ch, logits)]
        for k, g in zip(ch, gs):
            g_last = g[CHUNK - 1:CHUNK] if k["d"] == 0 else g[0:1]
            k["qd"] = (k["q"] * (GLA_DK ** -0.5) * jnp.exp(g)).astype(BF16)
            k["k_inv"] = k["k"] * jnp.exp(-g)
            k["k_end"] = k["k"] * jnp.exp(g_last - g)
            k["decay"] = jnp.exp(g_last)
        a_s = [jnp.where(keep2[k["d"]], _dot_nt(stack2(k["qd"].astype(F32)), stack2(k["k_inv"])), 0.0) for k in ch]
        intra = [fold2(_dot(a, stack2(k["v"]))) for k, a in zip(ch, a_s)]
        kvs = [jnp.where(same_head, _dot_tn(k["v"], k["k_end"]), 0.0) for k in ch]
        st = [st_scr[0], st_scr[1]]
        inter = []
        for k, kv in zip(ch, kvs):
            inter.append(_dot_nt(k["qd"], st[k["d"]]))
            st[k["d"]] = st[k["d"]] * k["decay"] + kv
        for k, oi, ox in zip(ch, intra, inter):
            if k["d"] == 0:
                of_scr[k["rows"], :] = oi + ox
            else:
                ob_scr[k["rows"], :] = oi + ox
        st_scr[0] = st[0]
        st_scr[1] = st[1]
        return carry

    lax.fori_loop(0, nc // unroll, body, 0)
    for d in range(2):
        st = st_scr[d]
        st_ref[0, d, 0] = st[:GLA_DV, :GLA_DK]
        st_ref[0, d, 1] = st[GLA_DV:, GLA_DK:]
    _out_norm_gate(of_scr, ob_scr, z_ref, gn_ref, o_ref, n)


def _gla(proj3, wg_p, bg_p, gn, s0t):
    b, n, _ = proj3.shape
    npair = GLA_HEADS // 2
    col = lambda base: (lambda bi, p: (bi, 0, base + p))
    s0r = _pair_major(s0t).reshape(b * npair, 2, 2, GLA_DV, GLA_DK)
    st_spec = pl.BlockSpec((1, 2, 2, GLA_DV, GLA_DK), lambda bi, p: (bi * npair + p, 0, 0, 0, 0))
    o, st = pl.pallas_call(
        functools.partial(_gla_kernel, n=n),
        grid=(b, npair),
        in_specs=[pl.BlockSpec((1, n, LANES), col(COL_GLA_Q)),
                  pl.BlockSpec((1, n, LANES), col(COL_GLA_K)),
                  pl.BlockSpec((1, n, LANES), col(COL_GLA_V)),
                  pl.BlockSpec((1, n, LANES), col(COL_GLA_Z)),
                  pl.BlockSpec((1, n, LANES), lambda bi, p: (bi, 0, COL_SMALL)),
                  pl.BlockSpec((1, LANES, 2 * LANES), lambda bi, p: (p, 0, 0)),
                  pl.BlockSpec((1, 1, 2 * LANES), lambda bi, p: (p, 0, 0)),
                  pl.BlockSpec((1, LANES), lambda bi, p: (0, 0)),
                  st_spec],
        out_specs=[pl.BlockSpec((1, n, LANES), lambda bi, p: (bi, 0, p)), st_spec],
        out_shape=[jax.ShapeDtypeStruct((b, n, GLA_W), BF16),
                   jax.ShapeDtypeStruct(s0r.shape, F32)],
        scratch_shapes=[pltpu.VMEM((n, LANES), F32), pltpu.VMEM((n, LANES), F32),
                        pltpu.VMEM((2, 2 * GLA_DV, 2 * GLA_DK), F32)],
        compiler_params=_cparams(("arbitrary", "arbitrary")),
    )(proj3, proj3, proj3, proj3, proj3, wg_p, bg_p, gn, s0r)
    return o, st


def _gdn_kernel(q_ref, k_ref, v_ref, z_ref, sm_ref, cq_ref, ck_ref, cv_ref, alog_ref, dtb_ref,
                ex_ref, gn_ref, s0_ref,
                o_ref, st_ref, a_scr, b_scr, qe_scr, dec_scr, of_scr, ob_scr, st_scr, *, n):
    nc = n // CHUNK
    unroll = 4
    o_scr = (of_scr, ob_scr)
    r, c = _tri_masks()
    tri = ((r >= c).astype(F32), (r <= c).astype(F32))
    row_id = _iota2((CHUNK, LANES), 0)
    lane_id = _iota2((CHUNK, LANES), 1)
    lo_lanes = lane_id < HEAD_W
    head_ones = _head_block_ones(LANES)
    r2 = _iota2((LANES, LANES), 0)
    c2 = _iota2((LANES, LANES), 1)
    same_head = (r2 // CHUNK) == (c2 // CHUNK)
    strict2 = (same_head & (r2 > c2), same_head & (r2 < c2))
    diag2 = r2 == c2
    blk16 = (r2 // 16) == (c2 // 16)

    def conv_silu(x_ref, w_ref, r0, first, last):
        cur = x_ref[0, pl.ds(r0, CHUNK), :]
        prev_row = x_ref[0, pl.ds(jnp.maximum(r0 - 1, 0), 1), :] * jnp.where(first, 0.0, 1.0)
        next_row = x_ref[0, pl.ds(jnp.minimum(r0 + CHUNK, n - 1), 1), :] * jnp.where(last, 0.0, 1.0)
        prev = jnp.where(row_id == 0, prev_row, pltpu.roll(cur, 1, 0))
        nxt = jnp.where(row_id == CHUNK - 1, next_row, pltpu.roll(cur, CHUNK - 1, 0))
        return _silu(prev * w_ref[0:1, :] + cur * w_ref[1:2, :] + nxt * w_ref[2:3, :])

    def stack2(x):
        return jnp.concatenate([jnp.where(lo_lanes, x, 0.0), jnp.where(lo_lanes, 0.0, x)], axis=0)

    def fold2(x):
        return x[:CHUNK] + x[CHUNK:]

    def bcast2(x):
        sw = pltpu.roll(x, HEAD_W, 1)
        return jnp.concatenate([jnp.where(lo_lanes, x, sw), jnp.where(lo_lanes, sw, x)], axis=0)

    def prep(it, carry):
        chunks = []
        for j in range(unroll):
            cc = it * unroll + j
            r0 = pl.multiple_of(cc * CHUNK, CHUNK)
            first = cc == 0
            last = cc == nc - 1
            chunks.append(dict(cc=cc, rows=pl.ds(r0, CHUNK), sm=sm_ref[0, pl.ds(r0, CHUNK), :],
                               qc=conv_silu(q_ref, cq_ref, r0, first, last),
                               kc=conv_silu(k_ref, ck_ref, r0, first, last),
                               vc=conv_silu(v_ref, cv_ref, r0, first, last)))
        qss = [_dot_sel(c["qc"] * c["qc"], head_ones) for c in chunks]
        kss = [_dot_sel(c["kc"] * c["kc"], head_ones) for c in chunks]
        for c, qs, ks in zip(chunks, qss, kss):
            c["qn"] = c["qc"] * lax.rsqrt(qs + EPS) * (GDN_DK ** -0.5)
            c["kn"] = c["kc"] * lax.rsqrt(ks + EPS)
            c["k2"] = stack2(c["kn"]).astype(BF16)
            la = -jnp.exp(alog_ref[...]) * _softplus(c["sm"] + dtb_ref[...])
            c["gates"] = jnp.where(lane_id >= SM_A, la, jax.nn.sigmoid(c["sm"]))
        kks = [_dot_nt(c["k2"], c["k2"]) for c in chunks]
        qk0s = [_dot_nt(stack2(c["qn"]), c["k2"]) for c in chunks]
        bls = [_dot_sel(c["gates"], ex_ref[0]) for c in chunks]
        for c, kk, qk0, bl in zip(chunks, kks, qk0s, bls):
            c["kk"], c["qk0"], c["bl"] = kk, qk0, bl

        ch = [dict(c=c, d=d, beta_x=c["bl"][:, 2 * d * LANES:(2 * d + 1) * LANES])
              for c in chunks for d in range(2)]
        g_xs = [_sel_dot(tri[k["d"]], k["c"]["bl"][:, (2 * k["d"] + 1) * LANES:(2 * k["d"] + 2) * LANES])
                for k in ch]
        ones8 = jnp.ones((8, LANES), F32)
        g_rows = [_sel_dot_nt(ones8, jnp.concatenate([jnp.where(lane_id == 0, g, 0.0),
                                                      jnp.where(lane_id == HEAD_W, g, 0.0)], axis=0))[0:1]
                  for g in g_xs]
        for k, g_x, g_row in zip(ch, g_xs, g_rows):
            d, c = k["d"], k["c"]
            k["g_x"] = g_x
            k["g_last"] = g_x[CHUNK - 1:CHUNK] if d == 0 else g_x[0:1]
            diff = bcast2(g_x) - g_row
            k["e_strict"] = jnp.where(strict2[d], jnp.exp(jnp.where(strict2[d], diff, 0.0)), 0.0)
            m = bcast2(k["beta_x"]) * c["kk"] * k["e_strict"]
            k["md"] = jnp.where(blk16, m, 0.0)
            k["lo"] = m - k["md"]
            k["eg"] = jnp.exp(g_x)
            k["rhs"] = jnp.concatenate([stack2(c["vc"] * k["beta_x"]),
                                        stack2(c["kn"] * k["beta_x"] * k["eg"])], axis=1)
        ps = [-k["md"] for k in ch]
        nds = list(ps)
        for _ in range(3):
            ps = [_dot(p, p) for p in ps]
            ts = [_dot(nd, p) for nd, p in zip(nds, ps)]
            nds = [nd + p + t for nd, p, t in zip(nds, ps, ts)]
        ts = [_dot(nd, k["rhs"]) for nd, k in zip(nds, ch)]
        x0s = [k["rhs"] + t for k, t in zip(ch, ts)]
        ts = [_dot(nd, k["lo"]) for nd, k in zip(nds, ch)]
        zs = [(k["lo"] + t).astype(BF16) for k, t in zip(ch, ts)]
        xs = x0s
        for _ in range(CHUNK // 16 - 1):
            ts = [_dot(z, x) for z, x in zip(zs, xs)]
            xs = [x0 - t for x0, t in zip(x0s, ts)]
        uws = [jnp.concatenate([fold2(x[:, :LANES]), fold2(x[:, LANES:])], axis=1) for x in xs]
        k_ends = [k["c"]["kn"] * jnp.exp(k["g_last"] - k["g_x"]) for k in ch]
        abs_ = [_dot_tn(ke, uw) for uw, ke in zip(uws, k_ends)]
        qus = [_dot(k["c"]["qk0"] * jnp.where(diag2, 1.0, k["e_strict"]), x) for k, x in zip(ch, xs)]
        for k, ab, qu in zip(ch, abs_, qus):
            d, c = k["d"], k["c"]
            blk = pl.ds(pl.multiple_of(c["cc"] * LANES, LANES), LANES)
            b_scr[d, blk, :] = jnp.where(same_head, ab[:, :LANES], 0.0)
            a_scr[d, blk, :] = jnp.where(same_head, ab[:, LANES:], 0.0).astype(BF16)
            qe_scr[d, c["rows"], :] = (c["qn"] * k["eg"] - fold2(qu[:, LANES:])).astype(BF16)
            o_scr[d][c["rows"], :] = fold2(qu[:, :LANES])
            dec_scr[d, pl.ds(pl.multiple_of(c["cc"] * 8, 8), 8), :] = jnp.broadcast_to(
                jnp.exp(k["g_last"]), (8, LANES))
        return carry

    lax.fori_loop(0, nc // unroll, prep, 0)

    zeros = jnp.zeros((GDN_DV, GDN_DK), F32)
    for d in range(2):
        st_scr[d] = jnp.concatenate([jnp.concatenate([s0_ref[0, d, 0], zeros], axis=1),
                                     jnp.concatenate([zeros, s0_ref[0, d, 1]], axis=1)], axis=0)

    def step(ci, carry):
        loaded = []
        for d in range(2):
            cc = ci if d == 0 else nc - 1 - ci
            rows = pl.ds(pl.multiple_of(cc * CHUNK, CHUNK), CHUNK)
            blk = pl.ds(pl.multiple_of(cc * LANES, LANES), LANES)
            loaded.append((rows, st_scr[d], a_scr[d, blk, :], b_scr[d, blk, :], qe_scr[d, rows, :],
                           o_scr[d][rows, :], dec_scr[d, pl.ds(pl.multiple_of(cc * 8, 8), 1), :]))
        ts = [_dot(a, st) for _, st, a, _, _, _, _ in loaded]
        qs = [_dot(qe, st) for _, st, _, _, qe, _, _ in loaded]
        for d, ((rows, st, _, b, _, o_part, decay), t, q) in enumerate(zip(loaded, ts, qs)):
            decay_rows = jnp.concatenate([jnp.broadcast_to(decay[:, 0:1], (GDN_DK, LANES)),
                                          jnp.broadcast_to(decay[:, HEAD_W:HEAD_W + 1], (GDN_DK, LANES))], axis=0)
            st_scr[d] = st * decay_rows - t + b
            o_scr[d][rows, :] = o_part + q
        return carry

    lax.fori_loop(0, nc, step, 0)
    for d in range(2):
        st = st_scr[d]
        st_ref[0, d, 0] = st[:GDN_DV, :GDN_DK]
        st_ref[0, d, 1] = st[GDN_DV:, GDN_DK:]
    _out_norm_gate(of_scr, ob_scr, z_ref, gn_ref, o_ref, n)


def _gdn_select_constants():
    npair = GDN_HEADS // 2
    ex = np.zeros((npair, LANES, 4 * LANES), np.float32)
    for p in range(npair):
        for d in range(2):
            for h in range(2):
                hh = 2 * p + h
                lanes_b = slice(2 * d * LANES + h * HEAD_W, 2 * d * LANES + (h + 1) * HEAD_W)
                lanes_g = slice((2 * d + 1) * LANES + h * HEAD_W, (2 * d + 1) * LANES + (h + 1) * HEAD_W)
                ex[p, SM_BETA + d * GDN_HEADS + hh, lanes_b] = 1.0
                ex[p, SM_A + d * GDN_HEADS + hh, lanes_g] = 1.0
    return jnp.asarray(ex)


def _gdn(proj3, conv_w, alog_row, dtb_row, gn, s0t):
    b, n, _ = proj3.shape
    npair = GDN_HEADS // 2
    ex = _gdn_select_constants()
    nc = n // CHUNK
    col = lambda base: (lambda bi, p: (bi, 0, base + p))
    cw = GDN_HEADS * GDN_DK
    s0r = _pair_major(s0t).reshape(b * npair, 2, 2, GDN_DV, GDN_DK)
    st_spec = pl.BlockSpec((1, 2, 2, GDN_DV, GDN_DK), lambda bi, p: (bi * npair + p, 0, 0, 0, 0))
    pair_spec = lambda shp: pl.BlockSpec((1,) + shp, lambda bi, p: (p, 0, 0))
    o, st = pl.pallas_call(
        functools.partial(_gdn_kernel, n=n),
        grid=(b, npair),
        in_specs=[pl.BlockSpec((1, n, LANES), col(COL_GDN_Q)),
                  pl.BlockSpec((1, n, LANES), col(COL_GDN_K)),
                  pl.BlockSpec((1, n, LANES), col(COL_GDN_V)),
                  pl.BlockSpec((1, n, LANES), col(COL_GDN_Z)),
                  pl.BlockSpec((1, n, LANES), lambda bi, p: (bi, 0, COL_SMALL)),
                  pl.BlockSpec((3, LANES), lambda bi, p: (0, p)),
                  pl.BlockSpec((3, LANES), lambda bi, p: (0, cw // LANES + p)),
                  pl.BlockSpec((3, LANES), lambda bi, p: (0, 2 * cw // LANES + p)),
                  pl.BlockSpec((1, LANES), lambda bi, p: (0, 0)),
                  pl.BlockSpec((1, LANES), lambda bi, p: (0, 0)),
                  pair_spec((LANES, 4 * LANES)),
                  pl.BlockSpec((1, LANES), lambda bi, p: (0, 0)),
                  st_spec],
        out_specs=[pl.BlockSpec((1, n, LANES), lambda bi, p: (bi, 0, p)), st_spec],
        out_shape=[jax.ShapeDtypeStruct((b, n, GDN_W), BF16),
                   jax.ShapeDtypeStruct(s0r.shape, F32)],
        scratch_shapes=[pltpu.VMEM((2, nc * LANES, LANES), BF16),
                        pltpu.VMEM((2, nc * LANES, LANES), F32),
                        pltpu.VMEM((2, n, LANES), BF16),
                        pltpu.VMEM((2, nc * 8, LANES), F32),
                        pltpu.VMEM((n, LANES), F32), pltpu.VMEM((n, LANES), F32),
                        pltpu.VMEM((2, 2 * GDN_DV, 2 * GDN_DK), F32)],
        compiler_params=_cparams(("arbitrary", "arbitrary")),
    )(proj3, proj3, proj3, proj3, proj3, conv_w, conv_w, conv_w, alog_row, dtb_row, ex, gn, s0r)
    return o, st


def _outproj_kernel(omla_ref, ogla_ref, ogdn_ref, x_ref, mod_ref, nrm_ref, wout_ref, wr_ref,
                    x1_ref, xm_ref, afft_ref):
    y = (_dot(omla_ref[...], wout_ref[0:MLA_W, :])
         + _dot(ogla_ref[...], wout_ref[MLA_W:MLA_W + GLA_W, :])
         + _dot(ogdn_ref[...], wout_ref[MLA_W + GLA_W:, :]))
    m = mod_ref[0]
    x1 = x_ref[...] + m[2:3] * _rms(y, nrm_ref[1:2, :])
    x1_ref[...] = x1
    xm2 = _rms(x1, nrm_ref[2:3, :]) * (1.0 + m[4:5]) + m[3:4]
    d = xm2.shape[1]
    xm_ref[:, :d] = xm2
    logits = _dot_x3(xm2, wr_ref[...])
    lane = _iota2(logits.shape, 1)
    logits = jnp.where(lane < N_EXPERTS, logits, -jnp.inf)
    e = jnp.exp(logits - jnp.max(logits, axis=-1, keepdims=True))
    aff = e / jnp.sum(e, axis=-1, keepdims=True)
    xm_ref[:, d:] = aff
    sel = (_iota2((N_EXPERTS, LANES), 0) == _iota2((N_EXPERTS, LANES), 1)).astype(F32)
    afft_ref[...] = _sel_dot_nt(sel, aff, pieces=3)


def _out_projection(o_mla, o_gla, o_gdn, x2d, modg, nrm, w_out, w_router_p, tiles_per_batch, tm):
    t, d = x2d.shape
    per_batch = modg.shape[0] > 1
    return pl.pallas_call(
        _outproj_kernel,
        grid=(t // tm,),
        in_specs=[pl.BlockSpec((tm, MLA_W), lambda i: (i, 0)),
                  pl.BlockSpec((tm, GLA_W), lambda i: (i, 0)),
                  pl.BlockSpec((tm, GDN_W), lambda i: (i, 0)),
                  pl.BlockSpec((tm, d), lambda i: (i, 0)),
                  pl.BlockSpec((1, 6, d), (lambda i: (i // tiles_per_batch, 0, 0)) if per_batch
                               else (lambda i: (0, 0, 0))),
                  pl.BlockSpec((4, d), lambda i: (0, 0)),
                  pl.BlockSpec((d, d), lambda i: (0, 0)),
                  pl.BlockSpec((d, LANES), lambda i: (0, 0))],
        out_specs=[pl.BlockSpec((tm, d), lambda i: (i, 0)),
                   pl.BlockSpec((tm, d + LANES), lambda i: (i, 0)),
                   pl.BlockSpec((N_EXPERTS, tm), lambda i: (0, i))],
        out_shape=[jax.ShapeDtypeStruct((t, d), F32),
                   jax.ShapeDtypeStruct((t, d + LANES), F32),
                   jax.ShapeDtypeStruct((N_EXPERTS, t), F32)],
        compiler_params=_cparams(("arbitrary",)),
    )(o_mla, o_gla, o_gdn, x2d, modg, nrm, w_out, w_router_p)


def _route_kernel(afft_ref, rank_ref, pos_ref, *, cap):
    a = afft_ref[...]
    t = a.shape[1]

    def search(i, cur):
        cand = cur | jnp.left_shift(jnp.int32(1), 30 - i)
        cnt = jnp.sum((a >= lax.bitcast_convert_type(cand, F32)).astype(F32), axis=1, keepdims=True)
        return jnp.where(cnt >= cap, cand, cur)

    kth = lax.bitcast_convert_type(lax.fori_loop(0, 31, search, jnp.zeros((a.shape[0], 1), I32)), F32)
    gt = (a > kth).astype(F32)
    eq = (a == kth).astype(F32)
    need = cap - jnp.sum(gt, axis=1, keepdims=True)
    before = (_iota2((LANES, LANES), 0) < _iota2((LANES, LANES), 1)).astype(BF16)
    nb = t // LANES
    carry_eq = jnp.zeros((a.shape[0], 1), F32)
    carry = jnp.zeros((a.shape[0], 1), F32)
    for b in range(nb):
        sl = slice(b * LANES, (b + 1) * LANES)
        eqb = eq[:, sl]
        pre = jnp.dot(eqb.astype(BF16), before, preferred_element_type=F32) + carry_eq
        selb = jnp.maximum(gt[:, sl], jnp.where(pre < need, eqb, 0.0))
        carry_eq = carry_eq + jnp.sum(eqb, axis=1, keepdims=True)
        local = jnp.dot(selb.astype(BF16), before, preferred_element_type=F32)
        rank_ref[:, sl] = jnp.where(selb > 0.0, local, -1.0)
        pos_ref[:, sl] = local + carry
        carry = carry + jnp.sum(selb, axis=1, keepdims=True)


def _route(afft, cap):
    e, t = afft.shape
    return pl.pallas_call(
        functools.partial(_route_kernel, cap=cap),
        out_shape=[jax.ShapeDtypeStruct((e, t), F32), jax.ShapeDtypeStruct((e, t), F32)],
        compiler_params=_cparams(),
    )(afft)


def _compact_kernel(off_ref, rank_ref, idx_ref, *, nb):
    e = pl.program_id(0)
    idx_ref[...] = jnp.zeros(idx_ref.shape, I32)
    slot = _iota2((LANES, LANES), 0)
    tok = _iota2((LANES, LANES), 0).astype(BF16)
    group = min(8, nb)

    def blks(g, carry):
        bs = [g * group + j for j in range(group)]
        onehots = [(rank_ref[0, pl.ds(b, 1), :].astype(I32) == slot).astype(BF16) for b in bs]
        ids = [jnp.dot(oh, tok, preferred_element_type=F32).astype(I32) + b * LANES for oh, b in zip(onehots, bs)]
        for b, v in zip(bs, ids):
            idx_ref[0, pl.ds(off_ref[e, b], LANES), :] = v
        return carry

    lax.fori_loop(0, nb // group, blks, 0)


def _compact(off, rank, cap):
    e, t = rank.shape
    nb = t // LANES
    rows = cap + LANES
    idx = pl.pallas_call(
        functools.partial(_compact_kernel, nb=nb),
        grid_spec=pltpu.PrefetchScalarGridSpec(
            num_scalar_prefetch=1,
            grid=(e,),
            in_specs=[pl.BlockSpec((1, nb, LANES), lambda ei, off_r: (ei, 0, 0))],
            out_specs=pl.BlockSpec((1, rows, LANES), lambda ei, off_r: (ei, 0, 0))),
        out_shape=jax.ShapeDtypeStruct((e, rows, LANES), I32),
        compiler_params=_cparams(("arbitrary",)),
    )(off, rank.reshape(e, nb, LANES))
    return idx[:, :cap, 0]


def _gather_copy(x_hbm, xe_scr, sem, slot, src_row, dst_row, nrows):
    return pltpu.make_async_copy(x_hbm.at[pl.ds(src_row, nrows)], xe_scr.at[slot, pl.ds(dst_row, nrows)],
                                 sem.at[slot])


def _expert_kernel(idx_ref, x_hbm, w1_ref, w3_ref, w2_ref, ye_ref, xe_scr, xb_scr, sem, *, cap):
    e = pl.program_id(0)
    ne = pl.num_programs(0)

    slot = e % 2
    nxt = (e + 1) % ne
    nxt_slot = 1 - slot
    d = xb_scr.shape[1]

    @pl.when(e == 0)
    def _():
        def row(s, carry):
            _gather_copy(x_hbm, xe_scr, sem, 0, idx_ref[0, s], s, 1).start()
            return carry
        lax.fori_loop(0, cap, row, 0)

    _gather_copy(x_hbm, xe_scr, sem, slot, 0, 0, cap).wait()
    xb_scr[...] = xe_scr[slot, :, :d].astype(BF16)
    ge = xe_scr[slot, :, d:]
    gate = jnp.sum(jnp.where(_iota2(ge.shape, 1) == e, ge, 0.0), axis=-1, keepdims=True)

    for s in range(cap):
        _gather_copy(x_hbm, xe_scr, sem, nxt_slot, idx_ref[nxt, s], s, 1).start()

    ff = w1_ref.shape[2]
    fchunk = 512
    y = jnp.zeros((cap, w2_ref.shape[2]), F32)
    for j in range(ff // fchunk):
        fs = slice(j * fchunk, (j + 1) * fchunk)
        h1 = _dot(xb_scr[...], w1_ref[0, :, fs])
        h3 = _dot(xb_scr[...], w3_ref[0, :, fs])
        y = y + _dot(_silu(h1) * h3, w2_ref[0, fs, :])
    ye_ref[0] = y * gate

    @pl.when(e == ne - 1)
    def _():
        _gather_copy(x_hbm, xe_scr, sem, nxt_slot, 0, 0, cap).wait()


def _expert_ffn(idx, xm, w1, w3, w2, layer, cap):
    _, e, d, ff = w1.shape
    wspec = lambda shp: pl.BlockSpec((None, 1) + shp, lambda ei, idx_r: (layer, ei, 0, 0))
    return pl.pallas_call(
        functools.partial(_expert_kernel, cap=cap),
        grid_spec=pltpu.PrefetchScalarGridSpec(
            num_scalar_prefetch=1,
            grid=(e,),
            in_specs=[pl.BlockSpec(memory_space=pl.ANY),
                      wspec((d, ff)), wspec((d, ff)), wspec((ff, d))],
            out_specs=pl.BlockSpec((1, cap, d), lambda ei, idx_r: (ei, 0, 0)),
            scratch_shapes=[pltpu.VMEM((2, cap, xm.shape[1]), F32), pltpu.VMEM((cap, d), BF16),
                            pltpu.SemaphoreType.DMA((2,))]),
        out_shape=jax.ShapeDtypeStruct((e, cap, d), F32),
        compiler_params=_cparams(("arbitrary",)),
    )(idx, xm, w1, w3, w2)


def _scatter_kernel(idx_ref, lo_ref, ye_ref, x1_ref, mod_ref, nrm_ref, o_ref, *, part_rows):
    part = pl.program_id(0)
    e = pl.program_id(1)

    @pl.when(e == 0)
    def _():
        o_ref[...] = jnp.zeros(o_ref.shape, F32)

    base = part * part_rows

    lo = lo_ref[e, part]
    hi = lo_ref[e, part + 1]
    group = 8
    head = jnp.minimum(hi, (lo + group - 1) // group * group)
    ngroups = (hi - head) // group

    def add(s, carry):
        t = idx_ref[e, s] - base
        o_ref[pl.ds(t, 1), :] = o_ref[pl.ds(t, 1), :] + ye_ref[0, pl.ds(s, 1), :]
        return carry

    lax.fori_loop(lo, head, add, 0)

    def add_group(g, carry):
        s0 = pl.multiple_of(head + g * group, group)
        ts = [idx_ref[e, s0 + j] - base for j in range(group)]
        ys = ye_ref[0, pl.ds(s0, group), :]
        new = [o_ref[pl.ds(t, 1), :] + ys[j:j + 1] for j, t in enumerate(ts)]
        for t, v in zip(ts, new):
            o_ref[pl.ds(t, 1), :] = v
        return carry

    lax.fori_loop(0, ngroups, add_group, 0)
    lax.fori_loop(head + ngroups * group, hi, add, 0)

    @pl.when(e == pl.num_programs(1) - 1)
    def _():
        m = mod_ref[0]
        rows = 512

        def blk(i, carry):
            rs = pl.ds(pl.multiple_of(i * rows, rows), rows)
            o_ref[rs, :] = x1_ref[rs, :] + m[5:6] * _rms(o_ref[rs, :], nrm_ref[3:4, :])
            return carry

        lax.fori_loop(0, part_rows // rows, blk, 0)


def _scatter_residual(idx, bounds, ye, x1, modg, nrm, parts_per_batch, nparts):
    e, cap, d = ye.shape
    t = x1.shape[0]
    part_rows = t // nparts
    per_batch = modg.shape[0] > 1
    return pl.pallas_call(
        functools.partial(_scatter_kernel, part_rows=part_rows),
        grid_spec=pltpu.PrefetchScalarGridSpec(
            num_scalar_prefetch=2,
            grid=(nparts, e),
            in_specs=[pl.BlockSpec((1, cap, d), lambda pi, ei, a, b: (ei, 0, 0)),
                      pl.BlockSpec((part_rows, d), lambda pi, ei, a, b: (pi, 0)),
                      pl.BlockSpec((1, 6, d), (lambda pi, ei, a, b: (pi // parts_per_batch, 0, 0)) if per_batch
                                   else (lambda pi, ei, a, b: (0, 0, 0))),
                      pl.BlockSpec((4, d), lambda pi, ei, a, b: (0, 0))],
            out_specs=pl.BlockSpec((part_rows, d), lambda pi, ei, a, b: (pi, 0))),
        out_shape=jax.ShapeDtypeStruct((t, d), F32),
        compiler_params=_cparams(("arbitrary", "arbitrary")),
    )(idx, bounds, ye, x1, modg, nrm)


def _rot_cols(w):
    s = w.shape[:-1]
    w2 = w.reshape(s + (ROPE_DIM // 2, 2))
    return jnp.stack([-w2[..., 1], w2[..., 0]], axis=-1).reshape(s + (ROPE_DIM,))


def _layer_weights(l, w_in, mla_q_norm, mla_wq_b, mla_kv_norm, mla_wkv_b, gla_wg, gla_bg, gla_out_norm,
                   gdn_conv, gdn_a_log, gdn_dt_bias, gdn_out_norm, w_out, sandwich_norms, w_router,
                   w_e1, w_e3, w_e2):
    wi = w_in[l]
    o = 0
    parts = {}
    for name, sz in (("q_lat", Q_LORA), ("kv_lat", KV_LORA), ("k_rope", ROPE_DIM),
                     ("gla_q", GLA_W), ("gla_k", GLA_W), ("gla_v", GLA_W), ("gla_glr", 2 * GLA_GATE_RANK),
                     ("gla_z", GLA_W), ("gdn_q", GDN_W), ("gdn_k", GDN_W), ("gdn_v", GDN_W), ("gdn_z", GDN_W),
                     ("gdn_b", 2 * GDN_HEADS), ("gdn_a", 2 * GDN_HEADS)):
        parts[name] = wi[:, o:o + sz]
        o += sz
    w_in_p = jnp.concatenate(
        [parts[k] for k in ("q_lat", "kv_lat", "gla_q", "gla_k", "gla_v", "gla_z",
                            "gdn_q", "gdn_k", "gdn_v", "gdn_z")]
        + [parts["k_rope"], _rot_cols(parts["k_rope"]), parts["gla_glr"], parts["gdn_b"], parts["gdn_a"]],
        axis=1).astype(BF16)

    wq = mla_wq_b[l].reshape(Q_LORA, MLA_HEADS, NOPE_DIM + ROPE_DIM)
    wq_ext = jnp.concatenate([wq, _rot_cols(wq[..., NOPE_DIM:])], axis=-1).reshape(Q_LORA, -1).astype(BF16)
    wkv = mla_wkv_b[l].reshape(KV_LORA, MLA_HEADS, NOPE_DIM + MLA_V_DIM)
    wk = wkv[..., :NOPE_DIM].reshape(KV_LORA, -1).astype(BF16)
    wvt = wkv[..., NOPE_DIM:].reshape(KV_LORA, -1).T.astype(BF16)

    npair = GLA_HEADS // 2
    wg_p = jnp.zeros((npair, LANES, 2 * LANES), F32)
    bg_p = jnp.zeros((npair, 1, 2 * LANES), F32)
    for p in range(npair):
        for d in range(2):
            rows = slice(SM_GLR + d * GLA_GATE_RANK, SM_GLR + (d + 1) * GLA_GATE_RANK)
            wg_p = wg_p.at[p, rows, d * LANES:(d + 1) * LANES].set(gla_wg[l, d][:, p * LANES:(p + 1) * LANES])
            bg_p = bg_p.at[p, 0, d * LANES:(d + 1) * LANES].set(gla_bg[l, d][p * LANES:(p + 1) * LANES])

    alog_row = jnp.zeros((1, LANES), F32).at[0, SM_A:].set(gdn_a_log[l].reshape(-1))
    dtb_row = jnp.zeros((1, LANES), F32).at[0, SM_A:].set(gdn_dt_bias[l].reshape(-1))
    w_router_p = jnp.zeros((D_MODEL, LANES), F32).at[:, :N_EXPERTS].set(w_router[l])
    return dict(
        w_in=w_in_p, gq=mla_q_norm[l][None], wq=wq_ext, gkv=mla_kv_norm[l][None], wk=wk, wvt=wvt, layer=l,
        wg=wg_p.astype(BF16), bg=bg_p, gla_gn=jnp.tile(gla_out_norm[l], 2)[None],
        conv=gdn_conv[l], alog=alog_row, dtb=dtb_row, gdn_gn=jnp.tile(gdn_out_norm[l], 2)[None],
        w_out=w_out[l].astype(BF16), nrm=sandwich_norms[l], w_router=w_router_p,
        w1=w_e1, w3=w_e3, w2=w_e2)


def _rope_tables(n, rotate):
    ones = np.ones((1, NOPE_DIM), np.float32)
    if not rotate:
        t1 = np.concatenate([ones, np.ones((1, ROPE_DIM), np.float32), np.zeros((1, ROPE_DIM), np.float32)], 1)
        z = np.zeros_like(t1)
        return tuple(jnp.asarray(np.broadcast_to(t, (n, LANES)).copy()) for t in (t1, z, z))
    rows = n // GRID_W
    row = jnp.repeat(jnp.arange(rows), GRID_W).astype(F32)
    col = jnp.tile(jnp.arange(GRID_W), rows).astype(F32)
    half = ROPE_DIM // 2
    freqs = ROPE_BASE ** (-jnp.arange(0, half, 2, dtype=F32) / half)
    ang = jnp.concatenate([row[:, None] * freqs, col[:, None] * freqs], axis=-1)
    cos = jnp.repeat(jnp.cos(ang), 2, axis=-1)
    sin = jnp.repeat(jnp.sin(ang), 2, axis=-1)
    z32 = jnp.zeros((n, ROPE_DIM), F32)
    t1 = jnp.concatenate([jnp.ones((n, NOPE_DIM), F32), cos, z32], axis=1)
    t2 = jnp.concatenate([jnp.zeros((n, NOPE_DIM), F32), sin, z32], axis=1)
    t3 = jnp.concatenate([jnp.zeros((n, NOPE_DIM), F32), z32, jnp.ones((n, ROPE_DIM), F32)], axis=1)
    return t1, t2, t3


def _trunk_layer(x, lw, modg, ctx_cache, tabs):
    b, n, d = x.shape
    t = b * n
    tm = min(512, n)
    tiles_per_batch = n // tm
    x2d = x.reshape(t, d)

    proj = _in_projection(x2d, modg, lw["nrm"][0:1], lw["w_in"], tiles_per_batch, tm)
    proj3 = proj.reshape(b, n, PROJ_W)

    q, k, vt, ckv, k_rope = _mla_prep(proj, tabs, lw["gq"], lw["wq"], lw["gkv"], lw["wk"], lw["wvt"],
                                      tiles_per_batch, tm)
    qk_w = MLA_HEADS * MLA_QK_PAD
    if ctx_cache is None:
        kc = vct = None
        s0_gla = jnp.zeros((b, 2, GLA_HEADS, GLA_DV, GLA_DK), F32)
        s0_gdn = jnp.zeros((b, 2, GDN_HEADS, GDN_DV, GDN_DK), F32)
    else:
        ckv_c, kr_c, s0_gla, s0_gdn = ctx_cache
        lc = ckv_c.shape[1]
        kc, vct = _ctx_keys(ckv_c.reshape(b * lc, KV_LORA), kr_c.reshape(b * lc, ROPE_DIM), lw["wk"], lw["wvt"])
        kc = kc.reshape(b, lc, qk_w)
        s0_gla = jnp.swapaxes(s0_gla, -1, -2)
    o_mla = _attention(q.reshape(b, n, qk_w), k.reshape(b, n, qk_w), vt, kc, vct, tq=min(256, n))

    o_gla, st_gla = _gla(proj3, lw["wg"], lw["bg"], lw["gla_gn"], s0_gla)
    o_gdn, st_gdn = _gdn(proj3, lw["conv"], lw["alog"], lw["dtb"], lw["gdn_gn"], s0_gdn)

    x1, xm2, afft = _out_projection(o_mla.reshape(t, MLA_W), o_gla.reshape(t, GLA_W),
                                        o_gdn.reshape(t, GDN_W), x2d, modg, lw["nrm"], lw["w_out"],
                                        lw["w_router"], tiles_per_batch, tm)

    cap = EC_CAPACITY * t // N_EXPERTS
    rank, pos = _route(afft, cap)
    off = pos[:, ::LANES].astype(I32)
    idx = _compact(off, rank, cap)
    ye = _expert_ffn(idx, xm2, lw["w1"], lw["w3"], lw["w2"], lw["layer"], cap)
    nparts = 4 if t >= 4 * 512 else 1
    part_rows = t // nparts
    bounds = jnp.concatenate([off[:, ::part_rows // LANES], jnp.full((N_EXPERTS, 1), cap, I32)], axis=1)
    x2 = _scatter_residual(idx, bounds, ye, x1, modg, lw["nrm"], max(n // part_rows, 1), nparts)

    new = None
    if ctx_cache is None:
        st_gla = jnp.swapaxes(st_gla.reshape(b, GLA_HEADS // 2, 2, 2, GLA_DV, GLA_DK), 1, 2)
        st_gdn = jnp.swapaxes(st_gdn.reshape(b, GDN_HEADS // 2, 2, 2, GDN_DV, GDN_DK), 1, 2)
        new = (ckv.reshape(b, n, KV_LORA), k_rope.reshape(b, n, ROPE_DIM),
               jnp.swapaxes(st_gla.reshape(b, 2, GLA_HEADS, GLA_DV, GLA_DK), -1, -2),
               st_gdn.reshape(b, 2, GDN_HEADS, GDN_DK, GDN_DV))
    return x2.reshape(b, n, d), new


def _pair_major(s0):
    b, two, h = s0.shape[:3]
    return jnp.swapaxes(s0.reshape(b, 2, h // 2, 2, s0.shape[-2], s0.shape[-1]), 1, 2)


def kernel(x_prompt, x_sample, cache_ckv, cache_krope, state_gla, state_gdn, c, c_ctx, w_in, mla_q_norm,
           mla_wq_b, mla_kv_norm, mla_wkv_b, gla_wg, gla_bg, gla_out_norm, gdn_conv, gdn_a_log, gdn_dt_bias,
           gdn_out_norm, w_out, w_mod, b_mod, sandwich_norms, w_router, w_e1, w_e3, w_e2):
    depth = w_in.shape[0]
    nb_s = x_sample.shape[0]
    cond = jnp.zeros((8, D_MODEL), F32).at[0].set(c_ctx).at[1:1 + nb_s].set(c)
    mod = _modulation(cond, w_mod, b_mod)

    tabs_ctx = _rope_tables(8, rotate=False)
    tabs_smp = _rope_tables(x_sample.shape[1], rotate=True)
    tabs_ctx = tuple(jnp.broadcast_to(t[:1], (min(512, x_prompt.shape[1]), LANES)) for t in tabs_ctx)

    lws = [_layer_weights(l, w_in, mla_q_norm, mla_wq_b, mla_kv_norm, mla_wkv_b, gla_wg, gla_bg, gla_out_norm,
                          gdn_conv, gdn_a_log, gdn_dt_bias, gdn_out_norm, w_out, sandwich_norms, w_router,
                          w_e1, w_e3, w_e2) for l in range(depth)]

    xp = x_prompt
    ckv_l, kr_l, sg_l, sd_l = [], [], [], []
    for l in range(depth):
        modg = mod[l, 0:1].reshape(1, 6, D_MODEL)
        xp, (ckv, kr, sg, sd) = _trunk_layer(xp, lws[l], modg, None, tabs_ctx)
        ckv_l.append(ckv)
        kr_l.append(kr)
        sg_l.append(sg)
        sd_l.append(sd)

    xs = x_sample
    for l in range(depth):
        modg = mod[l, 1:1 + nb_s].reshape(nb_s, 6, D_MODEL)
        xs, _ = _trunk_layer(xs, lws[l], modg,
                             (cache_ckv[:, l], cache_krope[:, l], state_gla[:, l], state_gdn[:, l]), tabs_smp)

    return (xp, xs, jnp.stack(ckv_l, axis=1), jnp.stack(kr_l, axis=1),
            jnp.stack(sg_l, axis=1), jnp.stack(sd_l, axis=1))
```

```python
import functools

import numpy as np
import jax
import jax.numpy as jnp
from jax import lax
from jax.experimental import pallas as pl
from jax.experimental.pallas import tpu as pltpu

F32 = jnp.float32
BF16 = jnp.bfloat16
I32 = jnp.int32

D_MODEL = 1024
DEPTH = 2
GRID_W = 64
EPS = 1e-6
CHUNK = 64

MLA_HEADS = 4
Q_LORA = 256
KV_LORA = 128
NOPE_DIM = 64
ROPE_DIM = 32
MLA_V_DIM = 64
ROPE_BASE = 10000.0
MLA_SCALE = (NOPE_DIM + ROPE_DIM) ** -0.5
MLA_QK_PAD = 128

GLA_HEADS = 4
GLA_DK = 64
GLA_DV = 64
GLA_GATE_RANK = 16
GLA_TAU = 16.0

GDN_HEADS = 8
GDN_DK = 64
GDN_DV = 64

N_EXPERTS = 16
EXPERT_FF = 1024
EC_CAPACITY = 2

LANES = 128
HEAD_W = 64
MLA_W = MLA_HEADS * MLA_V_DIM
GLA_W = GLA_HEADS * GLA_DV
GDN_W = GDN_HEADS * GDN_DV

COL_QLAT = 0
COL_KVLAT = 2
COL_GLA_Q = 3
COL_GLA_K = 5
COL_GLA_V = 7
COL_GLA_Z = 9
COL_GDN_Q = 11
COL_GDN_K = 15
COL_GDN_V = 19
COL_GDN_Z = 23
COL_SMALL = 27
PROJ_W = 28 * LANES
SM_KROPE = 0
SM_KROT = 32
SM_GLR = 64
SM_BETA = 96
SM_A = 112

VMEM_LIMIT = 56 * 1024 * 1024


def _cparams(sem=None, **kw):
    return pltpu.CompilerParams(dimension_semantics=sem, vmem_limit_bytes=VMEM_LIMIT, **kw)


def _dot(a, b):
    return jnp.dot(a.astype(BF16), b.astype(BF16), preferred_element_type=F32)


def _dot_nt(a, b):
    return lax.dot_general(a.astype(BF16), b.astype(BF16), (((1,), (1,)), ((), ())),
                           preferred_element_type=F32)


def _dot_tn(a, b):
    return lax.dot_general(a.astype(BF16), b.astype(BF16), (((0,), (0,)), ((), ())),
                           preferred_element_type=F32)


def _dot_x3(a, b):
    a1 = a.astype(BF16)
    a2 = (a - a1.astype(F32)).astype(BF16)
    b1 = b.astype(BF16)
    b2 = (b - b1.astype(F32)).astype(BF16)
    return (jnp.dot(a1, b1, preferred_element_type=F32) + jnp.dot(a1, b2, preferred_element_type=F32)
            + jnp.dot(a2, b1, preferred_element_type=F32))


def _split(x, pieces):
    out = []
    for _ in range(pieces):
        p = x.astype(BF16)
        out.append(p)
        x = x - p.astype(F32)
    return out


def _dot_sel(x, sel, pieces=2):
    s = sel.astype(BF16)
    return sum(jnp.dot(p, s, preferred_element_type=F32) for p in _split(x, pieces))


def _sel_dot(sel, x, pieces=2):
    s = sel.astype(BF16)
    return sum(jnp.dot(s, p, preferred_element_type=F32) for p in _split(x, pieces))


def _sel_dot_nt(sel, x, pieces=2):
    s = sel.astype(BF16)
    dn = (((1,), (1,)), ((), ()))
    return sum(lax.dot_general(s, p, dn, preferred_element_type=F32) for p in _split(x, pieces))


def _silu(x):
    return x * jax.nn.sigmoid(x)


def _softplus(x):
    return jnp.maximum(x, 0.0) + jnp.log(1.0 + jnp.exp(-jnp.abs(x)))


def _log_sigmoid(x):
    return -_softplus(-x)


def _rms(x, g):
    return x * lax.rsqrt(jnp.mean(x * x, axis=-1, keepdims=True) + EPS) * g


def _iota2(shape, dim):
    return lax.broadcasted_iota(I32, shape, dim)


def _head_block_ones(width):
    r = _iota2((width, width), 0) // HEAD_W
    c = _iota2((width, width), 1) // HEAD_W
    return (r == c).astype(F32)


def _mod_kernel(cond_ref, w_ref, b_ref, o_ref):
    s = _silu(cond_ref[...])
    o_ref[0] = _dot(s, w_ref[0]) + b_ref[0]


def _modulation(cond, w_mod, b_mod):
    depth, d, d6 = w_mod.shape
    nj = d6 // d
    return pl.pallas_call(
        _mod_kernel,
        grid=(depth, nj),
        in_specs=[pl.BlockSpec((8, d), lambda l, j: (0, 0)),
                  pl.BlockSpec((1, d, d), lambda l, j: (l, 0, j)),
                  pl.BlockSpec((1, 1, d), lambda l, j: (l, 0, j))],
        out_specs=pl.BlockSpec((1, 8, d), lambda l, j: (l, 0, j)),
        out_shape=jax.ShapeDtypeStruct((depth, 8, d6), F32),
        compiler_params=_cparams(("arbitrary", "arbitrary")),
    )(cond, w_mod, b_mod.reshape(depth, 1, d6))


def _inproj_kernel(x_ref, mod_ref, nrm_ref, w_ref, o_ref, *, col_chunk):
    x = x_ref[...]
    m = mod_ref[0]
    xm = (_rms(x, nrm_ref[...]) * (1.0 + m[1:2]) + m[0:1]).astype(BF16)
    for j in range(o_ref.shape[1] // col_chunk):
        sl = slice(j * col_chunk, (j + 1) * col_chunk)
        o_ref[:, sl] = jnp.dot(xm, w_ref[:, sl], preferred_element_type=F32)


def _in_projection(x2d, modg, nrm0, w_in_p, tiles_per_batch, tm):
    t, d = x2d.shape
    pw = w_in_p.shape[1]
    per_batch = modg.shape[0] > 1
    return pl.pallas_call(
        functools.partial(_inproj_kernel, col_chunk=512),
        grid=(t // tm,),
        in_specs=[pl.BlockSpec((tm, d), lambda i: (i, 0)),
                  pl.BlockSpec((1, 6, d), (lambda i: (i // tiles_per_batch, 0, 0)) if per_batch
                               else (lambda i: (0, 0, 0))),
                  pl.BlockSpec((1, d), lambda i: (0, 0)),
                  pl.BlockSpec((d, pw), lambda i: (0, 0))],
        out_specs=pl.BlockSpec((tm, pw), lambda i: (i, 0)),
        out_shape=jax.ShapeDtypeStruct((t, pw), F32),
        compiler_params=_cparams(("arbitrary",)),
    )(x2d, modg, nrm0, w_in_p)


def _assemble_keys(k_nope, k_mid, k_last):
    parts = []
    for h in range(MLA_HEADS):
        parts += [k_nope[:, h * NOPE_DIM:(h + 1) * NOPE_DIM], k_mid, k_last]
    return jnp.concatenate(parts, axis=-1)


def _mla_prep_kernel(qlat_ref, kvlat_ref, sm_ref, t1_ref, t2_ref, t3_ref, gq_ref, wq_ref, gkv_ref, wk_ref,
                     wvt_ref, q_ref, k_ref, vt_ref, ckv_ref, kr_ref):
    t1 = t1_ref[...]
    t2 = t2_ref[...]
    t3 = t3_ref[...]
    qn = _rms(qlat_ref[...], gq_ref[...])
    raw = _dot(qn, wq_ref[...])
    w = raw.shape[1]
    tile = lambda t: jnp.concatenate([t] * MLA_HEADS, axis=-1)
    q = raw * tile(t1) + pltpu.roll(raw, w - ROPE_DIM, 1) * tile(t2) + pltpu.roll(raw, ROPE_DIM, 1) * tile(t3)
    q_ref[...] = (q * MLA_SCALE).astype(q_ref.dtype)

    ckv = _rms(kvlat_ref[...], gkv_ref[...])
    ckv_ref[...] = ckv
    k_nope = _dot(ckv, wk_ref[...])
    sm = sm_ref[...]
    k_rope = sm[:, SM_KROPE:SM_KROPE + ROPE_DIM]
    k_rot = sm[:, SM_KROT:SM_KROT + ROPE_DIM]
    kr_ref[...] = k_rope
    cs = slice(NOPE_DIM, NOPE_DIM + ROPE_DIM)
    kr = k_rope * t1[:, cs] + k_rot * t2[:, cs]
    k = _assemble_keys(k_nope, kr, jnp.zeros_like(kr))
    k_ref[...] = k.astype(k_ref.dtype)
    vt_ref[...] = _dot_nt(wvt_ref[...], ckv).astype(vt_ref.dtype)


def _mla_prep(proj, tabs, gq, wq_ext, gkv, wk, wvt, tiles_per_seq, tm):
    t = proj.shape[0]
    t1, t2, t3 = tabs
    per_pos = t1.shape[0] > tm
    tab_map = (lambda i: (i % tiles_per_seq, 0)) if per_pos else (lambda i: (0, 0))
    qk_w = MLA_HEADS * MLA_QK_PAD
    return pl.pallas_call(
        _mla_prep_kernel,
        grid=(t // tm,),
        in_specs=[pl.BlockSpec((tm, Q_LORA), lambda i: (i, COL_QLAT * LANES // Q_LORA)),
                  pl.BlockSpec((tm, KV_LORA), lambda i: (i, COL_KVLAT)),
                  pl.BlockSpec((tm, LANES), lambda i: (i, COL_SMALL)),
                  pl.BlockSpec((tm, LANES), tab_map),
                  pl.BlockSpec((tm, LANES), tab_map),
                  pl.BlockSpec((tm, LANES), tab_map),
                  pl.BlockSpec((1, Q_LORA), lambda i: (0, 0)),
                  pl.BlockSpec((Q_LORA, qk_w), lambda i: (0, 0)),
                  pl.BlockSpec((1, KV_LORA), lambda i: (0, 0)),
                  pl.BlockSpec((KV_LORA, MLA_HEADS * NOPE_DIM), lambda i: (0, 0)),
                  pl.BlockSpec((MLA_W, KV_LORA), lambda i: (0, 0))],
        out_specs=[pl.BlockSpec((tm, qk_w), lambda i: (i, 0)),
                   pl.BlockSpec((tm, qk_w), lambda i: (i, 0)),
                   pl.BlockSpec((MLA_W, tm), lambda i: (0, i)),
                   pl.BlockSpec((tm, KV_LORA), lambda i: (i, 0)),
                   pl.BlockSpec((tm, ROPE_DIM), lambda i: (i, 0))],
        out_shape=[jax.ShapeDtypeStruct((t, qk_w), BF16),
                   jax.ShapeDtypeStruct((t, qk_w), BF16),
                   jax.ShapeDtypeStruct((MLA_W, t), BF16),
                   jax.ShapeDtypeStruct((t, KV_LORA), F32),
                   jax.ShapeDtypeStruct((t, ROPE_DIM), F32)],
        compiler_params=_cparams(("arbitrary",)),
    )(proj, proj, proj, t1, t2, t3, gq, wq_ext, gkv, wk, wvt)


def _ctxkeys_kernel(ckv_ref, kr_ref, wk_ref, wvt_ref, k_ref, vt_ref):
    ckv = ckv_ref[...]
    kr = kr_ref[...]
    k = _assemble_keys(_dot(ckv, wk_ref[...]), jnp.zeros_like(kr), kr)
    k_ref[...] = k.astype(k_ref.dtype)
    vt_ref[...] = _dot_nt(wvt_ref[...], ckv).astype(vt_ref.dtype)


def _ctx_keys(ckv_c, kr_c, wk, wvt):
    t = ckv_c.shape[0]
    qk_w = MLA_HEADS * MLA_QK_PAD
    return pl.pallas_call(
        _ctxkeys_kernel,
        out_shape=[jax.ShapeDtypeStruct((t, qk_w), BF16), jax.ShapeDtypeStruct((MLA_W, t), BF16)],
        compiler_params=_cparams(),
    )(ckv_c, kr_c, wk, wvt)


def _attn_kernel(*refs, has_ctx):
    if has_ctx:
        q_ref, k_ref, vt_ref, kc_ref, vct_ref, o_ref = refs
    else:
        q_ref, k_ref, vt_ref, o_ref = refs
    heads = [slice(h * MLA_QK_PAD, (h + 1) * MLA_QK_PAD) for h in range(MLA_HEADS)]
    sts = [_dot_nt(k_ref[0, :, qs], q_ref[0, :, qs]) for qs in heads]
    if has_ctx:
        scts = [_dot_nt(kc_ref[0, :, qs], q_ref[0, :, qs]) for qs in heads]
    outs = []
    for h in range(MLA_HEADS):
        vs = slice(h * MLA_V_DIM, (h + 1) * MLA_V_DIM)
        st = sts[h]
        m = jnp.max(st, axis=0, keepdims=True)
        if has_ctx:
            sct = scts[h]
            m = jnp.maximum(m, jnp.max(sct, axis=0, keepdims=True))
        p = jnp.exp(st - m)
        den = jnp.sum(p, axis=0, keepdims=True)
        acc = _dot(vt_ref[vs, :], p)
        if has_ctx:
            pc = jnp.exp(sct - m)
            den = den + jnp.sum(pc, axis=0, keepdims=True)
            acc = acc + _dot(vct_ref[vs, :], pc)
        outs.append(acc / den)
    o_ref[0] = jnp.concatenate(outs, axis=0).T.astype(o_ref.dtype)


def _attention(q, k, vt, kc, vct, tq):
    b, n, qk_w = q.shape
    has_ctx = kc is not None
    in_specs = [pl.BlockSpec((1, tq, qk_w), lambda bi, i: (bi, i, 0)),
                pl.BlockSpec((1, n, qk_w), lambda bi, i: (bi, 0, 0)),
                pl.BlockSpec((MLA_W, n), lambda bi, i: (0, bi))]
    args = [q, k, vt]
    if has_ctx:
        lc = kc.shape[1]
        in_specs += [pl.BlockSpec((1, lc, qk_w), lambda bi, i: (bi, 0, 0)),
                     pl.BlockSpec((MLA_W, lc), lambda bi, i: (0, bi))]
        args += [kc, vct]
    return pl.pallas_call(
        functools.partial(_attn_kernel, has_ctx=has_ctx),
        grid=(b, n // tq),
        in_specs=in_specs,
        out_specs=pl.BlockSpec((1, tq, MLA_W), lambda bi, i: (bi, i, 0)),
        out_shape=jax.ShapeDtypeStruct((b, n, MLA_W), BF16),
        compiler_params=_cparams(("arbitrary", "arbitrary")),
    )(*args)


def _tri_masks():
    r = _iota2((CHUNK, CHUNK), 0)
    c = _iota2((CHUNK, CHUNK), 1)
    return r, c


def _out_norm_gate(o_f_ref, o_b_ref, z_ref, gn_ref, o_ref, n):
    bd = _head_block_ones(LANES) * (1.0 / HEAD_W)
    rows = min(n, 512)

    def blk(i, carry):
        r0 = pl.multiple_of(i * rows, rows)
        o = o_f_ref[pl.ds(r0, rows), :] + o_b_ref[pl.ds(r0, rows), :]
        ms = _dot_sel(o * o, bd)
        z = z_ref[0, pl.ds(r0, rows), :]
        o_ref[0, pl.ds(r0, rows), :] = (o * lax.rsqrt(ms + EPS) * gn_ref[...] * _silu(z)).astype(o_ref.dtype)
        return carry

    lax.fori_loop(0, n // rows, blk, 0)


def _gla_kernel(q_ref, k_ref, v_ref, z_ref, sm_ref, wg_ref, bg_ref, gn_ref, s0_ref,
                o_ref, st_ref, of_scr, ob_scr, st_scr, *, n):
    nc = n // CHUNK
    unroll = 4
    r, c = _tri_masks()
    tri = ((r >= c).astype(F32), (r <= c).astype(F32))
    lo_lanes = _iota2((CHUNK, LANES), 1) < HEAD_W
    r2 = _iota2((LANES, LANES), 0)
    c2 = _iota2((LANES, LANES), 1)
    same_head = (r2 // CHUNK) == (c2 // CHUNK)
    keep2 = (same_head & (r2 >= c2), same_head & (r2 <= c2))

    def stack2(x):
        return jnp.concatenate([jnp.where(lo_lanes, x, 0.0), jnp.where(lo_lanes, 0.0, x)], axis=0)

    def fold2(x):
        return x[:CHUNK] + x[CHUNK:]

    zeros = jnp.zeros((GLA_DV, GLA_DK), F32)
    for d in range(2):
        st_scr[d] = jnp.concatenate([jnp.concatenate([s0_ref[0, d, 0], zeros], axis=1),
                                     jnp.concatenate([zeros, s0_ref[0, d, 1]], axis=1)], axis=0)

    def body(it, carry):
        ch = []
        for j in range(unroll):
            for d in range(2):
                ci = it * unroll + j
                cc = ci if d == 0 else nc - 1 - ci
                rows = pl.ds(pl.multiple_of(cc * CHUNK, CHUNK), CHUNK)
                ch.append(dict(d=d, rows=rows, sm=sm_ref[0, rows, :], q=q_ref[0, rows, :], k=k_ref[0, rows, :],
                               v=v_ref[0, rows, :]))
        logits = [_dot(k["sm"], wg_ref[0, :, k["d"] * LANES:(k["d"] + 1) * LANES])
                  + bg_ref[0, :, k["d"] * LANES:(k["d"] + 1) * LANES] for k in ch]
        gs = [_sel_dot(tri[k["d"]], _log_sigmoid(lg) * (1.0 / GLA_TAU)) for k, lg in zip(ch, logits)]
        for k, g in zip(ch, gs):
            g_last = g[CHUNK - 1:CHUNK] if k["d"] == 0 else g[0:1]
            k["qd"] = (k["q"] * (GLA_DK ** -0.5) * jnp.exp(g)).astype(BF16)
            k["k_inv"] = k["k"] * jnp.exp(-g)
            k["k_end"] = k["k"] * jnp.exp(g_last - g)
            k["decay"] = jnp.exp(g_last)
        a_s = [jnp.where(keep2[k["d"]], _dot_nt(stack2(k["qd"].astype(F32)), stack2(k["k_inv"])), 0.0) for k in ch]
        intra = [fold2(_dot(a, stack2(k["v"]))) for k, a in zip(ch, a_s)]
        kvs = [jnp.where(same_head, _dot_tn(k["v"], k["k_end"]), 0.0) for k in ch]
        st = [st_scr[0], st_scr[1]]
        inter = []
        for k, kv in zip(ch, kvs):
            inter.append(_dot_nt(k["qd"], st[k["d"]]))
            st[k["d"]] = st[k["d"]] * k["decay"] + kv
        for k, oi, ox in zip(ch, intra, inter):
            if k["d"] == 0:
                of_scr[k["rows"], :] = oi + ox
            else:
                ob_scr[k["rows"], :] = oi + ox
        st_scr[0] = st[0]
        st_scr[1] = st[1]
        return carry

    lax.fori_loop(0, nc // unroll, body, 0)
    for d in range(2):
        st = st_scr[d]
        st_ref[0, d, 0] = st[:GLA_DV, :GLA_DK]
        st_ref[0, d, 1] = st[GLA_DV:, GLA_DK:]
    _out_norm_gate(of_scr, ob_scr, z_ref, gn_ref, o_ref, n)


def _gla(proj3, wg_p, bg_p, gn, s0t):
    b, n, _ = proj3.shape
    npair = GLA_HEADS // 2
    col = lambda base: (lambda bi, p: (bi, 0, base + p))
    s0r = _pair_major(s0t).reshape(b * npair, 2, 2, GLA_DV, GLA_DK)
    st_spec = pl.BlockSpec((1, 2, 2, GLA_DV, GLA_DK), lambda bi, p: (bi * npair + p, 0, 0, 0, 0))
    o, st = pl.pallas_call(
        functools.partial(_gla_kernel, n=n),
        grid=(b, npair),
        in_specs=[pl.BlockSpec((1, n, LANES), col(COL_GLA_Q)),
                  pl.BlockSpec((1, n, LANES), col(COL_GLA_K)),
                  pl.BlockSpec((1, n, LANES), col(COL_GLA_V)),
                  pl.BlockSpec((1, n, LANES), col(COL_GLA_Z)),
                  pl.BlockSpec((1, n, LANES), lambda bi, p: (bi, 0, COL_SMALL)),
                  pl.BlockSpec((1, LANES, 2 * LANES), lambda bi, p: (p, 0, 0)),
                  pl.BlockSpec((1, 1, 2 * LANES), lambda bi, p: (p, 0, 0)),
                  pl.BlockSpec((1, LANES), lambda bi, p: (0, 0)),
                  st_spec],
        out_specs=[pl.BlockSpec((1, n, LANES), lambda bi, p: (bi, 0, p)), st_spec],
        out_shape=[jax.ShapeDtypeStruct((b, n, GLA_W), BF16),
                   jax.ShapeDtypeStruct(s0r.shape, F32)],
        scratch_shapes=[pltpu.VMEM((n, LANES), F32), pltpu.VMEM((n, LANES), F32),
                        pltpu.VMEM((2, 2 * GLA_DV, 2 * GLA_DK), F32)],
        compiler_params=_cparams(("arbitrary", "arbitrary")),
    )(proj3, proj3, proj3, proj3, proj3, wg_p, bg_p, gn, s0r)
    return o, st


def _gdn_kernel(q_ref, k_ref, v_ref, z_ref, sm_ref, cq_ref, ck_ref, cv_ref, alog_ref, dtb_ref,
                ex_ref, gn_ref, s0_ref,
                o_ref, st_ref, a_scr, b_scr, qe_scr, dec_scr, of_scr, ob_scr, st_scr, *, n):
    nc = n // CHUNK
    unroll = 4
    o_scr = (of_scr, ob_scr)
    r, c = _tri_masks()
    tri = ((r >= c).astype(F32), (r <= c).astype(F32))
    row_id = _iota2((CHUNK, LANES), 0)
    lane_id = _iota2((CHUNK, LANES), 1)
    lo_lanes = lane_id < HEAD_W
    head_ones = _head_block_ones(LANES)
    r2 = _iota2((LANES, LANES), 0)
    c2 = _iota2((LANES, LANES), 1)
    same_head = (r2 // CHUNK) == (c2 // CHUNK)
    strict2 = (same_head & (r2 > c2), same_head & (r2 < c2))
    diag2 = r2 == c2
    blk16 = (r2 // 16) == (c2 // 16)

    def conv_silu(x_ref, w_ref, r0, first, last):
        cur = x_ref[0, pl.ds(r0, CHUNK), :]
        prev_row = x_ref[0, pl.ds(jnp.maximum(r0 - 1, 0), 1), :] * jnp.where(first, 0.0, 1.0)
        next_row = x_ref[0, pl.ds(jnp.minimum(r0 + CHUNK, n - 1), 1), :] * jnp.where(last, 0.0, 1.0)
        prev = jnp.where(row_id == 0, prev_row, pltpu.roll(cur, 1, 0))
        nxt = jnp.where(row_id == CHUNK - 1, next_row, pltpu.roll(cur, CHUNK - 1, 0))
        return _silu(prev * w_ref[0:1, :] + cur * w_ref[1:2, :] + nxt * w_ref[2:3, :])

    def stack2(x):
        return jnp.concatenate([jnp.where(lo_lanes, x, 0.0), jnp.where(lo_lanes, 0.0, x)], axis=0)

    def fold2(x):
        return x[:CHUNK] + x[CHUNK:]

    def bcast2(x):
        sw = pltpu.roll(x, HEAD_W, 1)
        return jnp.concatenate([jnp.where(lo_lanes, x, sw), jnp.where(lo_lanes, sw, x)], axis=0)

    def prep(it, carry):
        chunks = []
        for j in range(unroll):
            cc = it * unroll + j
            r0 = pl.multiple_of(cc * CHUNK, CHUNK)
            first = cc == 0
            last = cc == nc - 1
            chunks.append(dict(cc=cc, rows=pl.ds(r0, CHUNK), sm=sm_ref[0, pl.ds(r0, CHUNK), :],
                               qc=conv_silu(q_ref, cq_ref, r0, first, last),
                               kc=conv_silu(k_ref, ck_ref, r0, first, last),
                               vc=conv_silu(v_ref, cv_ref, r0, first, last)))
        qss = [_dot_sel(c["qc"] * c["qc"], head_ones) for c in chunks]
        kss = [_dot_sel(c["kc"] * c["kc"], head_ones) for c in chunks]
        for c, qs, ks in zip(chunks, qss, kss):
            c["qn"] = c["qc"] * lax.rsqrt(qs + EPS) * (GDN_DK ** -0.5)
            c["kn"] = c["kc"] * lax.rsqrt(ks + EPS)
            c["k2"] = stack2(c["kn"]).astype(BF16)
            la = -jnp.exp(alog_ref[...]) * _softplus(c["sm"] + dtb_ref[...])
            c["gates"] = jnp.where(lane_id >= SM_A, la, jax.nn.sigmoid(c["sm"]))
        kks = [_dot_nt(c["k2"], c["k2"]) for c in chunks]
        qk0s = [_dot_nt(stack2(c["qn"]), c["k2"]) for c in chunks]
        bls = [_dot_sel(c["gates"], ex_ref[0]) for c in chunks]
        for c, kk, qk0, bl in zip(chunks, kks, qk0s, bls):
            c["kk"], c["qk0"], c["bl"] = kk, qk0, bl

        ch = [dict(c=c, d=d, beta_x=c["bl"][:, 2 * d * LANES:(2 * d + 1) * LANES])
              for c in chunks for d in range(2)]
        g_xs = [_sel_dot(tri[k["d"]], k["c"]["bl"][:, (2 * k["d"] + 1) * LANES:(2 * k["d"] + 2) * LANES])
                for k in ch]
        ones8 = jnp.ones((8, LANES), F32)
        g_rows = [_sel_dot_nt(ones8, jnp.concatenate([jnp.where(lane_id == 0, g, 0.0),
                                                      jnp.where(lane_id == HEAD_W, g, 0.0)], axis=0))[0:1]
                  for g in g_xs]
        for k, g_x, g_row in zip(ch, g_xs, g_rows):
            d, c = k["d"], k["c"]
            k["g_x"] = g_x
            k["g_last"] = g_x[CHUNK - 1:CHUNK] if d == 0 else g_x[0:1]
            diff = bcast2(g_x) - g_row
            k["e_strict"] = jnp.where(strict2[d], jnp.exp(jnp.where(strict2[d], diff, 0.0)), 0.0)
            m = bcast2(k["beta_x"]) * c["kk"] * k["e_strict"]
            k["md"] = jnp.where(blk16, m, 0.0)
            k["lo"] = m - k["md"]
            k["eg"] = jnp.exp(g_x)
            k["rhs"] = jnp.concatenate([stack2(c["vc"] * k["beta_x"]),
                                        stack2(c["kn"] * k["beta_x"] * k["eg"])], axis=1)
        ps = [-k["md"] for k in ch]
        nds = list(ps)
        for _ in range(3):
            ps = [_dot(p, p) for p in ps]
            ts = [_dot(nd, p) for nd, p in zip(nds, ps)]
            nds = [nd + p + t for nd, p, t in zip(nds, ps, ts)]
        ts = [_dot(nd, k["rhs"]) for nd, k in zip(nds, ch)]
        x0s = [k["rhs"] + t for k, t in zip(ch, ts)]
        ts = [_dot(nd, k["lo"]) for nd, k in zip(nds, ch)]
        zs = [(k["lo"] + t).astype(BF16) for k, t in zip(ch, ts)]
        xs = x0s
        for _ in range(CHUNK // 16 - 1):
            ts = [_dot(z, x) for z, x in zip(zs, xs)]
            xs = [x0 - t for x0, t in zip(x0s, ts)]
        uws = [jnp.concatenate([fold2(x[:, :LANES]), fold2(x[:, LANES:])], axis=1) for x in xs]
        k_ends = [k["c"]["kn"] * jnp.exp(k["g_last"] - k["g_x"]) for k in ch]
        abs_ = [_dot_tn(ke, uw) for uw, ke in zip(uws, k_ends)]
        qus = [_dot(k["c"]["qk0"] * jnp.where(diag2, 1.0, k["e_strict"]), x) for k, x in zip(ch, xs)]
        for k, ab, qu in zip(ch, abs_, qus):
            d, c = k["d"], k["c"]
            blk = pl.ds(pl.multiple_of(c["cc"] * LANES, LANES), LANES)
            b_scr[d, blk, :] = jnp.where(same_head, ab[:, :LANES], 0.0)
            a_scr[d, blk, :] = jnp.where(same_head, ab[:, LANES:], 0.0).astype(BF16)
            qe_scr[d, c["rows"], :] = (c["qn"] * k["eg"] - fold2(qu[:, LANES:])).astype(BF16)
            o_scr[d][c["rows"], :] = fold2(qu[:, :LANES])
            dec_scr[d, pl.ds(pl.multiple_of(c["cc"] * 8, 8), 8), :] = jnp.broadcast_to(
                jnp.exp(k["g_last"]), (8, LANES))
        return carry

    lax.fori_loop(0, nc // unroll, prep, 0)

    zeros = jnp.zeros((GDN_DV, GDN_DK), F32)
    for d in range(2):
        st_scr[d] = jnp.concatenate([jnp.concatenate([s0_ref[0, d, 0], zeros], axis=1),
                                     jnp.concatenate([zeros, s0_ref[0, d, 1]], axis=1)], axis=0)

    def step(ci, carry):
        loaded = []
        for d in range(2):
            cc = ci if d == 0 else nc - 1 - ci
            rows = pl.ds(pl.multiple_of(cc * CHUNK, CHUNK), CHUNK)
            blk = pl.ds(pl.multiple_of(cc * LANES, LANES), LANES)
            loaded.append((rows, st_scr[d], a_scr[d, blk, :], b_scr[d, blk, :], qe_scr[d, rows, :],
                           o_scr[d][rows, :], dec_scr[d, pl.ds(pl.multiple_of(cc * 8, 8), 1), :]))
        ts = [_dot(a, st) for _, st, a, _, _, _, _ in loaded]
        qs = [_dot(qe, st) for _, st, _, _, qe, _, _ in loaded]
        for d, ((rows, st, _, b, _, o_part, decay), t, q) in enumerate(zip(loaded, ts, qs)):
            decay_rows = jnp.concatenate([jnp.broadcast_to(decay[:, 0:1], (GDN_DK, LANES)),
                                          jnp.broadcast_to(decay[:, HEAD_W:HEAD_W + 1], (GDN_DK, LANES))], axis=0)
            st_scr[d] = st * decay_rows - t + b
            o_scr[d][rows, :] = o_part + q
        return carry

    lax.fori_loop(0, nc, step, 0)
    for d in range(2):
        st = st_scr[d]
        st_ref[0, d, 0] = st[:GDN_DV, :GDN_DK]
        st_ref[0, d, 1] = st[GDN_DV:, GDN_DK:]
    _out_norm_gate(of_scr, ob_scr, z_ref, gn_ref, o_ref, n)


def _gdn_select_constants():
    npair = GDN_HEADS // 2
    ex = np.zeros((npair, LANES, 4 * LANES), np.float32)
    for p in range(npair):
        for d in range(2):
            for h in range(2):
                hh = 2 * p + h
                lanes_b = slice(2 * d * LANES + h * HEAD_W, 2 * d * LANES + (h + 1) * HEAD_W)
                lanes_g = slice((2 * d + 1) * LANES + h * HEAD_W, (2 * d + 1) * LANES + (h + 1) * HEAD_W)
                ex[p, SM_BETA + d * GDN_HEADS + hh, lanes_b] = 1.0
                ex[p, SM_A + d * GDN_HEADS + hh, lanes_g] = 1.0
    return jnp.asarray(ex)


def _gdn(proj3, conv_w, alog_row, dtb_row, gn, s0t):
    b, n, _ = proj3.shape
    npair = GDN_HEADS // 2
    ex = _gdn_select_constants()
    nc = n // CHUNK
    col = lambda base: (lambda bi, p: (bi, 0, base + p))
    cw = GDN_HEADS * GDN_DK
    s0r = _pair_major(s0t).reshape(b * npair, 2, 2, GDN_DV, GDN_DK)
    st_spec = pl.BlockSpec((1, 2, 2, GDN_DV, GDN_DK), lambda bi, p: (bi * npair + p, 0, 0, 0, 0))
    pair_spec = lambda shp: pl.BlockSpec((1,) + shp, lambda bi, p: (p, 0, 0))
    o, st = pl.pallas_call(
        functools.partial(_gdn_kernel, n=n),
        grid=(b, npair),
        in_specs=[pl.BlockSpec((1, n, LANES), col(COL_GDN_Q)),
                  pl.BlockSpec((1, n, LANES), col(COL_GDN_K)),
                  pl.BlockSpec((1, n, LANES), col(COL_GDN_V)),
                  pl.BlockSpec((1, n, LANES), col(COL_GDN_Z)),
                  pl.BlockSpec((1, n, LANES), lambda bi, p: (bi, 0, COL_SMALL)),
                  pl.BlockSpec((3, LANES), lambda bi, p: (0, p)),
                  pl.BlockSpec((3, LANES), lambda bi, p: (0, cw // LANES + p)),
                  pl.BlockSpec((3, LANES), lambda bi, p: (0, 2 * cw // LANES + p)),
                  pl.BlockSpec((1, LANES), lambda bi, p: (0, 0)),
                  pl.BlockSpec((1, LANES), lambda bi, p: (0, 0)),
                  pair_spec((LANES, 4 * LANES)),
                  pl.BlockSpec((1, LANES), lambda bi, p: (0, 0)),
                  st_spec],
        out_specs=[pl.BlockSpec((1, n, LANES), lambda bi, p: (bi, 0, p)), st_spec],
        out_shape=[jax.ShapeDtypeStruct((b, n, GDN_W), BF16),
                   jax.ShapeDtypeStruct(s0r.shape, F32)],
        scratch_shapes=[pltpu.VMEM((2, nc * LANES, LANES), BF16),
                        pltpu.VMEM((2, nc * LANES, LANES), F32),
                        pltpu.VMEM((2, n, LANES), BF16),
                        pltpu.VMEM((2, nc * 8, LANES), F32),
                        pltpu.VMEM((n, LANES), F32), pltpu.VMEM((n, LANES), F32),
                        pltpu.VMEM((2, 2 * GDN_DV, 2 * GDN_DK), F32)],
        compiler_params=_cparams(("arbitrary", "arbitrary")),
    )(proj3, proj3, proj3, proj3, proj3, conv_w, conv_w, conv_w, alog_row, dtb_row, ex, gn, s0r)
    return o, st


def _outproj_kernel(omla_ref, ogla_ref, ogdn_ref, x_ref, mod_ref, nrm_ref, wout_ref, wr_ref,
                    x1_ref, xm_ref, afft_ref):
    y = (_dot(omla_ref[...], wout_ref[0:MLA_W, :])
         + _dot(ogla_ref[...], wout_ref[MLA_W:MLA_W + GLA_W, :])
         + _dot(ogdn_ref[...], wout_ref[MLA_W + GLA_W:, :]))
    m = mod_ref[0]
    x1 = x_ref[...] + m[2:3] * _rms(y, nrm_ref[1:2, :])
    x1_ref[...] = x1
    xm2 = _rms(x1, nrm_ref[2:3, :]) * (1.0 + m[4:5]) + m[3:4]
    d = xm2.shape[1]
    xm_ref[:, :d] = xm2
    logits = _dot_x3(xm2, wr_ref[...])
    lane = _iota2(logits.shape, 1)
    logits = jnp.where(lane < N_EXPERTS, logits, -jnp.inf)
    e = jnp.exp(logits - jnp.max(logits, axis=-1, keepdims=True))
    aff = e / jnp.sum(e, axis=-1, keepdims=True)
    xm_ref[:, d:] = aff
    sel = (_iota2((N_EXPERTS, LANES), 0) == _iota2((N_EXPERTS, LANES), 1)).astype(F32)
    afft_ref[...] = _sel_dot_nt(sel, aff, pieces=3)


def _out_projection(o_mla, o_gla, o_gdn, x2d, modg, nrm, w_out, w_router_p, tiles_per_batch, tm):
    t, d = x2d.shape
    per_batch = modg.shape[0] > 1
    return pl.pallas_call(
        _outproj_kernel,
        grid=(t // tm,),
        in_specs=[pl.BlockSpec((tm, MLA_W), lambda i: (i, 0)),
                  pl.BlockSpec((tm, GLA_W), lambda i: (i, 0)),
                  pl.BlockSpec((tm, GDN_W), lambda i: (i, 0)),
                  pl.BlockSpec((tm, d), lambda i: (i, 0)),
                  pl.BlockSpec((1, 6, d), (lambda i: (i // tiles_per_batch, 0, 0)) if per_batch
                               else (lambda i: (0, 0, 0))),
                  pl.BlockSpec((4, d), lambda i: (0, 0)),
                  pl.BlockSpec((d, d), lambda i: (0, 0)),
                  pl.BlockSpec((d, LANES), lambda i: (0, 0))],
        out_specs=[pl.BlockSpec((tm, d), lambda i: (i, 0)),
                   pl.BlockSpec((tm, d + LANES), lambda i: (i, 0)),
                   pl.BlockSpec((N_EXPERTS, tm), lambda i: (0, i))],
        out_shape=[jax.ShapeDtypeStruct((t, d), F32),
                   jax.ShapeDtypeStruct((t, d + LANES), F32),
                   jax.ShapeDtypeStruct((N_EXPERTS, t), F32)],
        compiler_params=_cparams(("arbitrary",)),
    )(o_mla, o_gla, o_gdn, x2d, modg, nrm, w_out, w_router_p)


def _route_kernel(afft_ref, rank_ref, pos_ref, *, cap):
    a = afft_ref[...]
    t = a.shape[1]

    def search(i, cur):
        cand = cur | jnp.left_shift(jnp.int32(1), 30 - i)
        cnt = jnp.sum((a >= lax.bitcast_convert_type(cand, F32)).astype(F32), axis=1, keepdims=True)
        return jnp.where(cnt >= cap, cand, cur)

    kth = lax.bitcast_convert_type(lax.fori_loop(0, 31, search, jnp.zeros((a.shape[0], 1), I32)), F32)
    gt = (a > kth).astype(F32)
    eq = (a == kth).astype(F32)
    need = cap - jnp.sum(gt, axis=1, keepdims=True)
    before = (_iota2((LANES, LANES), 0) < _iota2((LANES, LANES), 1)).astype(BF16)
    nb = t // LANES
    carry_eq = jnp.zeros((a.shape[0], 1), F32)
    carry = jnp.zeros((a.shape[0], 1), F32)
    for b in range(nb):
        sl = slice(b * LANES, (b + 1) * LANES)
        eqb = eq[:, sl]
        pre = jnp.dot(eqb.astype(BF16), before, preferred_element_type=F32) + carry_eq
        selb = jnp.maximum(gt[:, sl], jnp.where(pre < need, eqb, 0.0))
        carry_eq = carry_eq + jnp.sum(eqb, axis=1, keepdims=True)
        local = jnp.dot(selb.astype(BF16), before, preferred_element_type=F32)
        rank_ref[:, sl] = jnp.where(selb > 0.0, local, -1.0)
        pos_ref[:, sl] = local + carry
        carry = carry + jnp.sum(selb, axis=1, keepdims=True)


def _route(afft, cap):
    e, t = afft.shape
    return pl.pallas_call(
        functools.partial(_route_kernel, cap=cap),
        out_shape=[jax.ShapeDtypeStruct((e, t), F32), jax.ShapeDtypeStruct((e, t), F32)],
        compiler_params=_cparams(),
    )(afft)


def _compact_kernel(off_ref, rank_ref, idx_ref, *, nb):
    e = pl.program_id(0)
    idx_ref[...] = jnp.zeros(idx_ref.shape, I32)
    slot = _iota2((LANES, LANES), 0)
    tok = _iota2((LANES, LANES), 0).astype(BF16)
    group = min(8, nb)

    def blks(g, carry):
        bs = [g * group + j for j in range(group)]
        onehots = [(rank_ref[0, pl.ds(b, 1), :].astype(I32) == slot).astype(BF16) for b in bs]
        ids = [jnp.dot(oh, tok, preferred_element_type=F32).astype(I32) + b * LANES for oh, b in zip(onehots, bs)]
        for b, v in zip(bs, ids):
            idx_ref[0, pl.ds(off_ref[e, b], LANES), :] = v
        return carry

    lax.fori_loop(0, nb // group, blks, 0)


def _compact(off, rank, cap):
    e, t = rank.shape
    nb = t // LANES
    rows = cap + LANES
    idx = pl.pallas_call(
        functools.partial(_compact_kernel, nb=nb),
        grid_spec=pltpu.PrefetchScalarGridSpec(
            num_scalar_prefetch=1,
            grid=(e,),
            in_specs=[pl.BlockSpec((1, nb, LANES), lambda ei, off_r: (ei, 0, 0))],
            out_specs=pl.BlockSpec((1, rows, LANES), lambda ei, off_r: (ei, 0, 0))),
        out_shape=jax.ShapeDtypeStruct((e, rows, LANES), I32),
        compiler_params=_cparams(("arbitrary",)),
    )(off, rank.reshape(e, nb, LANES))
    return idx[:, :cap, 0]


def _gather_copy(x_hbm, xe_scr, sem, slot, src_row, dst_row, nrows):
    return pltpu.make_async_copy(x_hbm.at[pl.ds(src_row, nrows)], xe_scr.at[slot, pl.ds(dst_row, nrows)],
                                 sem.at[slot])


def _expert_kernel(idx_ref, x_hbm, w1_ref, w3_ref, w2_ref, ye_ref, xe_scr, xb_scr, sem, *, cap):
    e = pl.program_id(0)
    ne = pl.num_programs(0)

    slot = e % 2
    nxt = (e + 1) % ne
    nxt_slot = 1 - slot
    d = xb_scr.shape[1]

    @pl.when(e == 0)
    def _():
        def row(s, carry):
            _gather_copy(x_hbm, xe_scr, sem, 0, idx_ref[0, s], s, 1).start()
            return carry
        lax.fori_loop(0, cap, row, 0)

    _gather_copy(x_hbm, xe_scr, sem, slot, 0, 0, cap).wait()
    xb_scr[...] = xe_scr[slot, :, :d].astype(BF16)
    ge = xe_scr[slot, :, d:]
    gate = jnp.sum(jnp.where(_iota2(ge.shape, 1) == e, ge, 0.0), axis=-1, keepdims=True)

    for s in range(cap):
        _gather_copy(x_hbm, xe_scr, sem, nxt_slot, idx_ref[nxt, s], s, 1).start(priority=s % 2)

    ff = w1_ref.shape[2]
    fchunk = 512
    y = jnp.zeros((cap, w2_ref.shape[2]), F32)
    for j in range(ff // fchunk):
        fs = slice(j * fchunk, (j + 1) * fchunk)
        h1 = _dot(xb_scr[...], w1_ref[0, :, fs])
        h3 = _dot(xb_scr[...], w3_ref[0, :, fs])
        y = y + _dot(_silu(h1) * h3, w2_ref[0, fs, :])
    ye_ref[0] = y * gate

    @pl.when(e == ne - 1)
    def _():
        _gather_copy(x_hbm, xe_scr, sem, nxt_slot, 0, 0, cap).wait()


def _expert_ffn(idx, xm, w1, w3, w2, layer, cap):
    _, e, d, ff = w1.shape
    wspec = lambda shp: pl.BlockSpec((None, 1) + shp, lambda ei, idx_r: (layer, ei, 0, 0))
    return pl.pallas_call(
        functools.partial(_expert_kernel, cap=cap),
        grid_spec=pltpu.PrefetchScalarGridSpec(
            num_scalar_prefetch=1,
            grid=(e,),
            in_specs=[pl.BlockSpec(memory_space=pl.ANY),
                      wspec((d, ff)), wspec((d, ff)), wspec((ff, d))],
            out_specs=pl.BlockSpec((1, cap, d), lambda ei, idx_r: (ei, 0, 0)),
            scratch_shapes=[pltpu.VMEM((2, cap, xm.shape[1]), F32), pltpu.VMEM((cap, d), BF16),
                            pltpu.SemaphoreType.DMA((2,))]),
        out_shape=jax.ShapeDtypeStruct((e, cap, d), F32),
        compiler_params=_cparams(("arbitrary",)),
    )(idx, xm, w1, w3, w2)


def _scatter_kernel(idx_ref, lo_ref, ye_ref, x1_ref, mod_ref, nrm_ref, o_ref, *, part_rows):
    part = pl.program_id(0)
    e = pl.program_id(1)

    @pl.when(e == 0)
    def _():
        o_ref[...] = jnp.zeros(o_ref.shape, F32)

    base = part * part_rows

    lo = lo_ref[e, part]
    hi = lo_ref[e, part + 1]
    group = 8
    head = jnp.minimum(hi, (lo + group - 1) // group * group)
    ngroups = (hi - head) // group

    def add(s, carry):
        t = idx_ref[e, s] - base
        o_ref[pl.ds(t, 1), :] = o_ref[pl.ds(t, 1), :] + ye_ref[0, pl.ds(s, 1), :]
        return carry

    lax.fori_loop(lo, head, add, 0)

    def add_group(g, carry):
        s0 = pl.multiple_of(head + g * group, group)
        ts = [idx_ref[e, s0 + j] - base for j in range(group)]
        ys = ye_ref[0, pl.ds(s0, group), :]
        new = [o_ref[pl.ds(t, 1), :] + ys[j:j + 1] for j, t in enumerate(ts)]
        for t, v in zip(ts, new):
            o_ref[pl.ds(t, 1), :] = v
        return carry

    lax.fori_loop(0, ngroups, add_group, 0)
    lax.fori_loop(head + ngroups * group, hi, add, 0)

    @pl.when(e == pl.num_programs(1) - 1)
    def _():
        m = mod_ref[0]
        rows = 512

        def blk(i, carry):
            rs = pl.ds(pl.multiple_of(i * rows, rows), rows)
            o_ref[rs, :] = x1_ref[rs, :] + m[5:6] * _rms(o_ref[rs, :], nrm_ref[3:4, :])
            return carry

        lax.fori_loop(0, part_rows // rows, blk, 0)


def _scatter_residual(idx, bounds, ye, x1, modg, nrm, parts_per_batch, nparts):
    e, cap, d = ye.shape
    t = x1.shape[0]
    part_rows = t // nparts
    per_batch = modg.shape[0] > 1
    return pl.pallas_call(
        functools.partial(_scatter_kernel, part_rows=part_rows),
        grid_spec=pltpu.PrefetchScalarGridSpec(
            num_scalar_prefetch=2,
            grid=(nparts, e),
            in_specs=[pl.BlockSpec((1, cap, d), lambda pi, ei, a, b: (ei, 0, 0)),
                      pl.BlockSpec((part_rows, d), lambda pi, ei, a, b: (pi, 0)),
                      pl.BlockSpec((1, 6, d), (lambda pi, ei, a, b: (pi // parts_per_batch, 0, 0)) if per_batch
                                   else (lambda pi, ei, a, b: (0, 0, 0))),
                      pl.BlockSpec((4, d), lambda pi, ei, a, b: (0, 0))],
            out_specs=pl.BlockSpec((part_rows, d), lambda pi, ei, a, b: (pi, 0))),
        out_shape=jax.ShapeDtypeStruct((t, d), F32),
        compiler_params=_cparams(("arbitrary", "arbitrary")),
    )(idx, bounds, ye, x1, modg, nrm)


def _rot_cols(w):
    s = w.shape[:-1]
    w2 = w.reshape(s + (ROPE_DIM // 2, 2))
    return jnp.stack([-w2[..., 1], w2[..., 0]], axis=-1).reshape(s + (ROPE_DIM,))


def _layer_weights(l, w_in, mla_q_norm, mla_wq_b, mla_kv_norm, mla_wkv_b, gla_wg, gla_bg, gla_out_norm,
                   gdn_conv, gdn_a_log, gdn_dt_bias, gdn_out_norm, w_out, sandwich_norms, w_router,
                   w_e1, w_e3, w_e2):
    wi = w_in[l]
    o = 0
    parts = {}
    for name, sz in (("q_lat", Q_LORA), ("kv_lat", KV_LORA), ("k_rope", ROPE_DIM),
                     ("gla_q", GLA_W), ("gla_k", GLA_W), ("gla_v", GLA_W), ("gla_glr", 2 * GLA_GATE_RANK),
                     ("gla_z", GLA_W), ("gdn_q", GDN_W), ("gdn_k", GDN_W), ("gdn_v", GDN_W), ("gdn_z", GDN_W),
                     ("gdn_b", 2 * GDN_HEADS), ("gdn_a", 2 * GDN_HEADS)):
        parts[name] = wi[:, o:o + sz]
        o += sz
    w_in_p = jnp.concatenate(
        [parts[k] for k in ("q_lat", "kv_lat", "gla_q", "gla_k", "gla_v", "gla_z",
                            "gdn_q", "gdn_k", "gdn_v", "gdn_z")]
        + [parts["k_rope"], _rot_cols(parts["k_rope"]), parts["gla_glr"], parts["gdn_b"], parts["gdn_a"]],
        axis=1).astype(BF16)

    wq = mla_wq_b[l].reshape(Q_LORA, MLA_HEADS, NOPE_DIM + ROPE_DIM)
    wq_ext = jnp.concatenate([wq, _rot_cols(wq[..., NOPE_DIM:])], axis=-1).reshape(Q_LORA, -1).astype(BF16)
    wkv = mla_wkv_b[l].reshape(KV_LORA, MLA_HEADS, NOPE_DIM + MLA_V_DIM)
    wk = wkv[..., :NOPE_DIM].reshape(KV_LORA, -1).astype(BF16)
    wvt = wkv[..., NOPE_DIM:].reshape(KV_LORA, -1).T.astype(BF16)

    npair = GLA_HEADS // 2
    wg_p = jnp.zeros((npair, LANES, 2 * LANES), F32)
    bg_p = jnp.zeros((npair, 1, 2 * LANES), F32)
    for p in range(npair):
        for d in range(2):
            rows = slice(SM_GLR + d * GLA_GATE_RANK, SM_GLR + (d + 1) * GLA_GATE_RANK)
            wg_p = wg_p.at[p, rows, d * LANES:(d + 1) * LANES].set(gla_wg[l, d][:, p * LANES:(p + 1) * LANES])
            bg_p = bg_p.at[p, 0, d * LANES:(d + 1) * LANES].set(gla_bg[l, d][p * LANES:(p + 1) * LANES])

    alog_row = jnp.zeros((1, LANES), F32).at[0, SM_A:].set(gdn_a_log[l].reshape(-1))
    dtb_row = jnp.zeros((1, LANES), F32).at[0, SM_A:].set(gdn_dt_bias[l].reshape(-1))
    w_router_p = jnp.zeros((D_MODEL, LANES), F32).at[:, :N_EXPERTS].set(w_router[l])
    return dict(
        w_in=w_in_p, gq=mla_q_norm[l][None], wq=wq_ext, gkv=mla_kv_norm[l][None], wk=wk, wvt=wvt, layer=l,
        wg=wg_p.astype(BF16), bg=bg_p, gla_gn=jnp.tile(gla_out_norm[l], 2)[None],
        conv=gdn_conv[l], alog=alog_row, dtb=dtb_row, gdn_gn=jnp.tile(gdn_out_norm[l], 2)[None],
        w_out=w_out[l].astype(BF16), nrm=sandwich_norms[l], w_router=w_router_p,
        w1=w_e1, w3=w_e3, w2=w_e2)


def _rope_tables(n, rotate):
    ones = np.ones((1, NOPE_DIM), np.float32)
    if not rotate:
        t1 = np.concatenate([ones, np.ones((1, ROPE_DIM), np.float32), np.zeros((1, ROPE_DIM), np.float32)], 1)
        z = np.zeros_like(t1)
        return tuple(jnp.asarray(np.broadcast_to(t, (n, LANES)).copy()) for t in (t1, z, z))
    rows = n // GRID_W
    row = jnp.repeat(jnp.arange(rows), GRID_W).astype(F32)
    col = jnp.tile(jnp.arange(GRID_W), rows).astype(F32)
    half = ROPE_DIM // 2
    freqs = ROPE_BASE ** (-jnp.arange(0, half, 2, dtype=F32) / half)
    ang = jnp.concatenate([row[:, None] * freqs, col[:, None] * freqs], axis=-1)
    cos = jnp.repeat(jnp.cos(ang), 2, axis=-1)
    sin = jnp.repeat(jnp.sin(ang), 2, axis=-1)
    z32 = jnp.zeros((n, ROPE_DIM), F32)
    t1 = jnp.concatenate([jnp.ones((n, NOPE_DIM), F32), cos, z32], axis=1)
    t2 = jnp.concatenate([jnp.zeros((n, NOPE_DIM), F32), sin, z32], axis=1)
    t3 = jnp.concatenate([jnp.zeros((n, NOPE_DIM), F32), z32, jnp.ones((n, ROPE_DIM), F32)], axis=1)
    return t1, t2, t3


def _trunk_layer(x, lw, modg, ctx_cache, tabs):
    b, n, d = x.shape
    t = b * n
    tm = min(512, n)
    tiles_per_batch = n // tm
    x2d = x.reshape(t, d)

    proj = _in_projection(x2d, modg, lw["nrm"][0:1], lw["w_in"], tiles_per_batch, tm)
    proj3 = proj.reshape(b, n, PROJ_W)

    q, k, vt, ckv, k_rope = _mla_prep(proj, tabs, lw["gq"], lw["wq"], lw["gkv"], lw["wk"], lw["wvt"],
                                      tiles_per_batch, tm)
    qk_w = MLA_HEADS * MLA_QK_PAD
    if ctx_cache is None:
        kc = vct = None
        s0_gla = jnp.zeros((b, 2, GLA_HEADS, GLA_DV, GLA_DK), F32)
        s0_gdn = jnp.zeros((b, 2, GDN_HEADS, GDN_DV, GDN_DK), F32)
    else:
        ckv_c, kr_c, s0_gla, s0_gdn = ctx_cache
        lc = ckv_c.shape[1]
        kc, vct = _ctx_keys(ckv_c.reshape(b * lc, KV_LORA), kr_c.reshape(b * lc, ROPE_DIM), lw["wk"], lw["wvt"])
        kc = kc.reshape(b, lc, qk_w)
        s0_gla = jnp.swapaxes(s0_gla, -1, -2)
    o_mla = _attention(q.reshape(b, n, qk_w), k.reshape(b, n, qk_w), vt, kc, vct, tq=min(256, n))

    o_gla, st_gla = _gla(proj3, lw["wg"], lw["bg"], lw["gla_gn"], s0_gla)
    o_gdn, st_gdn = _gdn(proj3, lw["conv"], lw["alog"], lw["dtb"], lw["gdn_gn"], s0_gdn)

    x1, xm2, afft = _out_projection(o_mla.reshape(t, MLA_W), o_gla.reshape(t, GLA_W),
                                        o_gdn.reshape(t, GDN_W), x2d, modg, lw["nrm"], lw["w_out"],
                                        lw["w_router"], tiles_per_batch, tm)

    cap = EC_CAPACITY * t // N_EXPERTS
    rank, pos = _route(afft, cap)
    off = pos[:, ::LANES].astype(I32)
    idx = _compact(off, rank, cap)
    ye = _expert_ffn(idx, xm2, lw["w1"], lw["w3"], lw["w2"], lw["layer"], cap)
    nparts = 4 if t >= 4 * 512 else 1
    part_rows = t // nparts
    bounds = jnp.concatenate([off[:, ::part_rows // LANES], jnp.full((N_EXPERTS, 1), cap, I32)], axis=1)
    x2 = _scatter_residual(idx, bounds, ye, x1, modg, lw["nrm"], max(n // part_rows, 1), nparts)

    new = None
    if ctx_cache is None:
        st_gla = jnp.swapaxes(st_gla.reshape(b, GLA_HEADS // 2, 2, 2, GLA_DV, GLA_DK), 1, 2)
        st_gdn = jnp.swapaxes(st_gdn.reshape(b, GDN_HEADS // 2, 2, 2, GDN_DV, GDN_DK), 1, 2)
        new = (ckv.reshape(b, n, KV_LORA), k_rope.reshape(b, n, ROPE_DIM),
               jnp.swapaxes(st_gla.reshape(b, 2, GLA_HEADS, GLA_DV, GLA_DK), -1, -2),
               st_gdn.reshape(b, 2, GDN_HEADS, GDN_DK, GDN_DV))
    return x2.reshape(b, n, d), new


def _pair_major(s0):
    b, two, h = s0.shape[:3]
    return jnp.swapaxes(s0.reshape(b, 2, h // 2, 2, s0.shape[-2], s0.shape[-1]), 1, 2)


def kernel(x_prompt, x_sample, cache_ckv, cache_krope, state_gla, state_gdn, c, c_ctx, w_in, mla_q_norm,
           mla_wq_b, mla_kv_norm, mla_wkv_b, gla_wg, gla_bg, gla_out_norm, gdn_conv, gdn_a_log, gdn_dt_bias,
           gdn_out_norm, w_out, w_mod, b_mod, sandwich_norms, w_router, w_e1, w_e3, w_e2):
    depth = w_in.shape[0]
    nb_s = x_sample.shape[0]
    cond = jnp.zeros((8, D_MODEL), F32).at[0].set(c_ctx).at[1:1 + nb_s].set(c)
    mod = _modulation(cond, w_mod, b_mod)

    tabs_ctx = _rope_tables(8, rotate=False)
    tabs_smp = _rope_tables(x_sample.shape[1], rotate=True)
    tabs_ctx = tuple(jnp.broadcast_to(t[:1], (min(512, x_prompt.shape[1]), LANES)) for t in tabs_ctx)

    lws = [_layer_weights(l, w_in, mla_q_norm, mla_wq_b, mla_kv_norm, mla_wkv_b, gla_wg, gla_bg, gla_out_norm,
                          gdn_conv, gdn_a_log, gdn_dt_bias, gdn_out_norm, w_out, sandwich_norms, w_router,
                          w_e1, w_e3, w_e2) for l in range(depth)]

    xp = x_prompt
    ckv_l, kr_l, sg_l, sd_l = [], [], [], []
    for l in range(depth):
        modg = mod[l, 0:1].reshape(1, 6, D_MODEL)
        xp, (ckv, kr, sg, sd) = _trunk_layer(xp, lws[l], modg, None, tabs_ctx)
        ckv_l.append(ckv)
        kr_l.append(kr)
        sg_l.append(sg)
        sd_l.append(sd)

    xs = x_sample
    for l in range(depth):
        modg = mod[l, 1:1 + nb_s].reshape(nb_s, 6, D_MODEL)
        xs, _ = _trunk_layer(xs, lws[l], modg,
                             (cache_ckv[:, l], cache_krope[:, l], state_gla[:, l], state_gdn[:, l]), tabs_smp)

    return (xp, xs, jnp.stack(ckv_l, axis=1), jnp.stack(kr_l, axis=1),
            jnp.stack(sg_l, axis=1), jnp.stack(sd_l, axis=1))
```
